```python
import math, functools
import jax, jax.numpy as jnp
from jax import lax
import numpy as np

D_MODEL = 1024
BATCH = 4
SEQ = 4096
DEPTH = 2
DEC_BATCH = 32
DEC_SEQ = 4
PAST_LEN = 16384
PAGE_SIZE = 128

SSM_HEADS = 16
SSM_HEAD_DIM = 64
SSM_INNER = SSM_HEADS * SSM_HEAD_DIM
SSM_GROUPS = 2
SSM_STATE = 128
CONV_WIDTH = 4
CONV_DIM = SSM_INNER + 2 * SSM_GROUPS * SSM_STATE
SSM_CHUNK = 128
ATT_HEADS = 16
ATT_KV_HEADS = 4
ATT_HEAD_DIM = 64
ATT_WIDTH = ATT_HEADS * ATT_HEAD_DIM
KV_WIDTH = ATT_KV_HEADS * ATT_HEAD_DIM
IDX_HEADS = 16
IDX_DIM = 64
TOPK_MAX = 256
Q_BLOCK = 128
RET_HEADS = 4
RET_DK = 128
RET_DV = 256
RET_WIDTH = RET_HEADS * RET_DV
RET_CHUNK = 128
ROPE_BASE = 10000.0
N_BRANCH = 3
BRANCH_WIDTH = 1024
D_FF = 3584
N_EXPERTS = 8
TOP_K = 2
N_DENSE = (DEPTH + 1) // 2
N_MOE = DEPTH // 2
EPS = 1e-6

IN_SPLITS = (SSM_INNER, CONV_DIM, SSM_HEADS,
             ATT_WIDTH, KV_WIDTH, KV_WIDTH,
             IDX_HEADS * IDX_DIM, IDX_HEADS, IDX_DIM,
             RET_HEADS * RET_DK, RET_HEADS * RET_DK, RET_WIDTH, RET_WIDTH,
             N_BRANCH * D_MODEL)
N_IN = sum(IN_SPLITS)

kernel_name = 'hybrid_ssd_dsa_retention_decoder_step'


def _rmsnorm(x, g=None):
    xf = x.astype(jnp.float32)
    y = xf * lax.rsqrt(jnp.mean(xf * xf, axis=-1, keepdims=True) + EPS)
    if g is not None:
        y = y * g.astype(jnp.float32)
    return y.astype(x.dtype)


def _chunk_len(n, c):
    d = min(c, n)
    while n % d:
        d -= 1
    return d


def _decay_recurrence(q, k, v, log_a, s0, chunk):
    f32 = jnp.float32
    b, L, h, dk = q.shape
    dv = v.shape[-1]
    Q = _chunk_len(L, chunk)
    nc = L // Q
    qc = q.astype(f32).reshape(b, nc, Q, h, dk)
    kc = k.astype(f32).reshape(b, nc, Q, h, dk)
    vc = v.astype(f32).reshape(b, nc, Q, h, dv)
    cum = jnp.cumsum(log_a.astype(f32).reshape(b, nc, Q, h), axis=2)
    causal = jnp.tril(jnp.ones((Q, Q), dtype=bool))[None, None, :, :, None]
    diff = cum[:, :, :, None, :] - cum[:, :, None, :, :]
    decay = jnp.exp(jnp.where(causal, diff, -jnp.inf))
    scores = jnp.einsum('bcihd,bcjhd->bcijh', qc, kc) * decay
    y = jnp.einsum('bcijh,bcjhe->bcihe', scores, vc)
    tail = jnp.exp(cum[:, :, -1:, :] - cum)
    local = jnp.einsum('bcjhd,bcjhe->bchde', kc * tail[..., None], vc)
    chunk_decay = jnp.exp(cum[:, :, -1, :])

    def step(s, inp):
        loc, dec = inp
        return dec[:, :, None, None] * s + loc, s

    s_fin, s_prev = lax.scan(step, s0.astype(f32),
                             (jnp.moveaxis(local, 1, 0), jnp.moveaxis(chunk_decay, 1, 0)))
    s_prev = jnp.moveaxis(s_prev, 0, 1)
    y = y + jnp.einsum('bcihd,bchde->bcihe', qc * jnp.exp(cum)[..., None], s_prev)
    return y.reshape(b, L, h, dv), s_fin.astype(s0.dtype)


def _causal_conv(xbc, buf, w, bias):
    full = jnp.concatenate([buf.astype(xbc.dtype), xbc], axis=1)
    out = lax.conv_general_dilated(full, w[:, None, :].astype(xbc.dtype), window_strides=(1,),
                                   padding='VALID', dimension_numbers=('NWC', 'WIO', 'NWC'),
                                   feature_group_count=CONV_DIM)
    return jax.nn.silu(out + bias), full[:, full.shape[1] - (CONV_WIDTH - 1):]


def _rotate(x, pos):
    half = x.shape[-1] // 2
    inv = ROPE_BASE ** (-jnp.arange(half, dtype=jnp.float32) / half)
    ang = pos.astype(jnp.float32)[:, None] * inv[None, :]
    cos = jnp.cos(ang)[None, :, None, :]
    sin = jnp.sin(ang)[None, :, None, :]
    xf = x.astype(jnp.float32)
    x1, x2 = xf[..., :half], xf[..., half:]
    return jnp.concatenate([x1 * cos - x2 * sin, x1 * sin + x2 * cos], axis=-1).astype(x.dtype)


def _indexer_scores(iq, iw, ik, q_pos, k_pos):
    dots = jnp.einsum('bthd,bsd->bths', iq, ik).astype(jnp.float32) * IDX_DIM ** -0.5
    s = jnp.einsum('bths,bth->bts', jax.nn.relu(dots), iw.astype(jnp.float32) * IDX_HEADS ** -0.5)
    return jnp.where(k_pos[None, None, :] <= q_pos[None, :, None], s, -jnp.inf)


def _sparse_attend(q, kg, vg, valid):
    b, t = q.shape[:2]
    qg = q.reshape(b, t, ATT_KV_HEADS, ATT_HEADS // ATT_KV_HEADS, ATT_HEAD_DIM)
    s = jnp.einsum('btngd,btknd->btngk', qg, kg).astype(jnp.float32) * ATT_HEAD_DIM ** -0.5
    s = jnp.where(valid[:, :, None, None, :], s, -jnp.inf)
    p = jax.nn.softmax(s, axis=-1).astype(vg.dtype)
    o = jnp.einsum('btngk,btknd->btngd', p, vg)
    return o.reshape(b, t, ATT_WIDTH)


def _dsa_prompt(q, k, v, iq, iw, ik):
    b, s = q.shape[:2]
    blk = _chunk_len(s, Q_BLOCK)
    nb = s // blk
    k_sel = max(1, min(TOPK_MAX, s // 4))
    k_pos = jnp.arange(s)
    bidx = jnp.arange(b)[:, None, None]

    def to_blocks(a):
        return jnp.moveaxis(a.reshape((b, nb, blk) + a.shape[2:]), 1, 0)

    def one(args):
        qb, iqb, iwb, start = args
        q_pos = start + jnp.arange(blk)
        sc = _indexer_scores(iqb, iwb, ik, q_pos, k_pos)
        top, idx = lax.top_k(sc, k_sel)
        return _sparse_attend(qb, k[bidx, idx], v[bidx, idx], jnp.isfinite(top))

    out = lax.map(one, (to_blocks(q), to_blocks(iq), to_blocks(iw), jnp.arange(nb) * blk))
    return jnp.moveaxis(out, 0, 1).reshape(b, s, ATT_WIDTH)


def _dsa_sample(q, k, v, iq, iw, ik, ck, cv, cik, page_table):
    b, t = q.shape[:2]
    past = page_table.shape[1] * PAGE_SIZE
    L = past + t
    k_sel = max(1, min(TOPK_MAX, L // 4))
    ik_past = cik[page_table].reshape(b, past, IDX_DIM)
    ik_all = jnp.concatenate([ik_past.astype(ik.dtype), ik], axis=1)
    q_pos = past + jnp.arange(t)
    sc = _indexer_scores(iq, iw, ik_all, q_pos, jnp.arange(L))
    top, idx = lax.top_k(sc, k_sel)
    bidx = jnp.arange(b)[:, None, None]
    in_past = (idx < past)[..., None, None]
    pidx = jnp.minimum(idx, past - 1)
    phys = page_table[bidx, pidx // PAGE_SIZE]
    off = pidx % PAGE_SIZE
    nidx = jnp.clip(idx - past, 0, t - 1)
    kg = jnp.where(in_past, ck[phys, off].astype(k.dtype), k[bidx, nidx])
    vg = jnp.where(in_past, cv[phys, off].astype(v.dtype), v[bidx, nidx])
    return _sparse_attend(q, kg, vg, jnp.isfinite(top))


def _mixers(h, pos0, conv_buf, ssm0, ret0, attend, lp):
    norm_g, w_in, conv_w, conv_b, dt_bias, a_log, d_skip, ssm_norm_g, w_branch, w_out = lp
    f32 = jnp.float32
    b, L, _ = h.shape
    xn = _rmsnorm(h, norm_g)
    proj = xn @ w_in
    (z, xbc, dt_raw, q, k, v, iq, iw, ik, rq, rk, rv, rg, gates) = jnp.split(
        proj, np.cumsum(IN_SPLITS)[:-1].tolist(), axis=-1)

    xbc, conv_new = _causal_conv(xbc, conv_buf, conv_w, conv_b)
    xs, bs, cs = jnp.split(xbc, [SSM_INNER, SSM_INNER + SSM_GROUPS * SSM_STATE], axis=-1)
    xs = xs.reshape(b, L, SSM_HEADS, SSM_HEAD_DIM)
    rep = SSM_HEADS // SSM_GROUPS
    bs = jnp.repeat(bs.reshape(b, L, SSM_GROUPS, SSM_STATE), rep, axis=2)
    cs = jnp.repeat(cs.reshape(b, L, SSM_GROUPS, SSM_STATE), rep, axis=2)
    dt = jax.nn.softplus(dt_raw.astype(f32) + dt_bias.astype(f32))
    log_a = dt * (-jnp.exp(a_log.astype(f32)))
    ys, ssm_new = _decay_recurrence(cs, bs, xs.astype(f32) * dt[..., None], log_a, ssm0, SSM_CHUNK)
    ys = ys.astype(h.dtype) + d_skip[:, None] * xs
    ys = ys.reshape(b, L, SSM_INNER) * jax.nn.silu(z)
    ys = _rmsnorm(ys.reshape(b, L, SSM_GROUPS, SSM_INNER // SSM_GROUPS),
                  ssm_norm_g.reshape(SSM_GROUPS, SSM_INNER // SSM_GROUPS)).reshape(b, L, SSM_INNER)

    k = k.reshape(b, L, ATT_KV_HEADS, ATT_HEAD_DIM)
    v = v.reshape(b, L, ATT_KV_HEADS, ATT_HEAD_DIM)
    ya = attend(q.reshape(b, L, ATT_HEADS, ATT_HEAD_DIM), k, v,
                iq.reshape(b, L, IDX_HEADS, IDX_DIM), iw, ik)

    pos = pos0 + jnp.arange(L)
    rq = _rotate(rq.reshape(b, L, RET_HEADS, RET_DK), pos)
    rk = _rotate(rk.reshape(b, L, RET_HEADS, RET_DK), pos) * RET_DK ** -0.5
    rv = rv.reshape(b, L, RET_HEADS, RET_DV)
    log_g = jnp.log1p(-jnp.exp2(-5.0 - jnp.arange(RET_HEADS, dtype=f32)))
    yr, ret_new = _decay_recurrence(rq, rk, rv, jnp.broadcast_to(log_g, (b, L, RET_HEADS)), ret0, RET_CHUNK)
    yr = _rmsnorm(yr.astype(h.dtype)).reshape(b, L, RET_WIDTH) * jax.nn.silu(rg)

    branches = jnp.stack([ys, ya, yr], axis=2)
    proj_b = jnp.einsum('blnw,nwd->blnd', branches, w_branch)
    g = jax.nn.sigmoid(gates.reshape(b, L, N_BRANCH, D_MODEL).astype(f32)).astype(h.dtype)
    out = jnp.sum(g * proj_b, axis=2) @ w_out
    return h + out, (k, v, ik, ssm_new, conv_new, ret_new)


def _swiglu(x, wg, wu, wd):
    return (jax.nn.silu(x @ wg) * (x @ wu)) @ wd


def _moe(x, wr, wg, wu, wd):
    logits = (x @ wr).astype(jnp.float32)
    top_v, top_i = lax.top_k(logits, TOP_K)
    w = jax.nn.softmax(top_v, axis=-1)
    gate = jnp.sum(jax.nn.one_hot(top_i, N_EXPERTS, dtype=jnp.float32) * w[..., None], axis=-2).astype(x.dtype)
    out = jnp.zeros_like(x)
    for e in range(N_EXPERTS):
        out = out + gate[..., e:e + 1] * _swiglu(x, wg[e], wu[e], wd[e])
    return out


def _channel_mixer(h, l, norm_ffn_g, w_ffn_gate, w_ffn_up, w_ffn_down, w_router, w_moe_gate, w_moe_up, w_moe_down):
    xn = _rmsnorm(h, norm_ffn_g[l])
    j = l // 2
    if l % 2 == 0:
        return h + _swiglu(xn, w_ffn_gate[j], w_ffn_up[j], w_ffn_down[j])
    return h + _moe(xn, w_router[j], w_moe_gate[j], w_moe_up[j], w_moe_down[j])


def setup_inputs(seed: int = 0) -> dict:
    key = jax.random.key(seed)
    ks = iter(jax.random.split(key, 48))
    f32 = jnp.float32
    n_pages = PAST_LEN // PAGE_SIZE
    n_used = DEC_BATCH * n_pages
    n_phys = (5 * n_used + 3) // 4

    def nrm(shape, scale):
        return jax.random.normal(next(ks), shape, f32) * scale

    x_prompt = nrm((BATCH, SEQ, D_MODEL), 1.0)
    x_sample = nrm((DEC_BATCH, DEC_SEQ, D_MODEL), 1.0)
    cache_k = nrm((DEPTH, n_phys, PAGE_SIZE, ATT_KV_HEADS, ATT_HEAD_DIM), 1.0)
    cache_v = nrm((DEPTH, n_phys, PAGE_SIZE, ATT_KV_HEADS, ATT_HEAD_DIM), 1.0)
    cache_idx_k = nrm((DEPTH, n_phys, PAGE_SIZE, IDX_DIM), 1.0)
    state_ssm = nrm((DEPTH, DEC_BATCH, SSM_HEADS, SSM_STATE, SSM_HEAD_DIM), 0.1)
    state_conv = nrm((DEPTH, DEC_BATCH, CONV_WIDTH - 1, CONV_DIM), 1.0)
    state_ret = nrm((DEPTH, DEC_BATCH, RET_HEADS, RET_DK, RET_DV), 0.3)
    perm = jax.random.permutation(next(ks), n_phys)
    page_table = perm[:n_used].reshape(DEC_BATCH, n_pages).astype(jnp.int32)

    norm_mix_g = 1.0 + nrm((DEPTH, D_MODEL), 0.01)
    w_in = nrm((DEPTH, D_MODEL, N_IN), D_MODEL ** -0.5)
    conv_w = nrm((DEPTH, CONV_WIDTH, CONV_DIM), CONV_WIDTH ** -0.5)
    conv_b = nrm((DEPTH, CONV_DIM), 0.01)
    dt0 = jnp.exp(jax.random.uniform(next(ks), (DEPTH, SSM_HEADS), f32, math.log(1e-3), math.log(1e-1)))
    dt_bias = dt0 + jnp.log(-jnp.expm1(-dt0))
    a_log = jnp.log(jax.random.uniform(next(ks), (DEPTH, SSM_HEADS), f32, 1.0, 16.0))
    d_skip = 1.0 + nrm((DEPTH, SSM_HEADS), 0.01)
    ssm_norm_g = 1.0 + nrm((DEPTH, SSM_INNER), 0.01)
    w_branch = nrm((DEPTH, N_BRANCH, BRANCH_WIDTH, D_MODEL), BRANCH_WIDTH ** -0.5)
    w_out = nrm((DEPTH, D_MODEL, D_MODEL), D_MODEL ** -0.5)
    norm_ffn_g = 1.0 + nrm((DEPTH, D_MODEL), 0.01)
    w_ffn_gate = nrm((N_DENSE, D_MODEL, D_FF), D_MODEL ** -0.5)
    w_ffn_up = nrm((N_DENSE, D_MODEL, D_FF), D_MODEL ** -0.5)
    w_ffn_down = nrm((N_DENSE, D_FF, D_MODEL), D_FF ** -0.5)
    w_router = nrm((N_MOE, D_MODEL, N_EXPERTS), D_MODEL ** -0.5)
    w_moe_gate = nrm((N_MOE, N_EXPERTS, D_MODEL, D_FF), D_MODEL ** -0.5)
    w_moe_up = nrm((N_MOE, N_EXPERTS, D_MODEL, D_FF), D_MODEL ** -0.5)
    w_moe_down = nrm((N_MOE, N_EXPERTS, D_FF, D_MODEL), D_FF ** -0.5)
    final_norm_g = 1.0 + nrm((D_MODEL,), 0.01)
    return {'x_prompt': x_prompt, 'x_sample': x_sample,
            'cache_k': cache_k, 'cache_v': cache_v, 'cache_idx_k': cache_idx_k,
            'state_ssm': state_ssm, 'state_conv': state_conv, 'state_ret': state_ret,
            'page_table': page_table,
            'norm_mix_g': norm_mix_g, 'w_in': w_in, 'conv_w': conv_w, 'conv_b': conv_b,
            'dt_bias': dt_bias, 'a_log': a_log, 'd_skip': d_skip, 'ssm_norm_g': ssm_norm_g,
            'w_branch': w_branch, 'w_out': w_out, 'norm_ffn_g': norm_ffn_g,
            'w_ffn_gate': w_ffn_gate, 'w_ffn_up': w_ffn_up, 'w_ffn_down': w_ffn_down,
            'w_router': w_router, 'w_moe_gate': w_moe_gate, 'w_moe_up': w_moe_up,
            'w_moe_down': w_moe_down, 'final_norm_g': final_norm_g}


def reference(x_prompt, x_sample, cache_k, cache_v, cache_idx_k, state_ssm, state_conv, state_ret,
              page_table, norm_mix_g, w_in, conv_w, conv_b, dt_bias, a_log, d_skip, ssm_norm_g,
              w_branch, w_out, norm_ffn_g, w_ffn_gate, w_ffn_up, w_ffn_down, w_router,
              w_moe_gate, w_moe_up, w_moe_down, final_norm_g):
    past = page_table.shape[1] * PAGE_SIZE
    bp = x_prompt.shape[0]
    dt_ = x_prompt.dtype
    hp, hs = x_prompt, x_sample
    st_p, st_s = [], []
    for l in range(DEPTH):
        lp = (norm_mix_g[l], w_in[l], conv_w[l], conv_b[l], dt_bias[l], a_log[l], d_skip[l],
              ssm_norm_g[l], w_branch[l], w_out[l])
        hp, sp = _mixers(hp, 0,
                         jnp.zeros((bp, CONV_WIDTH - 1, CONV_DIM), dt_),
                         jnp.zeros((bp, SSM_HEADS, SSM_STATE, SSM_HEAD_DIM), dt_),
                         jnp.zeros((bp, RET_HEADS, RET_DK, RET_DV), dt_),
                         _dsa_prompt, lp)
        attend_s = functools.partial(_dsa_sample, ck=cache_k[l], cv=cache_v[l], cik=cache_idx_k[l],
                                     page_table=page_table)
        hs, ss = _mixers(hs, past, state_conv[l], state_ssm[l], state_ret[l], attend_s, lp)
        hp = _channel_mixer(hp, l, norm_ffn_g, w_ffn_gate, w_ffn_up, w_ffn_down, w_router,
                            w_moe_gate, w_moe_up, w_moe_down)
        hs = _channel_mixer(hs, l, norm_ffn_g, w_ffn_gate, w_ffn_up, w_ffn_down, w_router,
                            w_moe_gate, w_moe_up, w_moe_down)
        st_p.append(sp)
        st_s.append(ss)
    y_prompt = _rmsnorm(hp, final_norm_g)
    y_sample = _rmsnorm(hs, final_norm_g)
    k_prompt = jnp.stack([s[0] for s in st_p])
    v_prompt = jnp.stack([s[1] for s in st_p])
    idxk_prompt = jnp.stack([s[2] for s in st_p])
    ssm_prompt = jnp.stack([s[3] for s in st_p])
    conv_prompt = jnp.stack([s[4] for s in st_p])
    ret_prompt = jnp.stack([s[5] for s in st_p])
    k_sample = jnp.stack([s[0] for s in st_s])
    v_sample = jnp.stack([s[1] for s in st_s])
    idxk_sample = jnp.stack([s[2] for s in st_s])
    ssm_sample = jnp.stack([s[3] for s in st_s])
    conv_sample = jnp.stack([s[4] for s in st_s])
    ret_sample = jnp.stack([s[5] for s in st_s])
    return (y_prompt, y_sample, k_prompt, v_prompt, idxk_prompt, ssm_prompt, conv_prompt, ret_prompt,
            k_sample, v_sample, idxk_sample, ssm_sample, conv_sample, ret_sample)
```

```python
import functools
import math

import jax
import jax.numpy as jnp
import numpy as np
from jax import lax
from jax.experimental import pallas as pl
from jax.experimental.pallas import tpu as pltpu

F32 = jnp.float32
BF16 = jnp.bfloat16
I32 = jnp.int32

D_MODEL = 1024
PAGE_SIZE = 128
SSM_HEADS = 16
SSM_HEAD_DIM = 64
SSM_INNER = 1024
SSM_GROUPS = 2
SSM_STATE = 128
CONV_WIDTH = 4
CONV_DIM = 1536
ATT_HEADS = 16
ATT_KV_HEADS = 4
ATT_HEAD_DIM = 64
KV_WIDTH = 256
IDX_HEADS = 16
IDX_DIM = 64
TOPK_MAX = 256
RET_HEADS = 4
RET_DK = 128
RET_DV = 256
ROPE_BASE = 10000.0
N_BRANCH = 3
D_FF = 3584
N_EXPERTS = 8
EPS = 1e-6
IN_SPLITS = (1024, 1536, 16, 1024, 256, 256, 1024, 16, 64, 512, 512, 1024, 1024, 3072)

LANES = 128
SUBLANES = 8
VMEM_LIMIT = 48 * 1024 * 1024

CHUNK = 128
INT_MIN = -(2 ** 31)
KEY_POS_INF = 0x7F800000
KEY_NEG_INF = -2139095041
NEG_BIG = -1e30


def _cparams(*sem):
    return pltpu.CompilerParams(dimension_semantics=sem, vmem_limit_bytes=VMEM_LIMIT)


def _dot(a, b):
    return jnp.dot(a, b, preferred_element_type=F32)


def _dot_nt(a, b):
    return lax.dot_general(a, b, (((1,), (1,)), ((), ())), preferred_element_type=F32)


def _dot_tn(a, b):
    return lax.dot_general(a, b, (((0,), (0,)), ((), ())), preferred_element_type=F32)


def _split3(x):
    hi = x.astype(BF16)
    r = x - hi.astype(F32)
    mid = r.astype(BF16)
    lo = (r - mid.astype(F32)).astype(BF16)
    return hi, mid, lo


def _silu(x):
    return x * jax.nn.sigmoid(x)


def _softplus(x):
    return jnp.maximum(x, 0.0) + jnp.log1p(jnp.exp(-jnp.abs(x)))


def _float_key(x):
    x = jnp.where(x == 0.0, 0.0, x)
    b = lax.bitcast_convert_type(x, I32)
    return jnp.where(b >= 0, b, b ^ jnp.int32(0x7FFFFFFF))


def _rms_matmul_kernel(x_ref, g_ref, w_ref, o_ref, xn_ref):
    @pl.when(pl.program_id(1) == 0)
    def _():
        x = x_ref[...]
        ms = jnp.mean(x * x, axis=-1, keepdims=True)
        xn_ref[...] = (x * lax.rsqrt(ms + EPS) * g_ref[...]).astype(BF16)

    o_ref[...] = _dot(xn_ref[...], w_ref[...]).astype(o_ref.dtype)


def _pick(n, prefs):
    for p in prefs:
        if n % p == 0:
            return p
    return n


def rms_matmul(x, g, w, out_dtype=F32):
    M, K = x.shape
    N = w.shape[1]
    tm = _pick(M, (1024, 512, 256, 128))
    tn = _pick(N, (512, 384, 256, 128))
    return pl.pallas_call(
        _rms_matmul_kernel,
        grid=(M // tm, N // tn),
        in_specs=[pl.BlockSpec((tm, K), lambda i, j: (i, 0)),
                  pl.BlockSpec((1, K), lambda i, j: (0, 0)),
                  pl.BlockSpec((K, tn), lambda i, j: (0, j))],
        out_specs=pl.BlockSpec((tm, tn), lambda i, j: (i, j)),
        out_shape=jax.ShapeDtypeStruct((M, N), out_dtype),
        scratch_shapes=[pltpu.VMEM((tm, K), BF16)],
        compiler_params=_cparams("parallel", "arbitrary"),
        name="rms_matmul",
    )(x, g.reshape(1, K), w)


def _ssd_kernel(zxd_ref, dtT_ref, conv0_ref, s0_ref, cw_ref, cb_ref, dtb_ref, alog_ref, dsk_ref,
                ng_ref, dtbT_ref, alogT_ref, y_ref, convn_ref, sn_ref, xpad_ref, st_ref,
                *, Q, last_valid, nc):
    c = pl.program_id(1)
    GW = SSM_INNER // SSM_GROUPS

    @pl.when(c == 0)
    def _():
        xpad_ref[0:8, :] = jnp.zeros((8, CONV_DIM), F32)
        xpad_ref[5:8, :] = conv0_ref[0]
        st_ref[...] = s0_ref[0]

    blk = zxd_ref[0]
    z = blk[:, :SSM_INNER]
    xbc = blk[:, SSM_INNER:SSM_INNER + CONV_DIM]
    dtr = blk[:, SSM_INNER + CONV_DIM:]
    xpad_ref[8:8 + Q, :] = xbc
    cw = cw_ref[...]
    conv = (xpad_ref[5:5 + Q, :] * cw[0:1] + xpad_ref[6:6 + Q, :] * cw[1:2]
            + xpad_ref[7:7 + Q, :] * cw[2:3] + xbc * cw[3:4]) + cb_ref[...]
    xc = _silu(conv)
    xs = xc[:, :SSM_INNER]
    Bm = xc[:, SSM_INNER:SSM_INNER + SSM_GROUPS * SSM_STATE]
    Cm = xc[:, SSM_INNER + SSM_GROUPS * SSM_STATE:]

    row = lax.broadcasted_iota(I32, (Q, 1), 0)
    colq = lax.broadcasted_iota(I32, (1, Q), 1)
    assert last_valid == Q or nc == 1
    lv = last_valid
    valid = row < lv
    tril = (lax.broadcasted_iota(I32, (Q, Q), 0) >= lax.broadcasted_iota(I32, (Q, Q), 1))
    tril_b = jnp.where(tril, 1.0, 0.0).astype(BF16)
    triu_b = jnp.where(lax.broadcasted_iota(I32, (Q, Q), 0) <= lax.broadcasted_iota(I32, (Q, Q), 1),
                       1.0, 0.0).astype(BF16)

    nega = -jnp.exp(alog_ref[...])
    dt = _softplus(dtr + dtb_ref[...])
    la = jnp.where(valid, dt * nega, 0.0)
    cum = sum(_dot(tril_b, p) for p in _split3(la))
    laT = jnp.where(colq < lv, _softplus(dtT_ref[0] + dtbT_ref[...]) * (-jnp.exp(alogT_ref[...])), 0.0)
    cumT = sum(_dot(p, triu_b) for p in _split3(laT))

    ecum = jnp.exp(cum)
    cl = cum[lv - 1:lv, :]
    xdt = xs * dt
    xtail = jnp.where(valid, xdt * jnp.exp(cl - cum), 0.0)
    cdecay = jnp.exp(cl)

    lane = lax.broadcasted_iota(I32, (1, LANES), 1)
    y_groups = []
    for g in range(SSM_GROUPS):
        l0 = g * GW
        Cg = Cm[:, g * SSM_STATE:(g + 1) * SSM_STATE].astype(BF16)
        Bg = Bm[:, g * SSM_STATE:(g + 1) * SSM_STATE].astype(BF16)
        G = _dot_nt(Cg, Bg)
        st_g = st_ref[:, l0:l0 + GW]
        inter = _dot(Cg, st_g.astype(BF16)) * ecum[:, l0:l0 + GW]
        local = _dot_tn(Bg, xtail[:, l0:l0 + GW].astype(BF16))
        st_ref[:, l0:l0 + GW] = st_g * cdecay[:, l0:l0 + GW] + local
        pairs = []
        for p in range(GW // LANES):
            xp = xdt[:, l0 + p * LANES:l0 + (p + 1) * LANES]
            acc = None
            for hh in range(2):
                h = (l0 + p * LANES) // SSM_HEAD_DIM + hh
                ccol = cum[:, h * SSM_HEAD_DIM:h * SSM_HEAD_DIM + 1]
                diff = ccol - cumT[h:h + 1, :]
                dm = jnp.exp(jnp.where(tril, diff, -jnp.inf))
                s = (G * dm).astype(BF16)
                half = (lane >= hh * SSM_HEAD_DIM) & (lane < (hh + 1) * SSM_HEAD_DIM)
                part = _dot(s, jnp.where(half, xp, 0.0).astype(BF16))
                acc = part if acc is None else acc + part
            pairs.append(acc)
        y_groups.append(jnp.concatenate(pairs, axis=1) + inter)
    y = jnp.concatenate(y_groups, axis=1)

    y = (y + dsk_ref[...] * xs) * _silu(z)
    outs = []
    for g in range(SSM_GROUPS):
        seg = y[:, g * GW:(g + 1) * GW]
        ms = jnp.mean(seg * seg, axis=-1, keepdims=True)
        outs.append(seg * lax.rsqrt(ms + EPS) * ng_ref[:, g * GW:(g + 1) * GW])
    y_ref[0] = jnp.concatenate(outs, axis=1)

    @pl.when(c == nc - 1)
    def _():
        convn_ref[0] = xpad_ref[5 + last_valid:8 + last_valid, :]
        sn_ref[0] = st_ref[...]

    xpad_ref[0:8, :] = xpad_ref[Q:Q + 8, :]


def ssd_branch(zxd, dtT, conv0, s0, conv_w, conv_b, dt_bias, a_log, d_skip, norm_g, last_valid):
    b, L, W = zxd.shape
    Q = CHUNK
    nc = L // Q
    rep = lambda v: jnp.repeat(v.astype(F32), SSM_HEAD_DIM).reshape(1, SSM_INNER)
    col = lambda v: v.astype(F32).reshape(SSM_HEADS, 1)
    full = lambda shape: pl.BlockSpec(shape, lambda i, c: (0,) * len(shape))
    kern = functools.partial(_ssd_kernel, Q=Q, last_valid=last_valid, nc=nc)
    return pl.pallas_call(
        kern,
        grid=(b, nc),
        in_specs=[pl.BlockSpec((1, Q, W), lambda i, c: (i, c, 0)),
                  pl.BlockSpec((1, SSM_HEADS, Q), lambda i, c: (i, 0, c)),
                  pl.BlockSpec((1, CONV_WIDTH - 1, CONV_DIM), lambda i, c: (i, 0, 0)),
                  pl.BlockSpec((1, SSM_STATE, SSM_INNER), lambda i, c: (i, 0, 0)),
                  full((CONV_WIDTH, CONV_DIM)), full((1, CONV_DIM)),
                  full((1, SSM_INNER)), full((1, SSM_INNER)), full((1, SSM_INNER)), full((1, SSM_INNER)),
                  full((SSM_HEADS, 1)), full((SSM_HEADS, 1))],
        out_specs=[pl.BlockSpec((1, Q, SSM_INNER), lambda i, c: (i, c, 0)),
                   pl.BlockSpec((1, CONV_WIDTH - 1, CONV_DIM), lambda i, c: (i, 0, 0)),
                   pl.BlockSpec((1, SSM_STATE, SSM_INNER), lambda i, c: (i, 0, 0))],
        out_shape=[jax.ShapeDtypeStruct((b, L, SSM_INNER), F32),
                   jax.ShapeDtypeStruct((b, CONV_WIDTH - 1, CONV_DIM), F32),
                   jax.ShapeDtypeStruct((b, SSM_STATE, SSM_INNER), F32)],
        scratch_shapes=[pltpu.VMEM((Q + 8, CONV_DIM), F32), pltpu.VMEM((SSM_STATE, SSM_INNER), F32)],
        compiler_params=_cparams("parallel", "arbitrary"),
        name="ssd_branch",
    )(zxd, dtT, conv0, s0, conv_w.astype(F32), conv_b.reshape(1, CONV_DIM).astype(F32),
      rep(dt_bias), rep(a_log), rep(d_skip), norm_g.reshape(1, SSM_INNER).astype(F32),
      col(dt_bias), col(a_log))


def _ret_kernel(x_ref, cos_ref, sin_ref, r0_ref, y_ref, rn_ref, st_ref, *, Q, last_valid, nc):
    c = pl.program_id(1)

    @pl.when(c == 0)
    def _():
        st_ref[...] = r0_ref[0]

    blk = x_ref[0]
    cos2 = cos_ref[...]
    sin2 = sin_ref[...]
    ri = lax.broadcasted_iota(I32, (Q, Q), 0)
    ci = lax.broadcasted_iota(I32, (Q, Q), 1)
    dij = (ri - ci).astype(F32)
    row = lax.broadcasted_iota(I32, (Q, 1), 0)
    assert last_valid == Q or nc == 1
    lv = last_valid
    rowf = row.astype(F32)
    lvf = float(lv)
    KO = RET_HEADS * RET_DK
    outs = []
    for h in range(RET_HEADS):
        lg = math.log1p(-2.0 ** (-5.0 - h))
        qh = blk[:, h * RET_DK:(h + 1) * RET_DK]
        kh = blk[:, KO + h * RET_DK:KO + (h + 1) * RET_DK]
        vh = blk[:, 2 * KO + h * RET_DV:2 * KO + (h + 1) * RET_DV].astype(BF16)
        gh = blk[:, 2 * KO + RET_HEADS * RET_DV + h * RET_DV:2 * KO + RET_HEADS * RET_DV + (h + 1) * RET_DV]
        qr = qh * cos2 + pltpu.roll(qh, RET_DK // 2, 1) * sin2
        kr = (kh * cos2 + pltpu.roll(kh, RET_DK // 2, 1) * sin2) * (RET_DK ** -0.5)
        qb = qr.astype(BF16)
        dm = jnp.exp(jnp.where(ri >= ci, dij * lg, -jnp.inf))
        s = (_dot_nt(qb, kr.astype(BF16)) * dm).astype(BF16)
        st_h = st_ref[h]
        y = _dot(s, vh) + jnp.exp((rowf + 1.0) * lg) * _dot(qb, st_h.astype(BF16))
        ktail = jnp.where(row < lv, kr * jnp.exp((lvf - 1.0 - rowf) * lg), 0.0)
        st_ref[h] = st_h * math.exp(lvf * lg) + _dot_tn(ktail.astype(BF16), vh)
        ms = jnp.mean(y * y, axis=-1, keepdims=True)
        outs.append(y * lax.rsqrt(ms + EPS) * _silu(gh))
    y_ref[0] = jnp.concatenate(outs, axis=1)

    @pl.when(c == nc - 1)
    def _():
        rn_ref[0] = st_ref[...]


def ret_branch(x, cos2, sin2, r0, last_valid):
    b, L, W = x.shape
    Q = CHUNK
    nc = L // Q
    kern = functools.partial(_ret_kernel, Q=Q, last_valid=last_valid, nc=nc)
    return pl.pallas_call(
        kern,
        grid=(b, nc),
        in_specs=[pl.BlockSpec((1, Q, W), lambda i, c: (i, c, 0)),
                  pl.BlockSpec((Q, RET_DK), lambda i, c: (c, 0)),
                  pl.BlockSpec((Q, RET_DK), lambda i, c: (c, 0)),
                  pl.BlockSpec((1, RET_HEADS, RET_DK, RET_DV), lambda i, c: (i, 0, 0, 0))],
        out_specs=[pl.BlockSpec((1, Q, RET_HEADS * RET_DV), lambda i, c: (i, c, 0)),
                   pl.BlockSpec((1, RET_HEADS, RET_DK, RET_DV), lambda i, c: (i, 0, 0, 0))],
        out_shape=[jax.ShapeDtypeStruct((b, L, RET_HEADS * RET_DV), F32),
                   jax.ShapeDtypeStruct((b, RET_HEADS, RET_DK, RET_DV), F32)],
        scratch_shapes=[pltpu.VMEM((RET_HEADS, RET_DK, RET_DV), F32)],
        compiler_params=_cparams("parallel", "arbitrary"),
        name="ret_branch",
    )(x, cos2, sin2, r0)


def _kth_largest_key(count_ge, k, shape):
    def body(t, prefix):
        bit = lax.shift_left(jnp.int32(1), jnp.int32(31) - t)
        cand = prefix | bit
        cnt = count_ge(cand ^ jnp.int32(INT_MIN))
        return jnp.where(cnt >= k, cand, prefix)

    prefix = lax.fori_loop(0, 32, body, jnp.zeros(shape, I32))
    return prefix ^ jnp.int32(INT_MIN)


def _dsa_prompt_kernel(q_ref, iq_ref, iw_ref, ik_ref, k_ref, v_ref, o_ref, keys_ref, jb_ref,
                       *, tq, ck, L, k_sel):
    i = pl.program_id(1)
    nk = lax.div((i + 1) * tq + (ck - 1), ck)
    rowg = i * tq + lax.broadcasted_iota(I32, (tq, 1), 0)
    coli = lax.broadcasted_iota(I32, (1, ck), 1)
    iw = iw_ref[0][:, IDX_HEADS:2 * IDX_HEADS] * (IDX_HEADS ** -0.5)

    def score_body(kc, carry):
        off = pl.multiple_of(kc * ck, ck)
        ikc = ik_ref[0, pl.ds(off, ck), :]
        acc = jnp.zeros((tq, ck), F32)
        for h in range(IDX_HEADS):
            d = _dot_nt(iq_ref[0, h], ikc) * (IDX_DIM ** -0.5)
            acc = acc + jnp.maximum(d, 0.0) * iw[:, h:h + 1]
        key = jnp.where(off + coli <= rowg, _float_key(acc), jnp.int32(INT_MIN))
        keys_ref[:, pl.ds(off, ck)] = key
        return carry

    lax.fori_loop(0, nk, score_body, 0)

    def count(pred):
        def body(kc, cnt):
            off = pl.multiple_of(kc * ck, ck)
            kk = keys_ref[:, pl.ds(off, ck)]
            return cnt + jnp.sum(jnp.where(pred(kk, off + coli), 1.0, 0.0), axis=1, keepdims=True)
        return lax.fori_loop(0, nk, body, jnp.zeros((tq, 1), F32))

    kf = float(k_sel)
    thr = _kth_largest_key(lambda cand: count(lambda kk, col: kk >= cand), kf, (tq, 1))
    thr = jnp.maximum(thr, jnp.int32(KEY_NEG_INF + 1))
    n_ge = count(lambda kk, col: kk >= thr)
    jb_ref[...] = jnp.full((tq, 1), L, I32)

    @pl.when(jnp.max(n_ge) > kf)
    def _():
        need = kf - count(lambda kk, col: kk > thr)
        nb = max(1, (L - 1).bit_length())

        def body(t, pfx):
            cand = pfx | lax.shift_left(jnp.int32(1), jnp.int32(nb - 1) - t)
            cnt = count(lambda kk, col: (kk == thr) & (col < cand))
            return jnp.where(cnt < need, cand, pfx)

        pfx = lax.fori_loop(0, nb, body, jnp.zeros((tq, 1), I32))
        jb_ref[...] = jnp.where(n_ge > kf, pfx + 1, L)

    jb = jb_ref[...]

    G = ATT_HEADS // ATT_KV_HEADS
    init = tuple((jnp.full((G * tq, 1), NEG_BIG, F32), jnp.zeros((G * tq, 1), F32),
                  jnp.zeros((G * tq, ATT_HEAD_DIM), F32)) for _ in range(ATT_KV_HEADS))

    def att_body(kc, carry):
        off = pl.multiple_of(kc * ck, ck)
        kk = keys_ref[:, pl.ds(off, ck)]
        sel = (kk >= thr) & (kk < jnp.int32(KEY_POS_INF)) & ((kk > thr) | (off + coli < jb))
        bias = jnp.where(sel, 0.0, -jnp.inf)
        new = []
        for g in range(ATT_KV_HEADS):
            m, l, acc = carry[g]
            qg = q_ref[0, g * G:(g + 1) * G].reshape(G * tq, ATT_HEAD_DIM)
            s = _dot_nt(qg, k_ref[0, g, pl.ds(off, ck), :]) * (ATT_HEAD_DIM ** -0.5)
            s = (s.reshape(G, tq, ck) + bias[None]).reshape(G * tq, ck)
            m_new = jnp.maximum(m, jnp.max(s, axis=1, keepdims=True))
            alpha = jnp.exp(m - m_new)
            p = jnp.exp(s - m_new)
            l = alpha * l + jnp.sum(p, axis=1, keepdims=True)
            acc = alpha * acc + _dot(p.astype(BF16), v_ref[0, g, pl.ds(off, ck), :])
            new.append((m_new, l, acc))
        return tuple(new)

    fin = lax.fori_loop(0, nk, att_body, init)
    for g in range(ATT_KV_HEADS):
        m, l, acc = fin[g]
        o_ref[0, g * G:(g + 1) * G] = (acc / l).reshape(G, tq, ATT_HEAD_DIM)


def dsa_prompt(q_hm, iq_hm, small, ik, k_hm, v_hm):
    b, _, L, _ = q_hm.shape
    tq = min(128, L)
    ck = min(512, L)
    k_sel = max(1, min(TOPK_MAX, L // 4))
    kern = functools.partial(_dsa_prompt_kernel, tq=tq, ck=ck, L=L, k_sel=k_sel)
    return pl.pallas_call(
        kern,
        grid=(b, L // tq),
        in_specs=[pl.BlockSpec((1, ATT_HEADS, tq, ATT_HEAD_DIM), lambda bi, i: (bi, 0, i, 0)),
                  pl.BlockSpec((1, IDX_HEADS, tq, IDX_DIM), lambda bi, i: (bi, 0, i, 0)),
                  pl.BlockSpec((1, tq, LANES), lambda bi, i: (bi, i, 0)),
                  pl.BlockSpec((1, L, IDX_DIM), lambda bi, i: (bi, 0, 0)),
                  pl.BlockSpec((1, ATT_KV_HEADS, L, ATT_HEAD_DIM), lambda bi, i: (bi, 0, 0, 0)),
                  pl.BlockSpec((1, ATT_KV_HEADS, L, ATT_HEAD_DIM), lambda bi, i: (bi, 0, 0, 0))],
        out_specs=pl.BlockSpec((1, ATT_HEADS, tq, ATT_HEAD_DIM), lambda bi, i: (bi, 0, i, 0)),
        out_shape=jax.ShapeDtypeStruct((b, ATT_HEADS, L, ATT_HEAD_DIM), F32),
        scratch_shapes=[pltpu.VMEM((tq, L), I32), pltpu.VMEM((tq, 1), I32)],
        compiler_params=_cparams("parallel", "arbitrary"),
        name="dsa_prompt",
    )(q_hm, iq_hm, small, ik, k_hm, v_hm)


def _sample_scores(iq_ref, iww_ref, keys_bf16):
    n = keys_bf16.shape[0]
    d = _dot_nt(iq_ref[0], keys_bf16) * (IDX_DIM ** -0.5)
    r = jnp.maximum(d, 0.0) * (iww_ref[0][:, 0:1] * (IDX_HEADS ** -0.5))
    return jnp.sum(r.reshape(-1, IDX_HEADS, n), axis=1)


def _dsa_sample_scores_kernel(pt_ref, cik_ref, iq_ref, iww_ref, keys_ref):
    sc = _sample_scores(iq_ref, iww_ref, cik_ref[0].astype(BF16))
    keys_ref[0] = _float_key(sc)


def _dsa_sample_thr_kernel(kp_ref, iq_ref, iww_ref, ikn_ref, thr_ref, jb_ref, kn_ref,
                           *, T, past, k_sel):
    kp = kp_ref[0]
    sc = _sample_scores(iq_ref, iww_ref, ikn_ref[0])
    rowi = lax.broadcasted_iota(I32, (T, LANES), 0)
    coln = lax.broadcasted_iota(I32, (T, LANES), 1)
    kn = jnp.where((coln <= rowi) & (coln < T), _float_key(sc), jnp.int32(INT_MIN))
    colp = lax.broadcasted_iota(I32, (T, past), 1)

    def count(pred):
        return (jnp.sum(jnp.where(pred(kp, colp), 1.0, 0.0), axis=1, keepdims=True)
                + jnp.sum(jnp.where(pred(kn, coln + past), 1.0, 0.0), axis=1, keepdims=True))

    kf = float(k_sel)
    thr = _kth_largest_key(lambda cand: count(lambda kk, col: kk >= cand), kf, (T, 1))
    thr = jnp.maximum(thr, jnp.int32(KEY_NEG_INF + 1))
    need = kf - count(lambda kk, col: kk > thr)
    nb = (past + LANES - 1).bit_length()

    def body(t, pfx):
        cand = pfx | lax.shift_left(jnp.int32(1), jnp.int32(nb - 1) - t)
        cnt = count(lambda kk, col: (kk == thr) & (col < cand))
        return jnp.where(cnt < need, cand, pfx)

    pfx = lax.fori_loop(0, nb, body, jnp.zeros((T, 1), I32))
    thr_ref[0] = jnp.broadcast_to(thr, (T, LANES))
    jb_ref[0] = jnp.broadcast_to(pfx + 1, (T, LANES))
    kn_ref[0] = kn


def _dsa_sample_attn_kernel(pt_ref, ck_ref, cv_ref, q_ref, kp_ref, thr_ref, jb_ref, kn_ref,
                            knew_ref, vnew_ref, o_ref, m_ref, l_ref, acc_ref, *, T, past, npages):
    p = pl.program_id(1)
    R = T * ATT_HEADS

    @pl.when(p == 0)
    def _():
        m_ref[...] = jnp.full((R, 1), NEG_BIG, F32)
        l_ref[...] = jnp.zeros((R, 1), F32)
        acc_ref[...] = jnp.zeros((R, KV_WIDTH), F32)

    thr = thr_ref[0]
    jb = jb_ref[0]
    coli = lax.broadcasted_iota(I32, (T, LANES), 1)

    def step(kk, col0, kb, vb):
        sel = (kk >= thr) & (kk < jnp.int32(KEY_POS_INF)) & ((kk > thr) | (col0 + coli < jb))
        bias = jnp.where(sel, 0.0, -jnp.inf)
        s = _dot_nt(q_ref[0], kb) * (ATT_HEAD_DIM ** -0.5)
        s = (s.reshape(T, ATT_HEADS, LANES) + bias[:, None, :]).reshape(R, LANES)
        m = m_ref[...]
        m_new = jnp.maximum(m, jnp.max(s, axis=1, keepdims=True))
        alpha = jnp.exp(m - m_new)
        pr = jnp.exp(s - m_new)
        l_ref[...] = alpha * l_ref[...] + jnp.sum(pr, axis=1, keepdims=True)
        acc_ref[...] = alpha * acc_ref[...] + _dot(pr.astype(BF16), vb)
        m_ref[...] = m_new

    step(kp_ref[0], p * PAGE_SIZE, ck_ref[0].astype(BF16), cv_ref[0].astype(BF16))

    @pl.when(p == npages - 1)
    def _():
        step(kn_ref[0], past, knew_ref[0], vnew_ref[0])
        o_ref[0] = acc_ref[...] / l_ref[...]


def dsa_sample(q, k, v, iq, iw, ik, ck, cv, cik, page_table):
    b, T, _ = q.shape
    npages = page_table.shape[1]
    past = npages * PAGE_SIZE
    n_phys = ck.shape[0]
    k_sel = max(1, min(TOPK_MAX, (past + T) // 4))
    R = T * ATT_HEADS
    ck2 = ck.reshape(n_phys, PAGE_SIZE, KV_WIDTH)
    cv2 = cv.reshape(n_phys, PAGE_SIZE, KV_WIDTH)
    iq_rows = iq.reshape(b, T * IDX_HEADS, IDX_DIM).astype(BF16)
    iww = jnp.broadcast_to(iw.reshape(b, T * IDX_HEADS, 1), (b, T * IDX_HEADS, LANES)).astype(F32)
    pad_rows = lambda a: jnp.pad(a, ((0, 0), (0, LANES - T), (0, 0))).astype(BF16)
    ik_new, k_new, v_new = pad_rows(ik), pad_rows(k), pad_rows(v)
    head_group = jnp.arange(ATT_HEADS) // (ATT_HEADS // ATT_KV_HEADS)
    onehot = (head_group[:, None] == jnp.arange(ATT_KV_HEADS)[None, :]).astype(F32)
    q_bd = (q.reshape(b, T, ATT_HEADS, 1, ATT_HEAD_DIM) * onehot[None, None, :, :, None])
    q_bd = q_bd.reshape(b, R, KV_WIDTH).astype(BF16)

    keys_past = pl.pallas_call(
        _dsa_sample_scores_kernel,
        grid_spec=pltpu.PrefetchScalarGridSpec(
            num_scalar_prefetch=1, grid=(b, npages),
            in_specs=[pl.BlockSpec((1, PAGE_SIZE, IDX_DIM), lambda bi, p, pt: (pt[bi, p], 0, 0)),
                      pl.BlockSpec((1, R, IDX_DIM), lambda bi, p, pt: (bi, 0, 0)),
                      pl.BlockSpec((1, R, LANES), lambda bi, p, pt: (bi, 0, 0))],
            out_specs=pl.BlockSpec((1, T, PAGE_SIZE), lambda bi, p, pt: (bi, 0, p))),
        out_shape=jax.ShapeDtypeStruct((b, T, past), I32),
        compiler_params=_cparams("parallel", "arbitrary"),
        name="dsa_sample_scores",
    )(page_table, cik, iq_rows, iww)

    row_spec = lambda w: pl.BlockSpec((1, T, w), lambda bi: (bi, 0, 0))
    thr, jb, keys_new = pl.pallas_call(
        functools.partial(_dsa_sample_thr_kernel, T=T, past=past, k_sel=k_sel),
        grid=(b,),
        in_specs=[row_spec(past),
                  pl.BlockSpec((1, R, IDX_DIM), lambda bi: (bi, 0, 0)),
                  pl.BlockSpec((1, R, LANES), lambda bi: (bi, 0, 0)),
                  pl.BlockSpec((1, LANES, IDX_DIM), lambda bi: (bi, 0, 0))],
        out_specs=[row_spec(LANES), row_spec(LANES), row_spec(LANES)],
        out_shape=[jax.ShapeDtypeStruct((b, T, LANES), I32)] * 3,
        compiler_params=_cparams("parallel"),
        name="dsa_sample_threshold",
    )(keys_past, iq_rows, iww, ik_new)

    fixed = lambda shape: pl.BlockSpec((1,) + shape, lambda bi, p, pt: (bi, 0, 0))
    o = pl.pallas_call(
        functools.partial(_dsa_sample_attn_kernel, T=T, past=past, npages=npages),
        grid_spec=pltpu.PrefetchScalarGridSpec(
            num_scalar_prefetch=1, grid=(b, npages),
            in_specs=[pl.BlockSpec((1, PAGE_SIZE, KV_WIDTH), lambda bi, p, pt: (pt[bi, p], 0, 0)),
                      pl.BlockSpec((1, PAGE_SIZE, KV_WIDTH), lambda bi, p, pt: (pt[bi, p], 0, 0)),
                      fixed((R, KV_WIDTH)),
                      pl.BlockSpec((1, T, PAGE_SIZE), lambda bi, p, pt: (bi, 0, p)),
                      fixed((T, LANES)), fixed((T, LANES)), fixed((T, LANES)),
                      fixed((LANES, KV_WIDTH)), fixed((LANES, KV_WIDTH))],
            out_specs=fixed((R, KV_WIDTH)),
            scratch_shapes=[pltpu.VMEM((R, 1), F32), pltpu.VMEM((R, 1), F32),
                            pltpu.VMEM((R, KV_WIDTH), F32)]),
        out_shape=jax.ShapeDtypeStruct((b, R, KV_WIDTH), F32),
        compiler_params=_cparams("parallel", "arbitrary"),
        name="dsa_sample_attention",
    )(page_table, ck2, cv2, q_bd, keys_past, thr, jb, keys_new, k_new, v_new)

    o = o.reshape(b, T, ATT_HEADS, ATT_KV_HEADS, ATT_HEAD_DIM)
    o = jnp.sum(o * onehot[None, None, :, :, None], axis=3)
    return o.reshape(b, T, ATT_HEADS * ATT_HEAD_DIM)


def _merge_kernel(ys_ref, ya_ref, yr_ref, g_ref, h_ref, wb_ref, wo_ref, o_ref):
    acc = None
    for n, y_ref in enumerate((ys_ref, ya_ref, yr_ref)):
        pr = _dot(y_ref[...].astype(BF16), wb_ref[n])
        t = jax.nn.sigmoid(g_ref[:, n * D_MODEL:(n + 1) * D_MODEL]) * pr
        acc = t if acc is None else acc + t
    o_ref[...] = h_ref[...] + _dot(acc.astype(BF16), wo_ref[...])


def merge_branches(ys, ya, yr, gates, h, wb, wo):
    M = h.shape[0]
    tm = _pick(M, (256, 128))
    rows = lambda w: pl.BlockSpec((tm, w), lambda i: (i, 0))
    return pl.pallas_call(
        _merge_kernel,
        grid=(M // tm,),
        in_specs=[rows(D_MODEL), rows(D_MODEL), rows(D_MODEL), rows(N_BRANCH * D_MODEL), rows(D_MODEL),
                  pl.BlockSpec((N_BRANCH, D_MODEL, D_MODEL), lambda i: (0, 0, 0)),
                  pl.BlockSpec((D_MODEL, D_MODEL), lambda i: (0, 0))],
        out_specs=rows(D_MODEL),
        out_shape=jax.ShapeDtypeStruct((M, D_MODEL), F32),
        compiler_params=_cparams("parallel"),
        name="merge_branches",
    )(ys, ya, yr, gates, h, wb, wo)


def _ffn_kernel(x_ref, g_ref, wg_ref, wu_ref, wd_ref, o_ref, xn_ref, acc_ref):
    j = pl.program_id(1)

    @pl.when(j == 0)
    def _():
        x = x_ref[...]
        ms = jnp.mean(x * x, axis=-1, keepdims=True)
        xn_ref[...] = (x * lax.rsqrt(ms + EPS) * g_ref[...]).astype(BF16)
        acc_ref[...] = jnp.zeros_like(acc_ref)

    xn = xn_ref[...]
    a = _silu(_dot(xn, wg_ref[...])) * _dot(xn, wu_ref[...])
    acc_ref[...] += _dot(a.astype(BF16), wd_ref[...])

    @pl.when(j == pl.num_programs(1) - 1)
    def _():
        o_ref[...] = x_ref[...] + acc_ref[...]


def ffn_dense(x, g, wg, wu, wd):
    M = x.shape[0]
    tm = _pick(M, (1024, 512, 256, 128))
    tf = 512
    return pl.pallas_call(
        _ffn_kernel,
        grid=(M // tm, D_FF // tf),
        in_specs=[pl.BlockSpec((tm, D_MODEL), lambda i, j: (i, 0)),
                  pl.BlockSpec((1, D_MODEL), lambda i, j: (0, 0)),
                  pl.BlockSpec((D_MODEL, tf), lambda i, j: (0, j)),
                  pl.BlockSpec((D_MODEL, tf), lambda i, j: (0, j)),
                  pl.BlockSpec((tf, D_MODEL), lambda i, j: (j, 0))],
        out_specs=pl.BlockSpec((tm, D_MODEL), lambda i, j: (i, 0)),
        out_shape=jax.ShapeDtypeStruct((M, D_MODEL), F32),
        scratch_shapes=[pltpu.VMEM((tm, D_MODEL), BF16), pltpu.VMEM((tm, D_MODEL), F32)],
        compiler_params=_cparams("parallel", "arbitrary"),
        name="ffn_dense",
    )(x, g.reshape(1, D_MODEL), wg, wu, wd)


def _router_kernel(x_ref, g_ref, wr_ref, gate_ref):
    x = x_ref[...]
    ms = jnp.mean(x * x, axis=-1, keepdims=True)
    xn = x * lax.rsqrt(ms + EPS) * g_ref[...]
    logits = jnp.dot(xn, wr_ref[...], preferred_element_type=F32, precision=lax.Precision.HIGHEST)
    lane = lax.broadcasted_iota(I32, logits.shape, 1)
    logits = jnp.where(lane < N_EXPERTS, logits, -jnp.inf)
    v1 = jnp.max(logits, axis=1, keepdims=True)
    i1 = jnp.min(jnp.where(logits == v1, lane, LANES), axis=1, keepdims=True)
    rest = jnp.where(lane == i1, -jnp.inf, logits)
    v2 = jnp.max(rest, axis=1, keepdims=True)
    i2 = jnp.min(jnp.where(rest == v2, lane, LANES), axis=1, keepdims=True)
    e2 = jnp.exp(v2 - v1)
    w1 = 1.0 / (1.0 + e2)
    w2 = e2 / (1.0 + e2)
    gate_ref[...] = jnp.where(lane == i1, w1, 0.0) + jnp.where(lane == i2, w2, 0.0)


def moe_router(x, g, wr):
    M = x.shape[0]
    tm = _pick(M, (512, 256, 128))
    wr_pad = jnp.pad(wr.astype(F32), ((0, 0), (0, LANES - N_EXPERTS)))
    return pl.pallas_call(
        _router_kernel,
        grid=(M // tm,),
        in_specs=[pl.BlockSpec((tm, D_MODEL), lambda i: (i, 0)),
                  pl.BlockSpec((1, D_MODEL), lambda i: (0, 0)),
                  pl.BlockSpec((D_MODEL, LANES), lambda i: (0, 0))],
        out_specs=pl.BlockSpec((tm, LANES), lambda i: (i, 0)),
        out_shape=jax.ShapeDtypeStruct((M, LANES), F32),
        compiler_params=_cparams("parallel"),
        name="moe_router",
    )(x, g.reshape(1, D_MODEL), wr_pad)


def _moe_kernel(x_ref, g_ref, gate_ref, wg_ref, wu_ref, wd_ref, o_ref, xn_ref, acc_ref):
    e = pl.program_id(1)
    j = pl.program_id(2)

    @pl.when((e == 0) & (j == 0))
    def _():
        x = x_ref[...]
        ms = jnp.mean(x * x, axis=-1, keepdims=True)
        xn_ref[...] = (x * lax.rsqrt(ms + EPS) * g_ref[...]).astype(BF16)
        acc_ref[...] = jnp.zeros_like(acc_ref)

    lane = lax.broadcasted_iota(I32, gate_ref.shape, 1)
    gcol = jnp.sum(jnp.where(lane == e, gate_ref[...], 0.0), axis=1, keepdims=True)
    xn = xn_ref[...]
    a = _silu(_dot(xn, wg_ref[0])) * _dot(xn, wu_ref[0])
    acc_ref[...] += _dot((a * gcol).astype(BF16), wd_ref[0])

    @pl.when((e == pl.num_programs(1) - 1) & (j == pl.num_programs(2) - 1))
    def _():
        o_ref[...] = x_ref[...] + acc_ref[...]


def moe_ffn(x, g, gate, wg, wu, wd):
    M = x.shape[0]
    tm = _pick(M, (1024, 512, 256, 128))
    tf = 512
    return pl.pallas_call(
        _moe_kernel,
        grid=(M // tm, N_EXPERTS, D_FF // tf),
        in_specs=[pl.BlockSpec((tm, D_MODEL), lambda i, e, j: (i, 0)),
                  pl.BlockSpec((1, D_MODEL), lambda i, e, j: (0, 0)),
                  pl.BlockSpec((tm, LANES), lambda i, e, j: (i, 0)),
                  pl.BlockSpec((1, D_MODEL, tf), lambda i, e, j: (e, 0, j)),
                  pl.BlockSpec((1, D_MODEL, tf), lambda i, e, j: (e, 0, j)),
                  pl.BlockSpec((1, tf, D_MODEL), lambda i, e, j: (e, j, 0))],
        out_specs=pl.BlockSpec((tm, D_MODEL), lambda i, e, j: (i, 0)),
        out_shape=jax.ShapeDtypeStruct((M, D_MODEL), F32),
        scratch_shapes=[pltpu.VMEM((tm, D_MODEL), BF16), pltpu.VMEM((tm, D_MODEL), F32)],
        compiler_params=_cparams("parallel", "arbitrary", "arbitrary"),
        name="moe_ffn",
    )(x, g.reshape(1, D_MODEL), gate, wg, wu, wd)


def _rmsnorm_kernel(x_ref, g_ref, o_ref):
    x = x_ref[...]
    ms = jnp.mean(x * x, axis=-1, keepdims=True)
    o_ref[...] = x * lax.rsqrt(ms + EPS) * g_ref[...]


def rmsnorm(x, g):
    M = x.shape[0]
    tm = _pick(M, (1024, 512, 256, 128))
    return pl.pallas_call(
        _rmsnorm_kernel,
        grid=(M // tm,),
        in_specs=[pl.BlockSpec((tm, D_MODEL), lambda i: (i, 0)),
                  pl.BlockSpec((1, D_MODEL), lambda i: (0, 0))],
        out_specs=pl.BlockSpec((tm, D_MODEL), lambda i: (i, 0)),
        out_shape=jax.ShapeDtypeStruct((M, D_MODEL), F32),
        compiler_params=_cparams("parallel"),
        name="final_rmsnorm",
    )(x, g.reshape(1, D_MODEL))


def _split_w_in(w_in):
    offs = np.cumsum((0,) + IN_SPLITS)
    seg = {n: w_in[:, offs[i]:offs[i + 1]] for i, n in enumerate(
        ("z", "xbc", "dt", "q", "k", "v", "iq", "iw", "ik", "rq", "rk", "rv", "rg", "gates"))}
    cat = lambda *names: jnp.concatenate([seg[n] if isinstance(n, str) else n for n in names], axis=1)
    dt_wide = jnp.repeat(seg["dt"], SSM_HEAD_DIM, axis=1)
    small = cat("dt", "iw", jnp.zeros((D_MODEL, LANES - 2 * IDX_HEADS - IDX_DIM), w_in.dtype), "ik")
    groups = dict(ssm=cat("z", "xbc", dt_wide), att=cat("q", "k", "v"), idx=cat("iq", small),
                  ret=cat("rq", "rk", "rv", "rg"), gate=seg["gates"])
    return {n: w.astype(BF16) for n, w in groups.items()}


def _rope_tables(pos):
    half = RET_DK // 2
    inv = ROPE_BASE ** (-jnp.arange(half, dtype=F32) / half)
    ang = pos.astype(F32)[:, None] * inv[None, :]
    cos, sin = jnp.cos(ang), jnp.sin(ang)
    return jnp.concatenate([cos, cos], axis=1), jnp.concatenate([-sin, sin], axis=1)


def _pad_rows(a, L):
    return jnp.pad(a, ((0, 0), (0, L - a.shape[1]), (0, 0)))


def _mixers(h, pos0, conv0, ssm0, ret0, attend, lw):
    b, L, _ = h.shape
    hf = h.reshape(b * L, D_MODEL)
    proj = {n: rms_matmul(hf, lw["norm_g"], w).reshape(b, L, -1) for n, w in lw["w_in"].items()}
    att, idx = proj["att"], proj["idx"]
    q = att[..., :1024]
    k = att[..., 1024:1280]
    v = att[..., 1280:1536]
    iq = idx[..., :1024]
    small = idx[..., 1024:]
    ik = small[..., LANES - IDX_DIM:]

    Lp = -(-L // CHUNK) * CHUNK
    last_valid = L - (Lp - CHUNK)
    dtT = jnp.swapaxes(_pad_rows(small[..., :SSM_HEADS], Lp), 1, 2)
    s0 = jnp.transpose(ssm0, (0, 2, 1, 3)).reshape(b, SSM_STATE, SSM_INNER)
    ys, conv_new, s_new = ssd_branch(_pad_rows(proj["ssm"], Lp), dtT, conv0, s0, lw["conv_w"], lw["conv_b"],
                                     lw["dt_bias"], lw["a_log"], lw["d_skip"], lw["ssm_norm_g"], last_valid)
    ssm_new = jnp.transpose(s_new.reshape(b, SSM_STATE, SSM_HEADS, SSM_HEAD_DIM), (0, 2, 1, 3))

    cos2, sin2 = _rope_tables(pos0 + jnp.arange(Lp))
    yr, ret_new = ret_branch(_pad_rows(proj["ret"], Lp), cos2, sin2, ret0, last_valid)

    ya = attend(q, k, v, iq, small, ik)

    ys = ys[:, :L].reshape(b * L, -1)
    yr = yr[:, :L].reshape(b * L, -1)
    out = merge_branches(ys, ya.reshape(b * L, -1), yr, proj["gate"].reshape(b * L, -1), hf,
                         lw["w_branch"], lw["w_out"])
    return out.reshape(b, L, D_MODEL), (k.reshape(b, L, ATT_KV_HEADS, ATT_HEAD_DIM),
                                        v.reshape(b, L, ATT_KV_HEADS, ATT_HEAD_DIM),
                                        ik, ssm_new, conv_new, ret_new)


def _attend_prompt(q, k, v, iq, small, ik):
    b, L, _ = q.shape
    hm = lambda a, nh: jnp.transpose(a.reshape(b, L, nh, -1), (0, 2, 1, 3)).astype(BF16)
    o = dsa_prompt(hm(q, ATT_HEADS), hm(iq, IDX_HEADS), small, ik.astype(BF16),
                   hm(k, ATT_KV_HEADS), hm(v, ATT_KV_HEADS))
    return jnp.transpose(o, (0, 2, 1, 3)).reshape(b, L, ATT_HEADS * ATT_HEAD_DIM)


def _attend_sample(q, k, v, iq, small, ik, *, ck, cv, cik, page_table):
    iw = small[..., IDX_HEADS:2 * IDX_HEADS]
    return dsa_sample(q, k, v, iq, iw, ik, ck, cv, cik, page_table)


def _channel_mixer(h, l, p):
    b, L, _ = h.shape
    hf = h.reshape(b * L, D_MODEL)
    j = l // 2
    if l % 2 == 0:
        out = ffn_dense(hf, p["norm_ffn_g"][l], p["w_ffn_gate"][j].astype(BF16),
                        p["w_ffn_up"][j].astype(BF16), p["w_ffn_down"][j].astype(BF16))
    else:
        gate = moe_router(hf, p["norm_ffn_g"][l], p["w_router"][j])
        out = moe_ffn(hf, p["norm_ffn_g"][l], gate, p["w_moe_gate"][j].astype(BF16),
                      p["w_moe_up"][j].astype(BF16), p["w_moe_down"][j].astype(BF16))
    return out.reshape(b, L, D_MODEL)


def kernel(x_prompt, x_sample, cache_k, cache_v, cache_idx_k, state_ssm, state_conv, state_ret,
           page_table, norm_mix_g, w_in, conv_w, conv_b, dt_bias, a_log, d_skip, ssm_norm_g,
           w_branch, w_out, norm_ffn_g, w_ffn_gate, w_ffn_up, w_ffn_down, w_router,
           w_moe_gate, w_moe_up, w_moe_down, final_norm_g):
    depth = w_in.shape[0]
    past = page_table.shape[1] * PAGE_SIZE
    bp = x_prompt.shape[0]
    p = dict(norm_ffn_g=norm_ffn_g, w_ffn_gate=w_ffn_gate, w_ffn_up=w_ffn_up, w_ffn_down=w_ffn_down,
             w_router=w_router, w_moe_gate=w_moe_gate, w_moe_up=w_moe_up, w_moe_down=w_moe_down)
    hp, hs = x_prompt, x_sample
    st_p, st_s = [], []
    for l in range(depth):
        lw = dict(norm_g=norm_mix_g[l], w_in=_split_w_in(w_in[l]), conv_w=conv_w[l], conv_b=conv_b[l],
                  dt_bias=dt_bias[l], a_log=a_log[l], d_skip=d_skip[l], ssm_norm_g=ssm_norm_g[l],
                  w_branch=w_branch[l].astype(BF16), w_out=w_out[l].astype(BF16))
        hp, sp = _mixers(hp, 0,
                         jnp.zeros((bp, CONV_WIDTH - 1, CONV_DIM), F32),
                         jnp.zeros((bp, SSM_HEADS, SSM_STATE, SSM_HEAD_DIM), F32),
                         jnp.zeros((bp, RET_HEADS, RET_DK, RET_DV), F32),
                         _attend_prompt, lw)
        attend_s = functools.partial(_attend_sample, ck=cache_k[l], cv=cache_v[l], cik=cache_idx_k[l],
                                     page_table=page_table)
        hs, ss = _mixers(hs, past, state_conv[l], state_ssm[l], state_ret[l], attend_s, lw)
        hp = _channel_mixer(hp, l, p)
        hs = _channel_mixer(hs, l, p)
        st_p.append(sp)
        st_s.append(ss)
    y_prompt = rmsnorm(hp.reshape(-1, D_MODEL), final_norm_g).reshape(hp.shape)
    y_sample = rmsnorm(hs.reshape(-1, D_MODEL), final_norm_g).reshape(hs.shape)
    stack = lambda sts, i: jnp.stack([s[i] for s in sts])
    return (y_prompt, y_sample,
            stack(st_p, 0), stack(st_p, 1), stack(st_p, 2), stack(st_p, 3), stack(st_p, 4), stack(st_p, 5),
            stack(st_s, 0), stack(st_s, 1), stack(st_s, 2), stack(st_s, 3), stack(st_s, 4), stack(st_s, 5))
```

```python
import functools
import math

import jax
import jax.numpy as jnp
import numpy as np
from jax import lax
from jax.experimental import pallas as pl
from jax.experimental.pallas import tpu as pltpu

F32 = jnp.float32
BF16 = jnp.bfloat16
I32 = jnp.int32

D_MODEL = 1024
PAGE_SIZE = 128
SSM_HEADS = 16
SSM_HEAD_DIM = 64
SSM_INNER = 1024
SSM_GROUPS = 2
SSM_STATE = 128
CONV_WIDTH = 4
CONV_DIM = 1536
ATT_HEADS = 16
ATT_KV_HEADS = 4
ATT_HEAD_DIM = 64
KV_WIDTH = 256
IDX_HEADS = 16
IDX_DIM = 64
TOPK_MAX = 256
RET_HEADS = 4
RET_DK = 128
RET_DV = 256
ROPE_BASE = 10000.0
N_BRANCH = 3
D_FF = 3584
N_EXPERTS = 8
EPS = 1e-6
IN_SPLITS = (1024, 1536, 16, 1024, 256, 256, 1024, 16, 64, 512, 512, 1024, 1024, 3072)

LANES = 128
SUBLANES = 8
VMEM_LIMIT = 48 * 1024 * 1024

CHUNK = 128
INT_MIN = -(2 ** 31)
KEY_POS_INF = 0x7F800000
KEY_NEG_INF = -2139095041
NEG_BIG = -1e30


def _cparams(*sem):
    return pltpu.CompilerParams(dimension_semantics=sem, vmem_limit_bytes=VMEM_LIMIT)


def _dot(a, b):
    return jnp.dot(a, b, preferred_element_type=F32)


def _dot_nt(a, b):
    return lax.dot_general(a, b, (((1,), (1,)), ((), ())), preferred_element_type=F32)


def _dot_tn(a, b):
    return lax.dot_general(a, b, (((0,), (0,)), ((), ())), preferred_element_type=F32)


def _split3(x):
    hi = x.astype(BF16)
    r = x - hi.astype(F32)
    mid = r.astype(BF16)
    lo = (r - mid.astype(F32)).astype(BF16)
    return hi, mid, lo


def _silu(x):
    return x * jax.nn.sigmoid(x)


def _softplus(x):
    return jnp.maximum(x, 0.0) + jnp.log1p(jnp.exp(-jnp.abs(x)))


def _float_key(x):
    x = jnp.where(x == 0.0, 0.0, x)
    b = lax.bitcast_convert_type(x, I32)
    return jnp.where(b >= 0, b, b ^ jnp.int32(0x7FFFFFFF))


def _rms_matmul_kernel(x_ref, g_ref, w_ref, o_ref, xn_ref):
    @pl.when(pl.program_id(1) == 0)
    def _():
        x = x_ref[...]
        ms = jnp.mean(x * x, axis=-1, keepdims=True)
        xn_ref[...] = (x * lax.rsqrt(ms + EPS) * g_ref[...]).astype(BF16)

    o_ref[...] = _dot(xn_ref[...], w_ref[...]).astype(o_ref.dtype)


def _pick(n, prefs):
    for p in prefs:
        if n % p == 0:
            return p
    return n


def rms_matmul(x, g, w, out_dtype=F32):
    M, K = x.shape
    N = w.shape[1]
    tm = _pick(M, (1024, 512, 256, 128))
    tn = _pick(N, (512, 384, 256, 128))
    return pl.pallas_call(
        _rms_matmul_kernel,
        grid=(M // tm, N // tn),
        in_specs=[pl.BlockSpec((tm, K), lambda i, j: (i, 0)),
                  pl.BlockSpec((1, K), lambda i, j: (0, 0)),
                  pl.BlockSpec((K, tn), lambda i, j: (0, j))],
        out_specs=pl.BlockSpec((tm, tn), lambda i, j: (i, j)),
        out_shape=jax.ShapeDtypeStruct((M, N), out_dtype),
        scratch_shapes=[pltpu.VMEM((tm, K), BF16)],
        compiler_params=_cparams("parallel", "arbitrary"),
        name="rms_matmul",
    )(x, g.reshape(1, K), w)


def _ssd_kernel(zxd_ref, dtT_ref, conv0_ref, s0_ref, cw_ref, cb_ref, dtb_ref, alog_ref, dsk_ref,
                ng_ref, dtbT_ref, alogT_ref, y_ref, convn_ref, sn_ref, xpad_ref, st_ref,
                *, Q, last_valid, nc):
    c = pl.program_id(1)
    GW = SSM_INNER // SSM_GROUPS

    @pl.when(c == 0)
    def _():
        xpad_ref[0:8, :] = jnp.zeros((8, CONV_DIM), F32)
        xpad_ref[5:8, :] = conv0_ref[0]
        st_ref[...] = s0_ref[0]

    blk = zxd_ref[0]
    z = blk[:, :SSM_INNER]
    xbc = blk[:, SSM_INNER:SSM_INNER + CONV_DIM]
    dtr = blk[:, SSM_INNER + CONV_DIM:]
    xpad_ref[8:8 + Q, :] = xbc
    cw = cw_ref[...]
    conv = (xpad_ref[5:5 + Q, :] * cw[0:1] + xpad_ref[6:6 + Q, :] * cw[1:2]
            + xpad_ref[7:7 + Q, :] * cw[2:3] + xbc * cw[3:4]) + cb_ref[...]
    xc = _silu(conv)
    xs = xc[:, :SSM_INNER]
    Bm = xc[:, SSM_INNER:SSM_INNER + SSM_GROUPS * SSM_STATE]
    Cm = xc[:, SSM_INNER + SSM_GROUPS * SSM_STATE:]

    row = lax.broadcasted_iota(I32, (Q, 1), 0)
    colq = lax.broadcasted_iota(I32, (1, Q), 1)
    assert last_valid == Q or nc == 1
    lv = last_valid
    valid = row < lv
    tril = (lax.broadcasted_iota(I32, (Q, Q), 0) >= lax.broadcasted_iota(I32, (Q, Q), 1))
    tril_b = jnp.where(tril, 1.0, 0.0).astype(BF16)
    triu_b = jnp.where(lax.broadcasted_iota(I32, (Q, Q), 0) <= lax.broadcasted_iota(I32, (Q, Q), 1),
                       1.0, 0.0).astype(BF16)

    nega = -jnp.exp(alog_ref[...])
    dt = _softplus(dtr + dtb_ref[...])
    la = jnp.where(valid, dt * nega, 0.0)
    cum = sum(_dot(tril_b, p) for p in _split3(la))
    laT = jnp.where(colq < lv, _softplus(dtT_ref[0] + dtbT_ref[...]) * (-jnp.exp(alogT_ref[...])), 0.0)
    cumT = sum(_dot(p, triu_b) for p in _split3(laT))

    ecum = jnp.exp(cum)
    cl = cum[lv - 1:lv, :]
    xdt = xs * dt
    xtail = jnp.where(valid, xdt * jnp.exp(cl - cum), 0.0)
    cdecay = jnp.exp(cl)

    lane = lax.broadcasted_iota(I32, (1, LANES), 1)
    y_groups = []
    for g in range(SSM_GROUPS):
        l0 = g * GW
        Cg = Cm[:, g * SSM_STATE:(g + 1) * SSM_STATE].astype(BF16)
        Bg = Bm[:, g * SSM_STATE:(g + 1) * SSM_STATE].astype(BF16)
        G = _dot_nt(Cg, Bg)
        st_g = st_ref[:, l0:l0 + GW]
        inter = _dot(Cg, st_g.astype(BF16)) * ecum[:, l0:l0 + GW]
        local = _dot_tn(Bg, xtail[:, l0:l0 + GW].astype(BF16))
        st_ref[:, l0:l0 + GW] = st_g * cdecay[:, l0:l0 + GW] + local
        pairs = []
        for p in range(GW // LANES):
            xp = xdt[:, l0 + p * LANES:l0 + (p + 1) * LANES]
            acc = None
            for hh in range(2):
                h = (l0 + p * LANES) // SSM_HEAD_DIM + hh
                ccol = cum[:, h * SSM_HEAD_DIM:h * SSM_HEAD_DIM + 1]
                diff = ccol - cumT[h:h + 1, :]
                dm = jnp.exp(jnp.where(tril, diff, -jnp.inf))
                s = (G * dm).astype(BF16)
                half = (lane >= hh * SSM_HEAD_DIM) & (lane < (hh + 1) * SSM_HEAD_DIM)
                part = _dot(s, jnp.where(half, xp, 0.0).astype(BF16))
                acc = part if acc is None else acc + part
            pairs.append(acc)
        y_groups.append(jnp.concatenate(pairs, axis=1) + inter)
    y = jnp.concatenate(y_groups, axis=1)

    y = (y + dsk_ref[...] * xs) * _silu(z)
    outs = []
    for g in range(SSM_GROUPS):
        seg = y[:, g * GW:(g + 1) * GW]
        ms = jnp.mean(seg * seg, axis=-1, keepdims=True)
        outs.append(seg * lax.rsqrt(ms + EPS) * ng_ref[:, g * GW:(g + 1) * GW])
    y_ref[0] = jnp.concatenate(outs, axis=1)

    @pl.when(c == nc - 1)
    def _():
        convn_ref[0] = xpad_ref[5 + last_valid:8 + last_valid, :]
        sn_ref[0] = st_ref[...]

    xpad_ref[0:8, :] = xpad_ref[Q:Q + 8, :]


def ssd_branch(zxd, dtT, conv0, s0, conv_w, conv_b, dt_bias, a_log, d_skip, norm_g, last_valid):
    b, L, W = zxd.shape
    Q = CHUNK
    nc = L // Q
    rep = lambda v: jnp.repeat(v.astype(F32), SSM_HEAD_DIM).reshape(1, SSM_INNER)
    col = lambda v: v.astype(F32).reshape(SSM_HEADS, 1)
    full = lambda shape: pl.BlockSpec(shape, lambda i, c: (0,) * len(shape))
    kern = functools.partial(_ssd_kernel, Q=Q, last_valid=last_valid, nc=nc)
    return pl.pallas_call(
        kern,
        grid=(b, nc),
        in_specs=[pl.BlockSpec((1, Q, W), lambda i, c: (i, c, 0)),
                  pl.BlockSpec((1, SSM_HEADS, Q), lambda i, c: (i, 0, c)),
                  pl.BlockSpec((1, CONV_WIDTH - 1, CONV_DIM), lambda i, c: (i, 0, 0)),
                  pl.BlockSpec((1, SSM_STATE, SSM_INNER), lambda i, c: (i, 0, 0)),
                  full((CONV_WIDTH, CONV_DIM)), full((1, CONV_DIM)),
                  full((1, SSM_INNER)), full((1, SSM_INNER)), full((1, SSM_INNER)), full((1, SSM_INNER)),
                  full((SSM_HEADS, 1)), full((SSM_HEADS, 1))],
        out_specs=[pl.BlockSpec((1, Q, SSM_INNER), lambda i, c: (i, c, 0)),
                   pl.BlockSpec((1, CONV_WIDTH - 1, CONV_DIM), lambda i, c: (i, 0, 0)),
                   pl.BlockSpec((1, SSM_STATE, SSM_INNER), lambda i, c: (i, 0, 0))],
        out_shape=[jax.ShapeDtypeStruct((b, L, SSM_INNER), F32),
                   jax.ShapeDtypeStruct((b, CONV_WIDTH - 1, CONV_DIM), F32),
                   jax.ShapeDtypeStruct((b, SSM_STATE, SSM_INNER), F32)],
        scratch_shapes=[pltpu.VMEM((Q + 8, CONV_DIM), F32), pltpu.VMEM((SSM_STATE, SSM_INNER), F32)],
        compiler_params=_cparams("parallel", "arbitrary"),
        name="ssd_branch",
    )(zxd, dtT, conv0, s0, conv_w.astype(F32), conv_b.reshape(1, CONV_DIM).astype(F32),
      rep(dt_bias), rep(a_log), rep(d_skip), norm_g.reshape(1, SSM_INNER).astype(F32),
      col(dt_bias), col(a_log))


def _ret_kernel(x_ref, cos_ref, sin_ref, r0_ref, y_ref, rn_ref, st_ref, *, Q, last_valid, nc):
    c = pl.program_id(1)

    @pl.when(c == 0)
    def _():
        st_ref[...] = r0_ref[0]

    blk = x_ref[0]
    cos2 = cos_ref[...]
    sin2 = sin_ref[...]
    ri = lax.broadcasted_iota(I32, (Q, Q), 0)
    ci = lax.broadcasted_iota(I32, (Q, Q), 1)
    dij = (ri - ci).astype(F32)
    row = lax.broadcasted_iota(I32, (Q, 1), 0)
    assert last_valid == Q or nc == 1
    lv = last_valid
    rowf = row.astype(F32)
    lvf = float(lv)
    KO = RET_HEADS * RET_DK
    outs = []
    for h in range(RET_HEADS):
        lg = math.log1p(-2.0 ** (-5.0 - h))
        qh = blk[:, h * RET_DK:(h + 1) * RET_DK]
        kh = blk[:, KO + h * RET_DK:KO + (h + 1) * RET_DK]
        vh = blk[:, 2 * KO + h * RET_DV:2 * KO + (h + 1) * RET_DV].astype(BF16)
        gh = blk[:, 2 * KO + RET_HEADS * RET_DV + h * RET_DV:2 * KO + RET_HEADS * RET_DV + (h + 1) * RET_DV]
        qr = qh * cos2 + pltpu.roll(qh, RET_DK // 2, 1) * sin2
        kr = (kh * cos2 + pltpu.roll(kh, RET_DK // 2, 1) * sin2) * (RET_DK ** -0.5)
        qb = qr.astype(BF16)
        dm = jnp.exp(jnp.where(ri >= ci, dij * lg, -jnp.inf))
        s = (_dot_nt(qb, kr.astype(BF16)) * dm).astype(BF16)
        st_h = st_ref[h]
        y = _dot(s, vh) + jnp.exp((rowf + 1.0) * lg) * _dot(qb, st_h.astype(BF16))
        ktail = jnp.where(row < lv, kr * jnp.exp((lvf - 1.0 - rowf) * lg), 0.0)
        st_ref[h] = st_h * math.exp(lvf * lg) + _dot_tn(ktail.astype(BF16), vh)
        ms = jnp.mean(y * y, axis=-1, keepdims=True)
        outs.append(y * lax.rsqrt(ms + EPS) * _silu(gh))
    y_ref[0] = jnp.concatenate(outs, axis=1)

    @pl.when(c == nc - 1)
    def _():
        rn_ref[0] = st_ref[...]


def ret_branch(x, cos2, sin2, r0, last_valid):
    b, L, W = x.shape
    Q = CHUNK
    nc = L // Q
    kern = functools.partial(_ret_kernel, Q=Q, last_valid=last_valid, nc=nc)
    return pl.pallas_call(
        kern,
        grid=(b, nc),
        in_specs=[pl.BlockSpec((1, Q, W), lambda i, c: (i, c, 0)),
                  pl.BlockSpec((Q, RET_DK), lambda i, c: (c, 0)),
                  pl.BlockSpec((Q, RET_DK), lambda i, c: (c, 0)),
                  pl.BlockSpec((1, RET_HEADS, RET_DK, RET_DV), lambda i, c: (i, 0, 0, 0))],
        out_specs=[pl.BlockSpec((1, Q, RET_HEADS * RET_DV), lambda i, c: (i, c, 0)),
                   pl.BlockSpec((1, RET_HEADS, RET_DK, RET_DV), lambda i, c: (i, 0, 0, 0))],
        out_shape=[jax.ShapeDtypeStruct((b, L, RET_HEADS * RET_DV), F32),
                   jax.ShapeDtypeStruct((b, RET_HEADS, RET_DK, RET_DV), F32)],
        scratch_shapes=[pltpu.VMEM((RET_HEADS, RET_DK, RET_DV), F32)],
        compiler_params=_cparams("parallel", "arbitrary"),
        name="ret_branch",
    )(x, cos2, sin2, r0)


def _kth_largest_key(count_ge, k, shape):
    def body(t, prefix):
        bit = lax.shift_left(jnp.int32(1), jnp.int32(31) - t)
        cand = prefix | bit
        cnt = count_ge(cand ^ jnp.int32(INT_MIN))
        return jnp.where(cnt >= k, cand, prefix)

    prefix = lax.fori_loop(0, 32, body, jnp.zeros(shape, I32))
    return prefix ^ jnp.int32(INT_MIN)


def _dsa_prompt_kernel(q_ref, iq_ref, iwt_ref, ik_ref, k_ref, vt_ref, o_ref, keys_ref, jb_ref,
                       m_ref, acc_ref, *, tq, ck, cka, L, k_sel):
    i = pl.program_id(1)
    G = ATT_HEADS // ATT_KV_HEADS
    PART = 4 * SUBLANES
    nk = lax.div((i + 1) * tq + (ck - 1), ck)
    qpos = i * tq + lax.broadcasted_iota(I32, (1, tq), 1)
    rowk = lax.broadcasted_iota(I32, (ck, 1), 0)
    iw = iwt_ref[0] * (IDX_HEADS ** -0.5 * IDX_DIM ** -0.5)

    def score_body(kc, carry):
        off = pl.multiple_of(kc * ck, ck)
        ikc = ik_ref[0, pl.ds(off, ck), :]
        acc = jnp.zeros((ck, tq), F32)
        for h4 in range(0, IDX_HEADS, G):
            d = _dot_nt(ikc, iq_ref[0, h4:h4 + G].reshape(G * tq, IDX_DIM))
            for j in range(G):
                acc = acc + jnp.maximum(d[:, j * tq:(j + 1) * tq], 0.0) * iw[h4 + j:h4 + j + 1, :]
        key = jnp.where(off + rowk <= qpos, _float_key(acc), jnp.int32(INT_MIN))
        keys_ref[pl.ds(off, ck), :] = key
        return carry

    lax.fori_loop(0, nk, score_body, 0)

    def count(pred):
        def body(kc, part):
            off = pl.multiple_of(kc * ck, ck)
            hit = jnp.where(pred(keys_ref[pl.ds(off, ck), :], off + rowk), 1.0, 0.0)
            return part + jnp.sum(hit.reshape(ck // PART, PART, tq), axis=0)
        part = lax.fori_loop(0, nk, body, jnp.zeros((PART, tq), F32))
        return jnp.sum(part, axis=0, keepdims=True)

    kf = float(k_sel)
    thr = _kth_largest_key(lambda cand: count(lambda kk, pos: kk >= cand), kf, (1, tq))
    thr = jnp.maximum(thr, jnp.int32(KEY_NEG_INF + 1))
    n_ge = count(lambda kk, pos: kk >= thr)
    jb_ref[...] = jnp.full((1, tq), L, I32)

    @pl.when(jnp.max(n_ge) > kf)
    def _():
        need = kf - count(lambda kk, pos: kk > thr)
        nb = max(1, (L - 1).bit_length())

        def body(t, pfx):
            cand = pfx | lax.shift_left(jnp.int32(1), jnp.int32(nb - 1) - t)
            cnt = count(lambda kk, pos: (kk == thr) & (pos < cand))
            return jnp.where(cnt < need, cand, pfx)

        pfx = lax.fori_loop(0, nb, body, jnp.zeros((1, tq), I32))
        jb_ref[...] = jnp.where(n_ge > kf, pfx + 1, L)

    jb = jb_ref[...]

    m_ref[...] = jnp.full(m_ref.shape, NEG_BIG, F32)
    acc_ref[...] = jnp.zeros(acc_ref.shape, F32)
    rowa = lax.broadcasted_iota(I32, (cka, 1), 0)

    def att_body(kc, carry):
        off = pl.multiple_of(kc * cka, cka)
        kk = keys_ref[pl.ds(off, cka), :]
        sel = (kk >= thr) & (kk < jnp.int32(KEY_POS_INF)) & ((kk > thr) | (off + rowa < jb))
        bias = jnp.where(sel, 0.0, -jnp.inf)
        bias = jnp.concatenate([bias] * G, axis=1)
        for g in range(ATT_KV_HEADS):
            kg = k_ref[0, g, pl.ds(off, cka), :]
            vg = vt_ref[0, g, :, pl.ds(off, cka)]
            qg = q_ref[0, g * G:(g + 1) * G].reshape(G * tq, ATT_HEAD_DIM)
            s = _dot_nt(kg, qg) + bias
            m = m_ref[g]
            smax = jnp.max(jnp.max(s.reshape(cka // PART, PART, G * tq), axis=0), axis=0, keepdims=True)
            m_new = jnp.maximum(m, smax)
            p = jnp.exp(s - m_new)
            acc_ref[g] = jnp.exp(m - m_new) * acc_ref[g] + _dot(vg, p.astype(BF16))
            m_ref[g] = m_new
        return carry

    lax.fori_loop(0, lax.div((i + 1) * tq + (cka - 1), cka), att_body, 0)
    for g in range(ATT_KV_HEADS):
        acc = acc_ref[g]
        og = acc[:ATT_HEAD_DIM] / acc[ATT_HEAD_DIM:ATT_HEAD_DIM + 1]
        for hh in range(G):
            o_ref[0, g * G + hh] = og[:, hh * tq:(hh + 1) * tq]


V_ROWS = ATT_HEAD_DIM + 16


def dsa_prompt(q_hm, iq_hm, iw_t, ik, k_hm, v_t):
    b, _, L, _ = q_hm.shape
    tq = min(128, L)
    ck = min(512, L)
    cka = min(512, L)
    gq = (ATT_HEADS // ATT_KV_HEADS) * tq
    k_sel = max(1, min(TOPK_MAX, L // 4))
    kern = functools.partial(_dsa_prompt_kernel, tq=tq, ck=ck, cka=cka, L=L, k_sel=k_sel)
    return pl.pallas_call(
        kern,
        grid=(b, L // tq),
        in_specs=[pl.BlockSpec((1, ATT_HEADS, tq, ATT_HEAD_DIM), lambda bi, i: (bi, 0, i, 0)),
                  pl.BlockSpec((1, IDX_HEADS, tq, IDX_DIM), lambda bi, i: (bi, 0, i, 0)),
                  pl.BlockSpec((1, IDX_HEADS, tq), lambda bi, i: (bi, 0, i)),
                  pl.BlockSpec((1, L, IDX_DIM), lambda bi, i: (bi, 0, 0)),
                  pl.BlockSpec((1, ATT_KV_HEADS, L, ATT_HEAD_DIM), lambda bi, i: (bi, 0, 0, 0)),
                  pl.BlockSpec((1, ATT_KV_HEADS, V_ROWS, L), lambda bi, i: (bi, 0, 0, 0))],
        out_specs=pl.BlockSpec((1, ATT_HEADS, ATT_HEAD_DIM, tq), lambda bi, i: (bi, 0, 0, i)),
        out_shape=jax.ShapeDtypeStruct((b, ATT_HEADS, ATT_HEAD_DIM, L), F32),
        scratch_shapes=[pltpu.VMEM((L, tq), I32), pltpu.VMEM((1, tq), I32),
                        pltpu.VMEM((ATT_KV_HEADS, 1, gq), F32),
                        pltpu.VMEM((ATT_KV_HEADS, V_ROWS, gq), F32)],
        compiler_params=_cparams("parallel", "arbitrary"),
        name="dsa_prompt",
    )(q_hm, iq_hm, iw_t, ik, k_hm, v_t)


TQ8 = SUBLANES


def _sample_score_keys(d, iww):
    n = d.shape[1]
    r = jnp.maximum(d, 0.0) * (iww[:, 0:1] * (IDX_HEADS ** -0.5 * IDX_DIM ** -0.5))
    return _float_key(jnp.sum(r.reshape(TQ8, IDX_HEADS, n), axis=1))


def _dsa_sample_scores_kernel(pt_ref, *refs, n_pages):
    page_refs = refs[:n_pages]
    iq_ref, iww_ref, keys_ref = refs[n_pages:]
    ik_t = jnp.concatenate([r[0, 0] for r in page_refs], axis=1).astype(BF16)
    keys_ref[0] = _sample_score_keys(_dot(iq_ref[0], ik_t), iww_ref[0])


def _dsa_sample_thr_kernel(kp_ref, iq_ref, iww_ref, ikn_ref, thr_ref, jb_ref, kn_ref,
                           *, nb_seq, T, past, k_sel):
    R8 = nb_seq * TQ8
    kp = kp_ref[...].reshape(R8, past)
    rowi = lax.broadcasted_iota(I32, (TQ8, LANES), 0)
    col1 = lax.broadcasted_iota(I32, (TQ8, LANES), 1)
    kn = jnp.concatenate(
        [jnp.where((col1 <= rowi) & (col1 < T),
                   _sample_score_keys(_dot_nt(iq_ref[s], ikn_ref[s]), iww_ref[s]), jnp.int32(INT_MIN))
         for s in range(nb_seq)], axis=0)
    coln = lax.broadcasted_iota(I32, (R8, LANES), 1)
    colp = lax.broadcasted_iota(I32, (R8, past), 1)

    def count(pred):
        hit = jnp.where(pred(kp, colp), 1.0, 0.0)
        part = jnp.where(pred(kn, coln + past), 1.0, 0.0)
        for t in range(past // LANES):
            part = part + hit[:, t * LANES:(t + 1) * LANES]
        return jnp.sum(part, axis=1, keepdims=True)

    kf = float(k_sel)
    thr = _kth_largest_key(lambda cand: count(lambda kk, col: kk >= cand), kf, (R8, 1))
    thr = jnp.maximum(thr, jnp.int32(KEY_NEG_INF + 1))
    need = kf - count(lambda kk, col: kk > thr)
    nb = (past + LANES - 1).bit_length()

    def body(t, pfx):
        cand = pfx | lax.shift_left(jnp.int32(1), jnp.int32(nb - 1) - t)
        cnt = count(lambda kk, col: (kk == thr) & (col < cand))
        return jnp.where(cnt < need, cand, pfx)

    pfx = lax.fori_loop(0, nb, body, jnp.zeros((R8, 1), I32))
    thr_ref[...] = jnp.broadcast_to(thr, (R8, LANES)).reshape(nb_seq, TQ8, LANES)
    jb_ref[...] = jnp.broadcast_to(pfx + 1, (R8, LANES)).reshape(nb_seq, TQ8, LANES)
    kn_ref[...] = kn.reshape(nb_seq, TQ8, LANES)


def _dsa_sample_attn_kernel(pt_ref, *refs, n_pages, T, past):
    k_refs = refs[:n_pages]
    v_refs = refs[n_pages:2 * n_pages]
    (q_ref, kp_ref, thr_ref, jb_ref, kn_ref, knew_ref, vnew_ref,
     o_ref, m_ref, l_ref, acc_ref) = refs[2 * n_pages:]
    p = pl.program_id(1)
    R = T * ATT_HEADS

    @pl.when(p == 0)
    def _():
        m_ref[...] = jnp.full((R, 1), NEG_BIG, F32)
        l_ref[...] = jnp.zeros((R, 1), F32)
        acc_ref[...] = jnp.zeros((R, KV_WIDTH), F32)

    thr = thr_ref[0][0:T, 0:1]
    jb = jb_ref[0][0:T, 0:1]

    def step(kk, col0, s, pv):
        n = kk.shape[1]
        col = col0 + lax.broadcasted_iota(I32, (1, n), 1)
        sel = (kk >= thr) & (kk < jnp.int32(KEY_POS_INF)) & ((kk > thr) | (col < jb))
        bias = jnp.where(sel, 0.0, -jnp.inf)
        s = (s.reshape(T, ATT_HEADS, n) + bias[:, None, :]).reshape(R, n)
        m = m_ref[...]
        m_new = jnp.maximum(m, jnp.max(s, axis=1, keepdims=True))
        alpha = jnp.exp(m - m_new)
        pr = jnp.exp(s - m_new)
        l_ref[...] = alpha * l_ref[...] + jnp.sum(pr, axis=1, keepdims=True)
        acc_ref[...] = alpha * acc_ref[...] + pv(pr.astype(BF16))
        m_ref[...] = m_new

    k_t = jnp.concatenate([r[0, 0] for r in k_refs], axis=1).astype(BF16)
    v_t = jnp.concatenate([r[0, 0] for r in v_refs], axis=1).astype(BF16)
    step(kp_ref[0][0:T], p * (n_pages * PAGE_SIZE), _dot(q_ref[0], k_t), lambda pr: _dot_nt(pr, v_t))

    @pl.when(p == pl.num_programs(1) - 1)
    def _():
        step(kn_ref[0][0:T], past, _dot_nt(q_ref[0], knew_ref[0]), lambda pr: _dot(pr, vnew_ref[0]))
        o_ref[0] = acc_ref[...] / l_ref[...]


def _pages_per_step(npages, want):
    while npages % want:
        want //= 2
    return want


def dsa_sample(q, k, v, iq, iw, ik, ck_t, cv_t, cik_t, layer, page_table):
    b, T, _ = q.shape
    assert T <= TQ8
    npages = page_table.shape[1]
    past = npages * PAGE_SIZE
    k_sel = max(1, min(TOPK_MAX, (past + T) // 4))
    R = T * ATT_HEADS
    R8 = TQ8 * IDX_HEADS
    pad_to = lambda a, n: jnp.pad(a, ((0, 0), (0, n - a.shape[1]), (0, 0)))
    iq_rows = pad_to(iq.reshape(b, T * IDX_HEADS, IDX_DIM), R8).astype(BF16)
    iww = jnp.broadcast_to(pad_to(iw.reshape(b, T * IDX_HEADS, 1), R8), (b, R8, LANES)).astype(F32)
    ik_new, k_new, v_new = (pad_to(a, LANES).astype(BF16) for a in (ik, k, v))
    head_group = jnp.arange(ATT_HEADS) // (ATT_HEADS // ATT_KV_HEADS)
    onehot = (head_group[:, None] == jnp.arange(ATT_KV_HEADS)[None, :]).astype(F32)
    q_bd = (q.reshape(b, T, ATT_HEADS, 1, ATT_HEAD_DIM) * onehot[None, None, :, :, None])
    q_bd = q_bd.reshape(b, R, KV_WIDTH).astype(BF16)

    def page_spec(rows, per_step, j):
        return pl.BlockSpec((1, 1, rows, PAGE_SIZE),
                            lambda bi, p, pt: (layer, pt[bi, p * per_step + j], 0, 0))

    fixed = lambda shape: pl.BlockSpec((1,) + shape, lambda bi, p, pt: (bi, 0, 0))

    ps = _pages_per_step(npages, 32)
    keys_past = pl.pallas_call(
        functools.partial(_dsa_sample_scores_kernel, n_pages=ps),
        grid_spec=pltpu.PrefetchScalarGridSpec(
            num_scalar_prefetch=1, grid=(b, npages // ps),
            in_specs=[page_spec(IDX_DIM, ps, j) for j in range(ps)]
            + [fixed((R8, IDX_DIM)), fixed((R8, LANES))],
            out_specs=pl.BlockSpec((1, TQ8, ps * PAGE_SIZE), lambda bi, p, pt: (bi, 0, p))),
        out_shape=jax.ShapeDtypeStruct((b, TQ8, past), I32),
        compiler_params=_cparams("parallel", "arbitrary"),
        name="dsa_sample_scores",
    )(page_table, *([cik_t] * ps), iq_rows, iww)

    nb_seq = SUBLANES if b % SUBLANES == 0 else 1
    seq_spec = lambda r, w: pl.BlockSpec((nb_seq, r, w), lambda bi: (bi, 0, 0))
    thr, jb, keys_new = pl.pallas_call(
        functools.partial(_dsa_sample_thr_kernel, nb_seq=nb_seq, T=T, past=past, k_sel=k_sel),
        grid=(b // nb_seq,),
        in_specs=[seq_spec(TQ8, past), seq_spec(R8, IDX_DIM), seq_spec(R8, LANES),
                  seq_spec(LANES, IDX_DIM)],
        out_specs=[seq_spec(TQ8, LANES)] * 3,
        out_shape=[jax.ShapeDtypeStruct((b, TQ8, LANES), I32)] * 3,
        compiler_params=_cparams("parallel"),
        name="dsa_sample_threshold",
    )(keys_past, iq_rows, iww, ik_new)

    pa = _pages_per_step(npages, 16)
    o = pl.pallas_call(
        functools.partial(_dsa_sample_attn_kernel, n_pages=pa, T=T, past=past),
        grid_spec=pltpu.PrefetchScalarGridSpec(
            num_scalar_prefetch=1, grid=(b, npages // pa),
            in_specs=[page_spec(KV_WIDTH, pa, j) for j in range(pa)] * 2
            + [fixed((R, KV_WIDTH)),
               pl.BlockSpec((1, TQ8, pa * PAGE_SIZE), lambda bi, p, pt: (bi, 0, p)),
               fixed((TQ8, LANES)), fixed((TQ8, LANES)), fixed((TQ8, LANES)),
               fixed((LANES, KV_WIDTH)), fixed((LANES, KV_WIDTH))],
            out_specs=fixed((R, KV_WIDTH)),
            scratch_shapes=[pltpu.VMEM((R, 1), F32), pltpu.VMEM((R, 1), F32),
                            pltpu.VMEM((R, KV_WIDTH), F32)]),
        out_shape=jax.ShapeDtypeStruct((b, R, KV_WIDTH), F32),
        compiler_params=_cparams("parallel", "arbitrary"),
        name="dsa_sample_attention",
    )(page_table, *([ck_t] * pa), *([cv_t] * pa), q_bd, keys_past, thr, jb, keys_new, k_new, v_new)

    o = o.reshape(b, T, ATT_HEADS, ATT_KV_HEADS, ATT_HEAD_DIM)
    o = jnp.sum(o * onehot[None, None, :, :, None], axis=3)
    return o.reshape(b, T, ATT_HEADS * ATT_HEAD_DIM)


def _merge_kernel(ys_ref, ya_ref, yr_ref, g_ref, h_ref, wb_ref, wo_ref, o_ref):
    acc = None
    for n, y_ref in enumerate((ys_ref, ya_ref, yr_ref)):
        pr = _dot(y_ref[...].astype(BF16), wb_ref[n])
        t = jax.nn.sigmoid(g_ref[:, n * D_MODEL:(n + 1) * D_MODEL]) * pr
        acc = t if acc is None else acc + t
    o_ref[...] = h_ref[...] + _dot(acc.astype(BF16), wo_ref[...])


def merge_branches(ys, ya, yr, gates, h, wb, wo):
    M = h.shape[0]
    tm = _pick(M, (256, 128))
    rows = lambda w: pl.BlockSpec((tm, w), lambda i: (i, 0))
    return pl.pallas_call(
        _merge_kernel,
        grid=(M // tm,),
        in_specs=[rows(D_MODEL), rows(D_MODEL), rows(D_MODEL), rows(N_BRANCH * D_MODEL), rows(D_MODEL),
                  pl.BlockSpec((N_BRANCH, D_MODEL, D_MODEL), lambda i: (0, 0, 0)),
                  pl.BlockSpec((D_MODEL, D_MODEL), lambda i: (0, 0))],
        out_specs=rows(D_MODEL),
        out_shape=jax.ShapeDtypeStruct((M, D_MODEL), F32),
        compiler_params=_cparams("parallel"),
        name="merge_branches",
    )(ys, ya, yr, gates, h, wb, wo)


def _ffn_kernel(x_ref, g_ref, wg_ref, wu_ref, wd_ref, o_ref, xn_ref, acc_ref):
    j = pl.program_id(1)

    @pl.when(j == 0)
    def _():
        x = x_ref[...]
        ms = jnp.mean(x * x, axis=-1, keepdims=True)
        xn_ref[...] = (x * lax.rsqrt(ms + EPS) * g_ref[...]).astype(BF16)
        acc_ref[...] = jnp.zeros_like(acc_ref)

    xn = xn_ref[...]
    a = _silu(_dot(xn, wg_ref[...])) * _dot(xn, wu_ref[...])
    acc_ref[...] += _dot(a.astype(BF16), wd_ref[...])

    @pl.when(j == pl.num_programs(1) - 1)
    def _():
        o_ref[...] = x_ref[...] + acc_ref[...]


def ffn_dense(x, g, wg, wu, wd):
    M = x.shape[0]
    tm = _pick(M, (1024, 512, 256, 128))
    tf = 512
    return pl.pallas_call(
        _ffn_kernel,
        grid=(M // tm, D_FF // tf),
        in_specs=[pl.BlockSpec((tm, D_MODEL), lambda i, j: (i, 0)),
                  pl.BlockSpec((1, D_MODEL), lambda i, j: (0, 0)),
                  pl.BlockSpec((D_MODEL, tf), lambda i, j: (0, j)),
                  pl.BlockSpec((D_MODEL, tf), lambda i, j: (0, j)),
                  pl.BlockSpec((tf, D_MODEL), lambda i, j: (j, 0))],
        out_specs=pl.BlockSpec((tm, D_MODEL), lambda i, j: (i, 0)),
        out_shape=jax.ShapeDtypeStruct((M, D_MODEL), F32),
        scratch_shapes=[pltpu.VMEM((tm, D_MODEL), BF16), pltpu.VMEM((tm, D_MODEL), F32)],
        compiler_params=_cparams("parallel", "arbitrary"),
        name="ffn_dense",
    )(x, g.reshape(1, D_MODEL), wg, wu, wd)


def _router_kernel(x_ref, g_ref, wr_ref, gate_ref):
    x = x_ref[...]
    ms = jnp.mean(x * x, axis=-1, keepdims=True)
    xn = x * lax.rsqrt(ms + EPS) * g_ref[...]
    logits = jnp.dot(xn, wr_ref[...], preferred_element_type=F32, precision=lax.Precision.HIGHEST)
    lane = lax.broadcasted_iota(I32, logits.shape, 1)
    logits = jnp.where(lane < N_EXPERTS, logits, -jnp.inf)
    v1 = jnp.max(logits, axis=1, keepdims=True)
    i1 = jnp.min(jnp.where(logits == v1, lane, LANES), axis=1, keepdims=True)
    rest = jnp.where(lane == i1, -jnp.inf, logits)
    v2 = jnp.max(rest, axis=1, keepdims=True)
    i2 = jnp.min(jnp.where(rest == v2, lane, LANES), axis=1, keepdims=True)
    e2 = jnp.exp(v2 - v1)
    w1 = 1.0 / (1.0 + e2)
    w2 = e2 / (1.0 + e2)
    gate_ref[...] = jnp.where(lane == i1, w1, 0.0) + jnp.where(lane == i2, w2, 0.0)


def moe_router(x, g, wr):
    M = x.shape[0]
    tm = _pick(M, (512, 256, 128))
    wr_pad = jnp.pad(wr.astype(F32), ((0, 0), (0, LANES - N_EXPERTS)))
    return pl.pallas_call(
        _router_kernel,
        grid=(M // tm,),
        in_specs=[pl.BlockSpec((tm, D_MODEL), lambda i: (i, 0)),
                  pl.BlockSpec((1, D_MODEL), lambda i: (0, 0)),
                  pl.BlockSpec((D_MODEL, LANES), lambda i: (0, 0))],
        out_specs=pl.BlockSpec((tm, LANES), lambda i: (i, 0)),
        out_shape=jax.ShapeDtypeStruct((M, LANES), F32),
        compiler_params=_cparams("parallel"),
        name="moe_router",
    )(x, g.reshape(1, D_MODEL), wr_pad)


def _moe_kernel(x_ref, g_ref, gate_ref, wg_ref, wu_ref, wd_ref, o_ref, xn_ref, acc_ref):
    e = pl.program_id(1)
    j = pl.program_id(2)

    @pl.when((e == 0) & (j == 0))
    def _():
        x = x_ref[...]
        ms = jnp.mean(x * x, axis=-1, keepdims=True)
        xn_ref[...] = (x * lax.rsqrt(ms + EPS) * g_ref[...]).astype(BF16)
        acc_ref[...] = jnp.zeros_like(acc_ref)

    lane = lax.broadcasted_iota(I32, gate_ref.shape, 1)
    gcol = jnp.sum(jnp.where(lane == e, gate_ref[...], 0.0), axis=1, keepdims=True)
    xn = xn_ref[...]
    a = _silu(_dot(xn, wg_ref[0])) * _dot(xn, wu_ref[0])
    acc_ref[...] += _dot((a * gcol).astype(BF16), wd_ref[0])

    @pl.when((e == pl.num_programs(1) - 1) & (j == pl.num_programs(2) - 1))
    def _():
        o_ref[...] = x_ref[...] + acc_ref[...]


def moe_ffn(x, g, gate, wg, wu, wd):
    M = x.shape[0]
    tm = _pick(M, (1024, 512, 256, 128))
    tf = 512
    return pl.pallas_call(
        _moe_kernel,
        grid=(M // tm, N_EXPERTS, D_FF // tf),
        in_specs=[pl.BlockSpec((tm, D_MODEL), lambda i, e, j: (i, 0)),
                  pl.BlockSpec((1, D_MODEL), lambda i, e, j: (0, 0)),
                  pl.BlockSpec((tm, LANES), lambda i, e, j: (i, 0)),
                  pl.BlockSpec((1, D_MODEL, tf), lambda i, e, j: (e, 0, j)),
                  pl.BlockSpec((1, D_MODEL, tf), lambda i, e, j: (e, 0, j)),
                  pl.BlockSpec((1, tf, D_MODEL), lambda i, e, j: (e, j, 0))],
        out_specs=pl.BlockSpec((tm, D_MODEL), lambda i, e, j: (i, 0)),
        out_shape=jax.ShapeDtypeStruct((M, D_MODEL), F32),
        scratch_shapes=[pltpu.VMEM((tm, D_MODEL), BF16), pltpu.VMEM((tm, D_MODEL), F32)],
        compiler_params=_cparams("parallel", "arbitrary", "arbitrary"),
        name="moe_ffn",
    )(x, g.reshape(1, D_MODEL), gate, wg, wu, wd)


def _rmsnorm_kernel(x_ref, g_ref, o_ref):
    x = x_ref[...]
    ms = jnp.mean(x * x, axis=-1, keepdims=True)
    o_ref[...] = x * lax.rsqrt(ms + EPS) * g_ref[...]


def rmsnorm(x, g):
    M = x.shape[0]
    tm = _pick(M, (1024, 512, 256, 128))
    return pl.pallas_call(
        _rmsnorm_kernel,
        grid=(M // tm,),
        in_specs=[pl.BlockSpec((tm, D_MODEL), lambda i: (i, 0)),
                  pl.BlockSpec((1, D_MODEL), lambda i: (0, 0))],
        out_specs=pl.BlockSpec((tm, D_MODEL), lambda i: (i, 0)),
        out_shape=jax.ShapeDtypeStruct((M, D_MODEL), F32),
        compiler_params=_cparams("parallel"),
        name="final_rmsnorm",
    )(x, g.reshape(1, D_MODEL))


def _split_w_in(w_in):
    offs = np.cumsum((0,) + IN_SPLITS)
    seg = {n: w_in[:, offs[i]:offs[i + 1]] for i, n in enumerate(
        ("z", "xbc", "dt", "q", "k", "v", "iq", "iw", "ik", "rq", "rk", "rv", "rg", "gates"))}
    cat = lambda *names: jnp.concatenate([seg[n] if isinstance(n, str) else n for n in names], axis=1)
    dt_wide = jnp.repeat(seg["dt"], SSM_HEAD_DIM, axis=1)
    small = cat("dt", "iw", jnp.zeros((D_MODEL, LANES - 2 * IDX_HEADS - IDX_DIM), w_in.dtype), "ik")
    q_scaled = seg["q"] * (ATT_HEAD_DIM ** -0.5)
    groups = dict(ssm=cat("z", "xbc", dt_wide), att=cat(q_scaled, "k", "v"), idx=cat("iq", small),
                  ret=cat("rq", "rk", "rv", "rg"), gate=seg["gates"])
    return {n: w.astype(BF16) for n, w in groups.items()}


def _rope_tables(pos):
    half = RET_DK // 2
    inv = ROPE_BASE ** (-jnp.arange(half, dtype=F32) / half)
    ang = pos.astype(F32)[:, None] * inv[None, :]
    cos, sin = jnp.cos(ang), jnp.sin(ang)
    return jnp.concatenate([cos, cos], axis=1), jnp.concatenate([-sin, sin], axis=1)


def _pad_rows(a, L):
    return jnp.pad(a, ((0, 0), (0, L - a.shape[1]), (0, 0)))


def _mixers(h, pos0, conv0, ssm0, ret0, attend, lw):
    b, L, _ = h.shape
    hf = h.reshape(b * L, D_MODEL)
    proj = {n: rms_matmul(hf, lw["norm_g"], w).reshape(b, L, -1) for n, w in lw["w_in"].items()}
    att, idx = proj["att"], proj["idx"]
    q = att[..., :1024]
    k = att[..., 1024:1280]
    v = att[..., 1280:1536]
    iq = idx[..., :1024]
    small = idx[..., 1024:]
    ik = small[..., LANES - IDX_DIM:]

    Lp = -(-L // CHUNK) * CHUNK
    last_valid = L - (Lp - CHUNK)
    dtT = jnp.swapaxes(_pad_rows(small[..., :SSM_HEADS], Lp), 1, 2)
    s0 = jnp.transpose(ssm0, (0, 2, 1, 3)).reshape(b, SSM_STATE, SSM_INNER)
    ys, conv_new, s_new = ssd_branch(_pad_rows(proj["ssm"], Lp), dtT, conv0, s0, lw["conv_w"], lw["conv_b"],
                                     lw["dt_bias"], lw["a_log"], lw["d_skip"], lw["ssm_norm_g"], last_valid)
    ssm_new = jnp.transpose(s_new.reshape(b, SSM_STATE, SSM_HEADS, SSM_HEAD_DIM), (0, 2, 1, 3))

    cos2, sin2 = _rope_tables(pos0 + jnp.arange(Lp))
    yr, ret_new = ret_branch(_pad_rows(proj["ret"], Lp), cos2, sin2, ret0, last_valid)

    ya = attend(q, k, v, iq, small, ik)

    ys = ys[:, :L].reshape(b * L, -1)
    yr = yr[:, :L].reshape(b * L, -1)
    out = merge_branches(ys, ya.reshape(b * L, -1), yr, proj["gate"].reshape(b * L, -1), hf,
                         lw["w_branch"], lw["w_out"])
    return out.reshape(b, L, D_MODEL), (k.reshape(b, L, ATT_KV_HEADS, ATT_HEAD_DIM),
                                        v.reshape(b, L, ATT_KV_HEADS, ATT_HEAD_DIM),
                                        ik, ssm_new, conv_new, ret_new)


def _attend_prompt(q, k, v, iq, small, ik):
    b, L, _ = q.shape
    hm = lambda a, nh: jnp.transpose(a.reshape(b, L, nh, -1), (0, 2, 1, 3)).astype(BF16)
    v_t = jnp.transpose(v.reshape(b, L, ATT_KV_HEADS, ATT_HEAD_DIM), (0, 2, 3, 1)).astype(BF16)
    v_t = jnp.concatenate([v_t, jnp.ones((b, ATT_KV_HEADS, V_ROWS - ATT_HEAD_DIM, L), BF16)], axis=2)
    iw_t = jnp.swapaxes(small[..., IDX_HEADS:2 * IDX_HEADS], 1, 2)
    o = dsa_prompt(hm(q, ATT_HEADS), hm(iq, IDX_HEADS), iw_t, ik.astype(BF16), hm(k, ATT_KV_HEADS), v_t)
    return jnp.transpose(o, (0, 3, 1, 2)).reshape(b, L, ATT_HEADS * ATT_HEAD_DIM)


def _attend_sample(q, k, v, iq, small, ik, *, ck_t, cv_t, cik_t, layer, page_table):
    iw = small[..., IDX_HEADS:2 * IDX_HEADS]
    return dsa_sample(q, k, v, iq, iw, ik, ck_t, cv_t, cik_t, layer, page_table)


def _channel_mixer(h, l, p):
    b, L, _ = h.shape
    hf = h.reshape(b * L, D_MODEL)
    j = l // 2
    if l % 2 == 0:
        out = ffn_dense(hf, p["norm_ffn_g"][l], p["w_ffn_gate"][j].astype(BF16),
                        p["w_ffn_up"][j].astype(BF16), p["w_ffn_down"][j].astype(BF16))
    else:
        gate = moe_router(hf, p["norm_ffn_g"][l], p["w_router"][j])
        out = moe_ffn(hf, p["norm_ffn_g"][l], gate, p["w_moe_gate"][j].astype(BF16),
                      p["w_moe_up"][j].astype(BF16), p["w_moe_down"][j].astype(BF16))
    return out.reshape(b, L, D_MODEL)


def kernel(x_prompt, x_sample, cache_k, cache_v, cache_idx_k, state_ssm, state_conv, state_ret,
           page_table, norm_mix_g, w_in, conv_w, conv_b, dt_bias, a_log, d_skip, ssm_norm_g,
           w_branch, w_out, norm_ffn_g, w_ffn_gate, w_ffn_up, w_ffn_down, w_router,
           w_moe_gate, w_moe_up, w_moe_down, final_norm_g):
    depth = w_in.shape[0]
    past = page_table.shape[1] * PAGE_SIZE
    bp = x_prompt.shape[0]
    p = dict(norm_ffn_g=norm_ffn_g, w_ffn_gate=w_ffn_gate, w_ffn_up=w_ffn_up, w_ffn_down=w_ffn_down,
             w_router=w_router, w_moe_gate=w_moe_gate, w_moe_up=w_moe_up, w_moe_down=w_moe_down)
    n_phys = cache_k.shape[1]
    ck_t = jnp.transpose(cache_k, (0, 1, 3, 4, 2)).reshape(depth, n_phys, KV_WIDTH, PAGE_SIZE)
    cv_t = jnp.transpose(cache_v, (0, 1, 3, 4, 2)).reshape(depth, n_phys, KV_WIDTH, PAGE_SIZE)
    cik_t = jnp.transpose(cache_idx_k, (0, 1, 3, 2))
    hp, hs = x_prompt, x_sample
    st_p, st_s = [], []
    for l in range(depth):
        lw = dict(norm_g=norm_mix_g[l], w_in=_split_w_in(w_in[l]), conv_w=conv_w[l], conv_b=conv_b[l],
                  dt_bias=dt_bias[l], a_log=a_log[l], d_skip=d_skip[l], ssm_norm_g=ssm_norm_g[l],
                  w_branch=w_branch[l].astype(BF16), w_out=w_out[l].astype(BF16))
        hp, sp = _mixers(hp, 0,
                         jnp.zeros((bp, CONV_WIDTH - 1, CONV_DIM), F32),
                         jnp.zeros((bp, SSM_HEADS, SSM_STATE, SSM_HEAD_DIM), F32),
                         jnp.zeros((bp, RET_HEADS, RET_DK, RET_DV), F32),
                         _attend_prompt, lw)
        attend_s = functools.partial(_attend_sample, ck_t=ck_t, cv_t=cv_t, cik_t=cik_t, layer=l,
                                     page_table=page_table)
        hs, ss = _mixers(hs, past, state_conv[l], state_ssm[l], state_ret[l], attend_s, lw)
        hp = _channel_mixer(hp, l, p)
        hs = _channel_mixer(hs, l, p)
        st_p.append(sp)
        st_s.append(ss)
    y_prompt = rmsnorm(hp.reshape(-1, D_MODEL), final_norm_g).reshape(hp.shape)
    y_sample = rmsnorm(hs.reshape(-1, D_MODEL), final_norm_g).reshape(hs.shape)
    stack = lambda sts, i: jnp.stack([s[i] for s in sts])
    return (y_prompt, y_sample,
            stack(st_p, 0), stack(st_p, 1), stack(st_p, 2), stack(st_p, 3), stack(st_p, 4), stack(st_p, 5),
            stack(st_s, 0), stack(st_s, 1), stack(st_s, 2), stack(st_s, 3), stack(st_s, 4), stack(st_s, 5))
```

```python
import functools
import math

import jax
import jax.numpy as jnp
import numpy as np
from jax import lax
from jax.experimental import pallas as pl
from jax.experimental.pallas import tpu as pltpu

F32 = jnp.float32
BF16 = jnp.bfloat16
I32 = jnp.int32

D_MODEL = 1024
PAGE_SIZE = 128
SSM_HEADS = 16
SSM_HEAD_DIM = 64
SSM_INNER = 1024
SSM_GROUPS = 2
SSM_STATE = 128
CONV_WIDTH = 4
CONV_DIM = 1536
ATT_HEADS = 16
ATT_KV_HEADS = 4
ATT_HEAD_DIM = 64
KV_WIDTH = 256
IDX_HEADS = 16
IDX_DIM = 64
TOPK_MAX = 256
RET_HEADS = 4
RET_DK = 128
RET_DV = 256
ROPE_BASE = 10000.0
N_BRANCH = 3
D_FF = 3584
N_EXPERTS = 8
EPS = 1e-6
IN_SPLITS = (1024, 1536, 16, 1024, 256, 256, 1024, 16, 64, 512, 512, 1024, 1024, 3072)

LANES = 128
SUBLANES = 8
VMEM_LIMIT = 48 * 1024 * 1024
MOE_VMEM_LIMIT = 56 * 1024 * 1024

CHUNK = 128
INT_MIN = -(2 ** 31)
KEY_POS_INF = 0x7F800000
KEY_NEG_INF = -2139095041
NEG_BIG = -1e30


def _cparams(*sem):
    return pltpu.CompilerParams(dimension_semantics=sem, vmem_limit_bytes=VMEM_LIMIT)


def _dot(a, b):
    return jnp.dot(a, b, preferred_element_type=F32)


def _dot_nt(a, b):
    return lax.dot_general(a, b, (((1,), (1,)), ((), ())), preferred_element_type=F32)


def _dot_tn(a, b):
    return lax.dot_general(a, b, (((0,), (0,)), ((), ())), preferred_element_type=F32)


def _split3(x):
    hi = x.astype(BF16)
    r = x - hi.astype(F32)
    mid = r.astype(BF16)
    lo = (r - mid.astype(F32)).astype(BF16)
    return hi, mid, lo


def _silu(x):
    return x * jax.nn.sigmoid(x)


def _softplus(x):
    return jnp.maximum(x, 0.0) + jnp.log1p(jnp.exp(-jnp.abs(x)))


def _float_key(x):
    x = jnp.where(x == 0.0, 0.0, x)
    b = lax.bitcast_convert_type(x, I32)
    return jnp.where(b >= 0, b, b ^ jnp.int32(0x7FFFFFFF))


def _rms_matmul_kernel(x_ref, g_ref, w_ref, o_ref):
    x = x_ref[...]
    ms = jnp.mean(x * x, axis=-1, keepdims=True)
    xn = (x * lax.rsqrt(ms + EPS) * g_ref[...]).astype(BF16)
    o_ref[...] = _dot(xn, w_ref[...]).astype(o_ref.dtype)


def _pick(n, prefs):
    for p in prefs:
        if n % p == 0:
            return p
    return n


def rms_matmul(x, g, w, out_dtype=F32):
    M, K = x.shape
    N = w.shape[1]
    tm = _pick(M, (512, 256, 128))
    return pl.pallas_call(
        _rms_matmul_kernel,
        grid=(M // tm,),
        in_specs=[pl.BlockSpec((tm, K), lambda i: (i, 0)),
                  pl.BlockSpec((1, K), lambda i: (0, 0)),
                  pl.BlockSpec((K, N), lambda i: (0, 0))],
        out_specs=pl.BlockSpec((tm, N), lambda i: (i, 0)),
        out_shape=jax.ShapeDtypeStruct((M, N), out_dtype),
        compiler_params=_cparams("parallel"),
        name="rms_matmul",
    )(x, g.reshape(1, K), w)


def _ssd_kernel(zxd_ref, dtT_ref, conv0_ref, s0_ref, cw_ref, cb_ref, dtb_ref, alog_ref, dsk_ref,
                ng_ref, dtbT_ref, alogT_ref, y_ref, convn_ref, sn_ref, xpad_ref, st_ref,
                *, Q, last_valid, nc):
    c = pl.program_id(1)
    GW = SSM_INNER // SSM_GROUPS

    @pl.when(c == 0)
    def _():
        xpad_ref[0:8, :] = jnp.zeros((8, CONV_DIM), F32)
        xpad_ref[5:8, :] = conv0_ref[0]
        st_ref[...] = s0_ref[0]

    blk = zxd_ref[0]
    z = blk[:, :SSM_INNER]
    xbc = blk[:, SSM_INNER:SSM_INNER + CONV_DIM]
    dtr = blk[:, SSM_INNER + CONV_DIM:]
    xpad_ref[8:8 + Q, :] = xbc
    cw = cw_ref[...]
    conv = (xpad_ref[5:5 + Q, :] * cw[0:1] + xpad_ref[6:6 + Q, :] * cw[1:2]
            + xpad_ref[7:7 + Q, :] * cw[2:3] + xbc * cw[3:4]) + cb_ref[...]
    xc = _silu(conv)
    xs = xc[:, :SSM_INNER]
    Bm = xc[:, SSM_INNER:SSM_INNER + SSM_GROUPS * SSM_STATE]
    Cm = xc[:, SSM_INNER + SSM_GROUPS * SSM_STATE:]

    row = lax.broadcasted_iota(I32, (Q, 1), 0)
    colq = lax.broadcasted_iota(I32, (1, Q), 1)
    assert last_valid == Q or nc == 1
    lv = last_valid
    valid = row < lv
    tril = (lax.broadcasted_iota(I32, (Q, Q), 0) >= lax.broadcasted_iota(I32, (Q, Q), 1))
    tril_b = jnp.where(tril, 1.0, 0.0).astype(BF16)
    triu_b = jnp.where(lax.broadcasted_iota(I32, (Q, Q), 0) <= lax.broadcasted_iota(I32, (Q, Q), 1),
                       1.0, 0.0).astype(BF16)

    nega = -jnp.exp(alog_ref[...])
    dt = _softplus(dtr + dtb_ref[...])
    la = jnp.where(valid, dt * nega, 0.0)
    cum = sum(_dot(tril_b, p) for p in _split3(la))
    laT = jnp.where(colq < lv, _softplus(dtT_ref[0] + dtbT_ref[...]) * (-jnp.exp(alogT_ref[...])), 0.0)
    cumT = sum(_dot(p, triu_b) for p in _split3(laT))

    ecum = jnp.exp(cum)
    cl = cum[lv - 1:lv, :]
    xdt = xs * dt
    xtail = jnp.where(valid, xdt * jnp.exp(cl - cum), 0.0)
    cdecay = jnp.exp(cl)

    lane = lax.broadcasted_iota(I32, (1, LANES), 1)
    y_groups = []
    for g in range(SSM_GROUPS):
        l0 = g * GW
        Cg = Cm[:, g * SSM_STATE:(g + 1) * SSM_STATE].astype(BF16)
        Bg = Bm[:, g * SSM_STATE:(g + 1) * SSM_STATE].astype(BF16)
        G = _dot_nt(Cg, Bg)
        st_g = st_ref[:, l0:l0 + GW]
        inter = _dot(Cg, st_g.astype(BF16)) * ecum[:, l0:l0 + GW]
        local = _dot_tn(Bg, xtail[:, l0:l0 + GW].astype(BF16))
        st_ref[:, l0:l0 + GW] = st_g * cdecay[:, l0:l0 + GW] + local
        pairs = []
        for p in range(GW // LANES):
            xp = xdt[:, l0 + p * LANES:l0 + (p + 1) * LANES]
            acc = None
            for hh in range(2):
                h = (l0 + p * LANES) // SSM_HEAD_DIM + hh
                ccol = cum[:, h * SSM_HEAD_DIM:h * SSM_HEAD_DIM + 1]
                diff = ccol - cumT[h:h + 1, :]
                dm = jnp.exp(jnp.where(tril, diff, -jnp.inf))
                s = (G * dm).astype(BF16)
                half = (lane >= hh * SSM_HEAD_DIM) & (lane < (hh + 1) * SSM_HEAD_DIM)
                part = _dot(s, jnp.where(half, xp, 0.0).astype(BF16))
                acc = part if acc is None else acc + part
            pairs.append(acc)
        y_groups.append(jnp.concatenate(pairs, axis=1) + inter)
    y = jnp.concatenate(y_groups, axis=1)

    y = (y + dsk_ref[...] * xs) * _silu(z)
    outs = []
    for g in range(SSM_GROUPS):
        seg = y[:, g * GW:(g + 1) * GW]
        ms = jnp.mean(seg * seg, axis=-1, keepdims=True)
        outs.append(seg * lax.rsqrt(ms + EPS) * ng_ref[:, g * GW:(g + 1) * GW])
    y_ref[0] = jnp.concatenate(outs, axis=1)

    @pl.when(c == nc - 1)
    def _():
        convn_ref[0] = xpad_ref[5 + last_valid:8 + last_valid, :]
        sn_ref[0] = st_ref[...]

    xpad_ref[0:8, :] = xpad_ref[Q:Q + 8, :]


def ssd_branch(zxd, dtT, conv0, s0, conv_w, conv_b, dt_bias, a_log, d_skip, norm_g, last_valid):
    b, L, W = zxd.shape
    Q = CHUNK
    nc = L // Q
    rep = lambda v: jnp.repeat(v.astype(F32), SSM_HEAD_DIM).reshape(1, SSM_INNER)
    col = lambda v: v.astype(F32).reshape(SSM_HEADS, 1)
    full = lambda shape: pl.BlockSpec(shape, lambda i, c: (0,) * len(shape))
    kern = functools.partial(_ssd_kernel, Q=Q, last_valid=last_valid, nc=nc)
    return pl.pallas_call(
        kern,
        grid=(b, nc),
        in_specs=[pl.BlockSpec((1, Q, W), lambda i, c: (i, c, 0)),
                  pl.BlockSpec((1, SSM_HEADS, Q), lambda i, c: (i, 0, c)),
                  pl.BlockSpec((1, CONV_WIDTH - 1, CONV_DIM), lambda i, c: (i, 0, 0)),
                  pl.BlockSpec((1, SSM_STATE, SSM_INNER), lambda i, c: (i, 0, 0)),
                  full((CONV_WIDTH, CONV_DIM)), full((1, CONV_DIM)),
                  full((1, SSM_INNER)), full((1, SSM_INNER)), full((1, SSM_INNER)), full((1, SSM_INNER)),
                  full((SSM_HEADS, 1)), full((SSM_HEADS, 1))],
        out_specs=[pl.BlockSpec((1, Q, SSM_INNER), lambda i, c: (i, c, 0)),
                   pl.BlockSpec((1, CONV_WIDTH - 1, CONV_DIM), lambda i, c: (i, 0, 0)),
                   pl.BlockSpec((1, SSM_STATE, SSM_INNER), lambda i, c: (i, 0, 0))],
        out_shape=[jax.ShapeDtypeStruct((b, L, SSM_INNER), F32),
                   jax.ShapeDtypeStruct((b, CONV_WIDTH - 1, CONV_DIM), F32),
                   jax.ShapeDtypeStruct((b, SSM_STATE, SSM_INNER), F32)],
        scratch_shapes=[pltpu.VMEM((Q + 8, CONV_DIM), F32), pltpu.VMEM((SSM_STATE, SSM_INNER), F32)],
        compiler_params=_cparams("parallel", "arbitrary"),
        name="ssd_branch",
    )(zxd, dtT, conv0, s0, conv_w.astype(F32), conv_b.reshape(1, CONV_DIM).astype(F32),
      rep(dt_bias), rep(a_log), rep(d_skip), norm_g.reshape(1, SSM_INNER).astype(F32),
      col(dt_bias), col(a_log))


def _ret_kernel(x_ref, cos_ref, sin_ref, r0_ref, y_ref, rn_ref, st_ref, *, Q, last_valid, nc):
    c = pl.program_id(1)

    @pl.when(c == 0)
    def _():
        st_ref[...] = r0_ref[0]

    blk = x_ref[0]
    cos2 = cos_ref[...]
    sin2 = sin_ref[...]
    ri = lax.broadcasted_iota(I32, (Q, Q), 0)
    ci = lax.broadcasted_iota(I32, (Q, Q), 1)
    dij = (ri - ci).astype(F32)
    row = lax.broadcasted_iota(I32, (Q, 1), 0)
    assert last_valid == Q or nc == 1
    lv = last_valid
    rowf = row.astype(F32)
    lvf = float(lv)
    KO = RET_HEADS * RET_DK
    outs = []
    for h in range(RET_HEADS):
        lg = math.log1p(-2.0 ** (-5.0 - h))
        qh = blk[:, h * RET_DK:(h + 1) * RET_DK]
        kh = blk[:, KO + h * RET_DK:KO + (h + 1) * RET_DK]
        vh = blk[:, 2 * KO + h * RET_DV:2 * KO + (h + 1) * RET_DV].astype(BF16)
        gh = blk[:, 2 * KO + RET_HEADS * RET_DV + h * RET_DV:2 * KO + RET_HEADS * RET_DV + (h + 1) * RET_DV]
        qr = qh * cos2 + pltpu.roll(qh, RET_DK // 2, 1) * sin2
        kr = (kh * cos2 + pltpu.roll(kh, RET_DK // 2, 1) * sin2) * (RET_DK ** -0.5)
        qb = qr.astype(BF16)
        dm = jnp.exp(jnp.where(ri >= ci, dij * lg, -jnp.inf))
        s = (_dot_nt(qb, kr.astype(BF16)) * dm).astype(BF16)
        st_h = st_ref[h]
        y = _dot(s, vh) + jnp.exp((rowf + 1.0) * lg) * _dot(qb, st_h.astype(BF16))
        ktail = jnp.where(row < lv, kr * jnp.exp((lvf - 1.0 - rowf) * lg), 0.0)
        st_ref[h] = st_h * math.exp(lvf * lg) + _dot_tn(ktail.astype(BF16), vh)
        ms = jnp.mean(y * y, axis=-1, keepdims=True)
        outs.append(y * lax.rsqrt(ms + EPS) * _silu(gh))
    y_ref[0] = jnp.concatenate(outs, axis=1)

    @pl.when(c == nc - 1)
    def _():
        rn_ref[0] = st_ref[...]


def ret_branch(x, cos2, sin2, r0, last_valid):
    b, L, W = x.shape
    Q = CHUNK
    nc = L // Q
    kern = functools.partial(_ret_kernel, Q=Q, last_valid=last_valid, nc=nc)
    return pl.pallas_call(
        kern,
        grid=(b, nc),
        in_specs=[pl.BlockSpec((1, Q, W), lambda i, c: (i, c, 0)),
                  pl.BlockSpec((Q, RET_DK), lambda i, c: (c, 0)),
                  pl.BlockSpec((Q, RET_DK), lambda i, c: (c, 0)),
                  pl.BlockSpec((1, RET_HEADS, RET_DK, RET_DV), lambda i, c: (i, 0, 0, 0))],
        out_specs=[pl.BlockSpec((1, Q, RET_HEADS * RET_DV), lambda i, c: (i, c, 0)),
                   pl.BlockSpec((1, RET_HEADS, RET_DK, RET_DV), lambda i, c: (i, 0, 0, 0))],
        out_shape=[jax.ShapeDtypeStruct((b, L, RET_HEADS * RET_DV), F32),
                   jax.ShapeDtypeStruct((b, RET_HEADS, RET_DK, RET_DV), F32)],
        scratch_shapes=[pltpu.VMEM((RET_HEADS, RET_DK, RET_DV), F32)],
        compiler_params=_cparams("parallel", "arbitrary"),
        name="ret_branch",
    )(x, cos2, sin2, r0)


def _kth_largest_key(count_ge, k, shape):
    def body(t, prefix):
        bit = lax.shift_left(jnp.int32(1), jnp.int32(31) - t)
        cand = prefix | bit
        cnt = count_ge(cand ^ jnp.int32(INT_MIN))
        return jnp.where(cnt >= k, cand, prefix)

    prefix = lax.fori_loop(0, 32, body, jnp.zeros(shape, I32))
    return prefix ^ jnp.int32(INT_MIN)


def _dsa_prompt_kernel(q_ref, iq_ref, iwt_ref, ik_ref, k_ref, vt_ref, o_ref, keys_ref, jb_ref,
                       m_ref, acc_ref, *, tq, ck, cka, L, k_sel):
    i = pl.program_id(1)
    G = ATT_HEADS // ATT_KV_HEADS
    PART = 4 * SUBLANES
    nk = lax.div((i + 1) * tq + (ck - 1), ck)
    qpos = i * tq + lax.broadcasted_iota(I32, (1, tq), 1)
    rowk = lax.broadcasted_iota(I32, (ck, 1), 0)
    iw = iwt_ref[0] * (IDX_HEADS ** -0.5 * IDX_DIM ** -0.5)

    def score_body(kc, carry):
        off = pl.multiple_of(kc * ck, ck)
        ikc = ik_ref[0, pl.ds(off, ck), :]
        acc = jnp.zeros((ck, tq), F32)
        for h4 in range(0, IDX_HEADS, G):
            d = _dot_nt(ikc, iq_ref[0, h4:h4 + G].reshape(G * tq, IDX_DIM))
            for j in range(G):
                acc = acc + jnp.maximum(d[:, j * tq:(j + 1) * tq], 0.0) * iw[h4 + j:h4 + j + 1, :]
        key = jnp.where(off + rowk <= qpos, _float_key(acc), jnp.int32(INT_MIN))
        keys_ref[pl.ds(off, ck), :] = key
        return carry

    lax.fori_loop(0, nk, score_body, 0)

    def count(pred):
        def body(kc, part):
            off = pl.multiple_of(kc * ck, ck)
            hit = jnp.where(pred(keys_ref[pl.ds(off, ck), :], off + rowk), 1.0, 0.0)
            return part + jnp.sum(hit.reshape(ck // PART, PART, tq), axis=0)
        part = lax.fori_loop(0, nk, body, jnp.zeros((PART, tq), F32))
        return jnp.sum(part, axis=0, keepdims=True)

    kf = float(k_sel)
    thr = _kth_largest_key(lambda cand: count(lambda kk, pos: kk >= cand), kf, (1, tq))
    thr = jnp.maximum(thr, jnp.int32(KEY_NEG_INF + 1))
    n_ge = count(lambda kk, pos: kk >= thr)
    jb_ref[...] = jnp.full((1, tq), L, I32)

    @pl.when(jnp.max(n_ge) > kf)
    def _():
        need = kf - count(lambda kk, pos: kk > thr)
        nb = max(1, (L - 1).bit_length())

        def body(t, pfx):
            cand = pfx | lax.shift_left(jnp.int32(1), jnp.int32(nb - 1) - t)
            cnt = count(lambda kk, pos: (kk == thr) & (pos < cand))
            return jnp.where(cnt < need, cand, pfx)

        pfx = lax.fori_loop(0, nb, body, jnp.zeros((1, tq), I32))
        jb_ref[...] = jnp.where(n_ge > kf, pfx + 1, L)

    jb = jb_ref[...]

    m_ref[...] = jnp.full(m_ref.shape, NEG_BIG, F32)
    acc_ref[...] = jnp.zeros(acc_ref.shape, F32)
    rowa = lax.broadcasted_iota(I32, (cka, 1), 0)

    def att_body(kc, carry):
        off = pl.multiple_of(kc * cka, cka)
        kk = keys_ref[pl.ds(off, cka), :]
        sel = (kk >= thr) & (kk < jnp.int32(KEY_POS_INF)) & ((kk > thr) | (off + rowa < jb))
        bias = jnp.where(sel, 0.0, -jnp.inf)
        bias = jnp.concatenate([bias] * G, axis=1)
        for g in range(ATT_KV_HEADS):
            kg = k_ref[0, g, pl.ds(off, cka), :]
            vg = vt_ref[0, g, :, pl.ds(off, cka)]
            qg = q_ref[0, g * G:(g + 1) * G].reshape(G * tq, ATT_HEAD_DIM)
            s = _dot_nt(kg, qg) + bias
            m = m_ref[g]
            smax = jnp.max(jnp.max(s.reshape(cka // PART, PART, G * tq), axis=0), axis=0, keepdims=True)
            m_new = jnp.maximum(m, smax)
            p = jnp.exp(s - m_new)
            acc_ref[g] = jnp.exp(m - m_new) * acc_ref[g] + _dot(vg, p.astype(BF16))
            m_ref[g] = m_new
        return carry

    lax.fori_loop(0, lax.div((i + 1) * tq + (cka - 1), cka), att_body, 0)
    for g in range(ATT_KV_HEADS):
        acc = acc_ref[g]
        og = acc[:ATT_HEAD_DIM] / acc[ATT_HEAD_DIM:ATT_HEAD_DIM + 1]
        for hh in range(G):
            o_ref[0, g * G + hh] = og[:, hh * tq:(hh + 1) * tq]


V_ROWS = ATT_HEAD_DIM + 16


def dsa_prompt(q_hm, iq_hm, iw_t, ik, k_hm, v_t):
    b, _, L, _ = q_hm.shape
    tq = min(128, L)
    ck = min(512, L)
    cka = min(512, L)
    gq = (ATT_HEADS // ATT_KV_HEADS) * tq
    k_sel = max(1, min(TOPK_MAX, L // 4))
    kern = functools.partial(_dsa_prompt_kernel, tq=tq, ck=ck, cka=cka, L=L, k_sel=k_sel)
    return pl.pallas_call(
        kern,
        grid=(b, L // tq),
        in_specs=[pl.BlockSpec((1, ATT_HEADS, tq, ATT_HEAD_DIM), lambda bi, i: (bi, 0, i, 0)),
                  pl.BlockSpec((1, IDX_HEADS, tq, IDX_DIM), lambda bi, i: (bi, 0, i, 0)),
                  pl.BlockSpec((1, IDX_HEADS, tq), lambda bi, i: (bi, 0, i)),
                  pl.BlockSpec((1, L, IDX_DIM), lambda bi, i: (bi, 0, 0)),
                  pl.BlockSpec((1, ATT_KV_HEADS, L, ATT_HEAD_DIM), lambda bi, i: (bi, 0, 0, 0)),
                  pl.BlockSpec((1, ATT_KV_HEADS, V_ROWS, L), lambda bi, i: (bi, 0, 0, 0))],
        out_specs=pl.BlockSpec((1, ATT_HEADS, ATT_HEAD_DIM, tq), lambda bi, i: (bi, 0, 0, i)),
        out_shape=jax.ShapeDtypeStruct((b, ATT_HEADS, ATT_HEAD_DIM, L), F32),
        scratch_shapes=[pltpu.VMEM((L, tq), I32), pltpu.VMEM((1, tq), I32),
                        pltpu.VMEM((ATT_KV_HEADS, 1, gq), F32),
                        pltpu.VMEM((ATT_KV_HEADS, V_ROWS, gq), F32)],
        compiler_params=_cparams("parallel", "arbitrary"),
        name="dsa_prompt",
    )(q_hm, iq_hm, iw_t, ik, k_hm, v_t)


TQ8 = SUBLANES


def _sample_score_keys(d, iww):
    n = d.shape[1]
    r = jnp.maximum(d, 0.0) * (iww[:, 0:1] * (IDX_HEADS ** -0.5 * IDX_DIM ** -0.5))
    return _float_key(jnp.sum(r.reshape(TQ8, IDX_HEADS, n), axis=1))


def _dsa_sample_scores_kernel(pt_ref, *refs, n_pages):
    page_refs = refs[:n_pages]
    iq_ref, iww_ref, keys_ref = refs[n_pages:]
    ik_t = jnp.concatenate([r[0, 0] for r in page_refs], axis=1).astype(BF16)
    keys_ref[0] = _sample_score_keys(_dot(iq_ref[0], ik_t), iww_ref[0])


def _dsa_sample_thr_kernel(kp_ref, iq_ref, iww_ref, ikn_ref, thr_ref, jb_ref, kn_ref,
                           *, nb_seq, T, past, k_sel):
    R8 = nb_seq * TQ8
    kp = kp_ref[...].reshape(R8, past)
    rowi = lax.broadcasted_iota(I32, (TQ8, LANES), 0)
    col1 = lax.broadcasted_iota(I32, (TQ8, LANES), 1)
    kn = jnp.concatenate(
        [jnp.where((col1 <= rowi) & (col1 < T),
                   _sample_score_keys(_dot_nt(iq_ref[s], ikn_ref[s]), iww_ref[s]), jnp.int32(INT_MIN))
         for s in range(nb_seq)], axis=0)
    coln = lax.broadcasted_iota(I32, (R8, LANES), 1)
    colp = lax.broadcasted_iota(I32, (R8, past), 1)

    def count(pred):
        hit = jnp.where(pred(kp, colp), 1.0, 0.0)
        part = jnp.where(pred(kn, coln + past), 1.0, 0.0)
        for t in range(past // LANES):
            part = part + hit[:, t * LANES:(t + 1) * LANES]
        return jnp.sum(part, axis=1, keepdims=True)

    kf = float(k_sel)
    thr = _kth_largest_key(lambda cand: count(lambda kk, col: kk >= cand), kf, (R8, 1))
    thr = jnp.maximum(thr, jnp.int32(KEY_NEG_INF + 1))
    need = kf - count(lambda kk, col: kk > thr)
    nb = (past + LANES - 1).bit_length()

    def body(t, pfx):
        cand = pfx | lax.shift_left(jnp.int32(1), jnp.int32(nb - 1) - t)
        cnt = count(lambda kk, col: (kk == thr) & (col < cand))
        return jnp.where(cnt < need, cand, pfx)

    pfx = lax.fori_loop(0, nb, body, jnp.zeros((R8, 1), I32))
    thr_ref[...] = jnp.broadcast_to(thr, (R8, LANES)).reshape(nb_seq, TQ8, LANES)
    jb_ref[...] = jnp.broadcast_to(pfx + 1, (R8, LANES)).reshape(nb_seq, TQ8, LANES)
    kn_ref[...] = kn.reshape(nb_seq, TQ8, LANES)


def _dsa_sample_attn_kernel(pt_ref, *refs, n_pages, T, past):
    k_refs = refs[:n_pages]
    v_refs = refs[n_pages:2 * n_pages]
    (q_ref, kp_ref, thr_ref, jb_ref, kn_ref, knew_ref, vnew_ref,
     o_ref, m_ref, l_ref, acc_ref) = refs[2 * n_pages:]
    p = pl.program_id(1)
    R = T * ATT_HEADS

    @pl.when(p == 0)
    def _():
        m_ref[...] = jnp.full((R, 1), NEG_BIG, F32)
        l_ref[...] = jnp.zeros((R, 1), F32)
        acc_ref[...] = jnp.zeros((R, KV_WIDTH), F32)

    thr = thr_ref[0][0:T, 0:1]
    jb = jb_ref[0][0:T, 0:1]

    def step(kk, col0, s, pv):
        n = kk.shape[1]
        col = col0 + lax.broadcasted_iota(I32, (1, n), 1)
        sel = (kk >= thr) & (kk < jnp.int32(KEY_POS_INF)) & ((kk > thr) | (col < jb))
        bias = jnp.where(sel, 0.0, -jnp.inf)
        s = (s.reshape(T, ATT_HEADS, n) + bias[:, None, :]).reshape(R, n)
        m = m_ref[...]
        m_new = jnp.maximum(m, jnp.max(s, axis=1, keepdims=True))
        alpha = jnp.exp(m - m_new)
        pr = jnp.exp(s - m_new)
        l_ref[...] = alpha * l_ref[...] + jnp.sum(pr, axis=1, keepdims=True)
        acc_ref[...] = alpha * acc_ref[...] + pv(pr.astype(BF16))
        m_ref[...] = m_new

    k_t = jnp.concatenate([r[0, 0] for r in k_refs], axis=1).astype(BF16)
    v_t = jnp.concatenate([r[0, 0] for r in v_refs], axis=1).astype(BF16)
    step(kp_ref[0][0:T], p * (n_pages * PAGE_SIZE), _dot(q_ref[0], k_t), lambda pr: _dot_nt(pr, v_t))

    @pl.when(p == pl.num_programs(1) - 1)
    def _():
        step(kn_ref[0][0:T], past, _dot_nt(q_ref[0], knew_ref[0]), lambda pr: _dot(pr, vnew_ref[0]))
        o_ref[0] = acc_ref[...] / l_ref[...]


def _pages_per_step(npages, want):
    while npages % want:
        want //= 2
    return want


def dsa_sample(q, k, v, iq, iw, ik, ck_t, cv_t, cik_t, layer, page_table):
    b, T, _ = q.shape
    assert T <= TQ8
    npages = page_table.shape[1]
    past = npages * PAGE_SIZE
    k_sel = max(1, min(TOPK_MAX, (past + T) // 4))
    R = T * ATT_HEADS
    R8 = TQ8 * IDX_HEADS
    pad_to = lambda a, n: jnp.pad(a, ((0, 0), (0, n - a.shape[1]), (0, 0)))
    iq_rows = pad_to(iq.reshape(b, T * IDX_HEADS, IDX_DIM), R8).astype(BF16)
    iww = jnp.broadcast_to(pad_to(iw.reshape(b, T * IDX_HEADS, 1), R8), (b, R8, LANES)).astype(F32)
    ik_new, k_new, v_new = (pad_to(a, LANES).astype(BF16) for a in (ik, k, v))
    head_group = jnp.arange(ATT_HEADS) // (ATT_HEADS // ATT_KV_HEADS)
    onehot = (head_group[:, None] == jnp.arange(ATT_KV_HEADS)[None, :]).astype(F32)
    q_bd = (q.reshape(b, T, ATT_HEADS, 1, ATT_HEAD_DIM) * onehot[None, None, :, :, None])
    q_bd = q_bd.reshape(b, R, KV_WIDTH).astype(BF16)

    def page_spec(rows, per_step, j):
        return pl.BlockSpec((1, 1, rows, PAGE_SIZE),
                            lambda bi, p, pt: (layer, pt[bi, p * per_step + j], 0, 0))

    fixed = lambda shape: pl.BlockSpec((1,) + shape, lambda bi, p, pt: (bi, 0, 0))

    ps = _pages_per_step(npages, 32)
    keys_past = pl.pallas_call(
        functools.partial(_dsa_sample_scores_kernel, n_pages=ps),
        grid_spec=pltpu.PrefetchScalarGridSpec(
            num_scalar_prefetch=1, grid=(b, npages // ps),
            in_specs=[page_spec(IDX_DIM, ps, j) for j in range(ps)]
            + [fixed((R8, IDX_DIM)), fixed((R8, LANES))],
            out_specs=pl.BlockSpec((1, TQ8, ps * PAGE_SIZE), lambda bi, p, pt: (bi, 0, p))),
        out_shape=jax.ShapeDtypeStruct((b, TQ8, past), I32),
        compiler_params=_cparams("parallel", "arbitrary"),
        name="dsa_sample_scores",
    )(page_table, *([cik_t] * ps), iq_rows, iww)

    nb_seq = SUBLANES if b % SUBLANES == 0 else 1
    seq_spec = lambda r, w: pl.BlockSpec((nb_seq, r, w), lambda bi: (bi, 0, 0))
    thr, jb, keys_new = pl.pallas_call(
        functools.partial(_dsa_sample_thr_kernel, nb_seq=nb_seq, T=T, past=past, k_sel=k_sel),
        grid=(b // nb_seq,),
        in_specs=[seq_spec(TQ8, past), seq_spec(R8, IDX_DIM), seq_spec(R8, LANES),
                  seq_spec(LANES, IDX_DIM)],
        out_specs=[seq_spec(TQ8, LANES)] * 3,
        out_shape=[jax.ShapeDtypeStruct((b, TQ8, LANES), I32)] * 3,
        compiler_params=_cparams("parallel"),
        name="dsa_sample_threshold",
    )(keys_past, iq_rows, iww, ik_new)

    pa = _pages_per_step(npages, 16)
    o = pl.pallas_call(
        functools.partial(_dsa_sample_attn_kernel, n_pages=pa, T=T, past=past),
        grid_spec=pltpu.PrefetchScalarGridSpec(
            num_scalar_prefetch=1, grid=(b, npages // pa),
            in_specs=[page_spec(KV_WIDTH, pa, j) for j in range(pa)] * 2
            + [fixed((R, KV_WIDTH)),
               pl.BlockSpec((1, TQ8, pa * PAGE_SIZE), lambda bi, p, pt: (bi, 0, p)),
               fixed((TQ8, LANES)), fixed((TQ8, LANES)), fixed((TQ8, LANES)),
               fixed((LANES, KV_WIDTH)), fixed((LANES, KV_WIDTH))],
            out_specs=fixed((R, KV_WIDTH)),
            scratch_shapes=[pltpu.VMEM((R, 1), F32), pltpu.VMEM((R, 1), F32),
                            pltpu.VMEM((R, KV_WIDTH), F32)]),
        out_shape=jax.ShapeDtypeStruct((b, R, KV_WIDTH), F32),
        compiler_params=_cparams("parallel", "arbitrary"),
        name="dsa_sample_attention",
    )(page_table, *([ck_t] * pa), *([cv_t] * pa), q_bd, keys_past, thr, jb, keys_new, k_new, v_new)

    o = o.reshape(b, T, ATT_HEADS, ATT_KV_HEADS, ATT_HEAD_DIM)
    o = jnp.sum(o * onehot[None, None, :, :, None], axis=3)
    return o.reshape(b, T, ATT_HEADS * ATT_HEAD_DIM)


def _merge_kernel(ys_ref, ya_ref, yr_ref, g_ref, h_ref, wb_ref, wo_ref, o_ref):
    acc = None
    for n, y_ref in enumerate((ys_ref, ya_ref, yr_ref)):
        pr = _dot(y_ref[...].astype(BF16), wb_ref[n])
        t = jax.nn.sigmoid(g_ref[:, n * D_MODEL:(n + 1) * D_MODEL]) * pr
        acc = t if acc is None else acc + t
    o_ref[...] = h_ref[...] + _dot(acc.astype(BF16), wo_ref[...])


def merge_branches(ys, ya, yr, gates, h, wb, wo):
    M = h.shape[0]
    tm = _pick(M, (256, 128))
    rows = lambda w: pl.BlockSpec((tm, w), lambda i: (i, 0))
    return pl.pallas_call(
        _merge_kernel,
        grid=(M // tm,),
        in_specs=[rows(D_MODEL), rows(D_MODEL), rows(D_MODEL), rows(N_BRANCH * D_MODEL), rows(D_MODEL),
                  pl.BlockSpec((N_BRANCH, D_MODEL, D_MODEL), lambda i: (0, 0, 0)),
                  pl.BlockSpec((D_MODEL, D_MODEL), lambda i: (0, 0))],
        out_specs=rows(D_MODEL),
        out_shape=jax.ShapeDtypeStruct((M, D_MODEL), F32),
        compiler_params=_cparams("parallel"),
        name="merge_branches",
    )(ys, ya, yr, gates, h, wb, wo)


def _ffn_kernel(x_ref, g_ref, wg_ref, wu_ref, wd_ref, o_ref, xn_ref, acc_ref):
    j = pl.program_id(1)

    @pl.when(j == 0)
    def _():
        x = x_ref[...]
        ms = jnp.mean(x * x, axis=-1, keepdims=True)
        xn_ref[...] = (x * lax.rsqrt(ms + EPS) * g_ref[...]).astype(BF16)
        acc_ref[...] = jnp.zeros_like(acc_ref)

    xn = xn_ref[...]
    a = _silu(_dot(xn, wg_ref[...])) * _dot(xn, wu_ref[...])
    acc_ref[...] += _dot(a.astype(BF16), wd_ref[...])

    @pl.when(j == pl.num_programs(1) - 1)
    def _():
        o_ref[...] = x_ref[...] + acc_ref[...]


def ffn_dense(x, g, wg, wu, wd):
    M = x.shape[0]
    tm = _pick(M, (1024, 512, 256, 128))
    tf = 512
    return pl.pallas_call(
        _ffn_kernel,
        grid=(M // tm, D_FF // tf),
        in_specs=[pl.BlockSpec((tm, D_MODEL), lambda i, j: (i, 0)),
                  pl.BlockSpec((1, D_MODEL), lambda i, j: (0, 0)),
                  pl.BlockSpec((D_MODEL, tf), lambda i, j: (0, j)),
                  pl.BlockSpec((D_MODEL, tf), lambda i, j: (0, j)),
                  pl.BlockSpec((tf, D_MODEL), lambda i, j: (j, 0))],
        out_specs=pl.BlockSpec((tm, D_MODEL), lambda i, j: (i, 0)),
        out_shape=jax.ShapeDtypeStruct((M, D_MODEL), F32),
        scratch_shapes=[pltpu.VMEM((tm, D_MODEL), BF16), pltpu.VMEM((tm, D_MODEL), F32)],
        compiler_params=_cparams("parallel", "arbitrary"),
        name="ffn_dense",
    )(x, g.reshape(1, D_MODEL), wg, wu, wd)


def _router_kernel(x_ref, g_ref, wr_ref, gate_ref):
    x = x_ref[...]
    ms = jnp.mean(x * x, axis=-1, keepdims=True)
    xn = x * lax.rsqrt(ms + EPS) * g_ref[...]
    logits = jnp.dot(xn, wr_ref[...], preferred_element_type=F32, precision=lax.Precision.HIGHEST)
    lane = lax.broadcasted_iota(I32, logits.shape, 1)
    logits = jnp.where(lane < N_EXPERTS, logits, -jnp.inf)
    v1 = jnp.max(logits, axis=1, keepdims=True)
    i1 = jnp.min(jnp.where(logits == v1, lane, LANES), axis=1, keepdims=True)
    rest = jnp.where(lane == i1, -jnp.inf, logits)
    v2 = jnp.max(rest, axis=1, keepdims=True)
    i2 = jnp.min(jnp.where(rest == v2, lane, LANES), axis=1, keepdims=True)
    e2 = jnp.exp(v2 - v1)
    w1 = 1.0 / (1.0 + e2)
    w2 = e2 / (1.0 + e2)
    gate_ref[...] = jnp.where(lane == i1, w1, 0.0) + jnp.where(lane == i2, w2, 0.0)


def moe_router(x, g, wr):
    M = x.shape[0]
    tm = _pick(M, (512, 256, 128))
    wr_pad = jnp.pad(wr.astype(F32), ((0, 0), (0, LANES - N_EXPERTS)))
    return pl.pallas_call(
        _router_kernel,
        grid=(M // tm,),
        in_specs=[pl.BlockSpec((tm, D_MODEL), lambda i: (i, 0)),
                  pl.BlockSpec((1, D_MODEL), lambda i: (0, 0)),
                  pl.BlockSpec((D_MODEL, LANES), lambda i: (0, 0))],
        out_specs=pl.BlockSpec((tm, LANES), lambda i: (i, 0)),
        out_shape=jax.ShapeDtypeStruct((M, LANES), F32),
        compiler_params=_cparams("parallel"),
        name="moe_router",
    )(x, g.reshape(1, D_MODEL), wr_pad)


MOE_ROWS = 320


def _moe_kernel(x_ref, g_ref, gate_ref, gatet_ref, wg_ref, wu_ref, wd_ref, o_ref,
                xn_ref, rkc_ref, rkr_ref, xe_ref, ye_ref, cnt_ref, *, tm, R, nch):
    e = pl.program_id(1)
    j = pl.program_id(2)
    last_j = pl.num_programs(2) - 1

    @pl.when((e == 0) & (j == 0))
    def _():
        x = x_ref[...]
        ms = jnp.mean(x * x, axis=-1, keepdims=True)
        xn_ref[...] = (x * lax.rsqrt(ms + EPS) * g_ref[...]).astype(BF16)
        o_ref[...] = x
        ti = lax.broadcasted_iota(I32, (tm, tm), 0)
        tj = lax.broadcasted_iota(I32, (tm, tm), 1)
        flags = jnp.where(gate_ref[...] > 0.0, 1.0, 0.0).astype(BF16)
        flags_t = jnp.where(gatet_ref[...] > 0.0, 1.0, 0.0).astype(BF16)
        rkc_ref[...] = _dot(jnp.where(ti > tj, 1.0, 0.0).astype(BF16), flags)
        rkr_ref[...] = _dot(flags_t, jnp.where(ti < tj, 1.0, 0.0).astype(BF16))

    @pl.when(j == 0)
    def _():
        rank_row = rkr_ref[pl.ds(e, 1), :]
        flag_row = gatet_ref[pl.ds(e, 1), :] > 0.0
        cnt = jnp.sum(jnp.where(flag_row, 1, 0))
        cnt_ref[0] = cnt
        ye_ref[...] = jnp.zeros_like(ye_ref)
        for c in range(nch):
            @pl.when(c * R < cnt)
            def _():
                slot = (c * R + lax.broadcasted_iota(I32, (R, 1), 0)).astype(F32)
                sel = jnp.where((rank_row == slot) & flag_row, 1.0, 0.0).astype(BF16)
                xe_ref[c * R:(c + 1) * R, :] = _dot(sel, xn_ref[...]).astype(BF16)

    cnt = cnt_ref[0]
    for c in range(nch):
        @pl.when(c * R < cnt)
        def _():
            xe = xe_ref[c * R:(c + 1) * R, :]
            a = _silu(_dot(xe, wg_ref[0])) * _dot(xe, wu_ref[0])
            ye_ref[c * R:(c + 1) * R, :] += _dot(a.astype(BF16), wd_ref[0])

    @pl.when(j == last_j)
    def _():
        lane = lax.broadcasted_iota(I32, (tm, LANES), 1)
        gcol = jnp.sum(jnp.where(lane == e, gate_ref[...], 0.0), axis=1, keepdims=True)
        rcol = jnp.sum(jnp.where(lane == e, rkc_ref[...], 0.0), axis=1, keepdims=True)
        for c in range(nch):
            @pl.when(c * R < cnt)
            def _():
                slot = (c * R + lax.broadcasted_iota(I32, (1, R), 1)).astype(F32)
                sel_t = jnp.where((rcol == slot) & (gcol > 0.0), 1.0, 0.0).astype(BF16)
                ye = ye_ref[c * R:(c + 1) * R, :]
                hi = ye.astype(BF16)
                lo = (ye - hi.astype(F32)).astype(BF16)
                o_ref[...] += gcol * (_dot(sel_t, hi) + _dot(sel_t, lo))


def moe_ffn(x, g, gate, wg, wu, wd):
    M = x.shape[0]
    tm = _pick(M, (1024, 512, 256, 128))
    tf = 896
    R = min(MOE_ROWS, tm)
    nch = -(-tm // R)
    kern = functools.partial(_moe_kernel, tm=tm, R=R, nch=nch)
    return pl.pallas_call(
        kern,
        grid=(M // tm, N_EXPERTS, D_FF // tf),
        in_specs=[pl.BlockSpec((tm, D_MODEL), lambda i, e, j: (i, 0)),
                  pl.BlockSpec((1, D_MODEL), lambda i, e, j: (0, 0)),
                  pl.BlockSpec((tm, LANES), lambda i, e, j: (i, 0)),
                  pl.BlockSpec((LANES, tm), lambda i, e, j: (0, i)),
                  pl.BlockSpec((1, D_MODEL, tf), lambda i, e, j: (e, 0, j)),
                  pl.BlockSpec((1, D_MODEL, tf), lambda i, e, j: (e, 0, j)),
                  pl.BlockSpec((1, tf, D_MODEL), lambda i, e, j: (e, j, 0))],
        out_specs=pl.BlockSpec((tm, D_MODEL), lambda i, e, j: (i, 0)),
        out_shape=jax.ShapeDtypeStruct((M, D_MODEL), F32),
        scratch_shapes=[pltpu.VMEM((tm, D_MODEL), BF16),
                        pltpu.VMEM((tm, LANES), F32), pltpu.VMEM((LANES, tm), F32),
                        pltpu.VMEM((nch * R, D_MODEL), BF16), pltpu.VMEM((nch * R, D_MODEL), F32),
                        pltpu.SMEM((1,), I32)],
        compiler_params=pltpu.CompilerParams(
            dimension_semantics=("parallel", "arbitrary", "arbitrary"), vmem_limit_bytes=MOE_VMEM_LIMIT),
        name="moe_ffn",
    )(x, g.reshape(1, D_MODEL), gate, gate.T, wg, wu, wd)


def _rmsnorm_kernel(x_ref, g_ref, o_ref):
    x = x_ref[...]
    ms = jnp.mean(x * x, axis=-1, keepdims=True)
    o_ref[...] = x * lax.rsqrt(ms + EPS) * g_ref[...]


def rmsnorm(x, g):
    M = x.shape[0]
    tm = _pick(M, (1024, 512, 256, 128))
    return pl.pallas_call(
        _rmsnorm_kernel,
        grid=(M // tm,),
        in_specs=[pl.BlockSpec((tm, D_MODEL), lambda i: (i, 0)),
                  pl.BlockSpec((1, D_MODEL), lambda i: (0, 0))],
        out_specs=pl.BlockSpec((tm, D_MODEL), lambda i: (i, 0)),
        out_shape=jax.ShapeDtypeStruct((M, D_MODEL), F32),
        compiler_params=_cparams("parallel"),
        name="final_rmsnorm",
    )(x, g.reshape(1, D_MODEL))


def _split_w_in(w_in):
    offs = np.cumsum((0,) + IN_SPLITS)
    seg = {n: w_in[:, offs[i]:offs[i + 1]] for i, n in enumerate(
        ("z", "xbc", "dt", "q", "k", "v", "iq", "iw", "ik", "rq", "rk", "rv", "rg", "gates"))}
    cat = lambda *names: jnp.concatenate([seg[n] if isinstance(n, str) else n for n in names], axis=1)
    dt_wide = jnp.repeat(seg["dt"], SSM_HEAD_DIM, axis=1)
    small = cat("dt", "iw", jnp.zeros((D_MODEL, LANES - 2 * IDX_HEADS - IDX_DIM), w_in.dtype), "ik")
    q_scaled = seg["q"] * (ATT_HEAD_DIM ** -0.5)
    groups = dict(ssm=cat("z", "xbc", dt_wide), att=cat(q_scaled, "k", "v"), idx=cat("iq", small),
                  ret=cat("rq", "rk", "rv", "rg"), gate=seg["gates"])
    return {n: w.astype(BF16) for n, w in groups.items()}


def _rope_tables(pos):
    half = RET_DK // 2
    inv = ROPE_BASE ** (-jnp.arange(half, dtype=F32) / half)
    ang = pos.astype(F32)[:, None] * inv[None, :]
    cos, sin = jnp.cos(ang), jnp.sin(ang)
    return jnp.concatenate([cos, cos], axis=1), jnp.concatenate([-sin, sin], axis=1)


def _pad_rows(a, L):
    return jnp.pad(a, ((0, 0), (0, L - a.shape[1]), (0, 0)))


def _mixers(h, pos0, conv0, ssm0, ret0, attend, lw):
    b, L, _ = h.shape
    hf = h.reshape(b * L, D_MODEL)
    proj = {n: rms_matmul(hf, lw["norm_g"], w).reshape(b, L, -1) for n, w in lw["w_in"].items()}
    att, idx = proj["att"], proj["idx"]
    q = att[..., :1024]
    k = att[..., 1024:1280]
    v = att[..., 1280:1536]
    iq = idx[..., :1024]
    small = idx[..., 1024:]
    ik = small[..., LANES - IDX_DIM:]

    Lp = -(-L // CHUNK) * CHUNK
    last_valid = L - (Lp - CHUNK)
    dtT = jnp.swapaxes(_pad_rows(small[..., :SSM_HEADS], Lp), 1, 2)
    s0 = jnp.transpose(ssm0, (0, 2, 1, 3)).reshape(b, SSM_STATE, SSM_INNER)
    ys, conv_new, s_new = ssd_branch(_pad_rows(proj["ssm"], Lp), dtT, conv0, s0, lw["conv_w"], lw["conv_b"],
                                     lw["dt_bias"], lw["a_log"], lw["d_skip"], lw["ssm_norm_g"], last_valid)
    ssm_new = jnp.transpose(s_new.reshape(b, SSM_STATE, SSM_HEADS, SSM_HEAD_DIM), (0, 2, 1, 3))

    cos2, sin2 = _rope_tables(pos0 + jnp.arange(Lp))
    yr, ret_new = ret_branch(_pad_rows(proj["ret"], Lp), cos2, sin2, ret0, last_valid)

    ya = attend(q, k, v, iq, small, ik)

    ys = ys[:, :L].reshape(b * L, -1)
    yr = yr[:, :L].reshape(b * L, -1)
    out = merge_branches(ys, ya.reshape(b * L, -1), yr, proj["gate"].reshape(b * L, -1), hf,
                         lw["w_branch"], lw["w_out"])
    return out.reshape(b, L, D_MODEL), (k.reshape(b, L, ATT_KV_HEADS, ATT_HEAD_DIM),
                                        v.reshape(b, L, ATT_KV_HEADS, ATT_HEAD_DIM),
                                        ik, ssm_new, conv_new, ret_new)


def _attend_prompt(q, k, v, iq, small, ik):
    b, L, _ = q.shape
    hm = lambda a, nh: jnp.transpose(a.reshape(b, L, nh, -1), (0, 2, 1, 3)).astype(BF16)
    v_t = jnp.transpose(v.reshape(b, L, ATT_KV_HEADS, ATT_HEAD_DIM), (0, 2, 3, 1)).astype(BF16)
    v_t = jnp.concatenate([v_t, jnp.ones((b, ATT_KV_HEADS, V_ROWS - ATT_HEAD_DIM, L), BF16)], axis=2)
    iw_t = jnp.swapaxes(small[..., IDX_HEADS:2 * IDX_HEADS], 1, 2)
    o = dsa_prompt(hm(q, ATT_HEADS), hm(iq, IDX_HEADS), iw_t, ik.astype(BF16), hm(k, ATT_KV_HEADS), v_t)
    return jnp.transpose(o, (0, 3, 1, 2)).reshape(b, L, ATT_HEADS * ATT_HEAD_DIM)


def _attend_sample(q, k, v, iq, small, ik, *, ck_t, cv_t, cik_t, layer, page_table):
    iw = small[..., IDX_HEADS:2 * IDX_HEADS]
    return dsa_sample(q, k, v, iq, iw, ik, ck_t, cv_t, cik_t, layer, page_table)


def _channel_mixer(h, l, p):
    b, L, _ = h.shape
    hf = h.reshape(b * L, D_MODEL)
    j = l // 2
    if l % 2 == 0:
        out = ffn_dense(hf, p["norm_ffn_g"][l], p["w_ffn_gate"][j].astype(BF16),
                        p["w_ffn_up"][j].astype(BF16), p["w_ffn_down"][j].astype(BF16))
    else:
        gate = moe_router(hf, p["norm_ffn_g"][l], p["w_router"][j])
        out = moe_ffn(hf, p["norm_ffn_g"][l], gate, p["w_moe_gate"][j].astype(BF16),
                      p["w_moe_up"][j].astype(BF16), p["w_moe_down"][j].astype(BF16))
    return out.reshape(b, L, D_MODEL)


def kernel(x_prompt, x_sample, cache_k, cache_v, cache_idx_k, state_ssm, state_conv, state_ret,
           page_table, norm_mix_g, w_in, conv_w, conv_b, dt_bias, a_log, d_skip, ssm_norm_g,
           w_branch, w_out, norm_ffn_g, w_ffn_gate, w_ffn_up, w_ffn_down, w_router,
           w_moe_gate, w_moe_up, w_moe_down, final_norm_g):
    depth = w_in.shape[0]
    past = page_table.shape[1] * PAGE_SIZE
    bp = x_prompt.shape[0]
    p = dict(norm_ffn_g=norm_ffn_g, w_ffn_gate=w_ffn_gate, w_ffn_up=w_ffn_up, w_ffn_down=w_ffn_down,
             w_router=w_router, w_moe_gate=w_moe_gate, w_moe_up=w_moe_up, w_moe_down=w_moe_down)
    n_phys = cache_k.shape[1]
    ck_t = jnp.transpose(cache_k, (0, 1, 3, 4, 2)).reshape(depth, n_phys, KV_WIDTH, PAGE_SIZE)
    cv_t = jnp.transpose(cache_v, (0, 1, 3, 4, 2)).reshape(depth, n_phys, KV_WIDTH, PAGE_SIZE)
    cik_t = jnp.transpose(cache_idx_k, (0, 1, 3, 2))
    hp, hs = x_prompt, x_sample
    st_p, st_s = [], []
    for l in range(depth):
        lw = dict(norm_g=norm_mix_g[l], w_in=_split_w_in(w_in[l]), conv_w=conv_w[l], conv_b=conv_b[l],
                  dt_bias=dt_bias[l], a_log=a_log[l], d_skip=d_skip[l], ssm_norm_g=ssm_norm_g[l],
                  w_branch=w_branch[l].astype(BF16), w_out=w_out[l].astype(BF16))
        hp, sp = _mixers(hp, 0,
                         jnp.zeros((bp, CONV_WIDTH - 1, CONV_DIM), F32),
                         jnp.zeros((bp, SSM_HEADS, SSM_STATE, SSM_HEAD_DIM), F32),
                         jnp.zeros((bp, RET_HEADS, RET_DK, RET_DV), F32),
                         _attend_prompt, lw)
        attend_s = functools.partial(_attend_sample, ck_t=ck_t, cv_t=cv_t, cik_t=cik_t, layer=l,
                                     page_table=page_table)
        hs, ss = _mixers(hs, past, state_conv[l], state_ssm[l], state_ret[l], attend_s, lw)
        hp = _channel_mixer(hp, l, p)
        hs = _channel_mixer(hs, l, p)
        st_p.append(sp)
        st_s.append(ss)
    y_prompt = rmsnorm(hp.reshape(-1, D_MODEL), final_norm_g).reshape(hp.shape)
    y_sample = rmsnorm(hs.reshape(-1, D_MODEL), final_norm_g).reshape(hs.shape)
    stack = lambda sts, i: jnp.stack([s[i] for s in sts])
    return (y_prompt, y_sample,
            stack(st_p, 0), stack(st_p, 1), stack(st_p, 2), stack(st_p, 3), stack(st_p, 4), stack(st_p, 5),
            stack(st_s, 0), stack(st_s, 1), stack(st_s, 2), stack(st_s, 3), stack(st_s, 4), stack(st_s, 5))
```

```python
import functools
import math

import jax
import jax.numpy as jnp
import numpy as np
from jax import lax
from jax.experimental import pallas as pl
from jax.experimental.pallas import tpu as pltpu

F32 = jnp.float32
BF16 = jnp.bfloat16
I32 = jnp.int32

D_MODEL = 1024
PAGE_SIZE = 128
SSM_HEADS = 16
SSM_HEAD_DIM = 64
SSM_INNER = 1024
SSM_GROUPS = 2
SSM_STATE = 128
CONV_WIDTH = 4
CONV_DIM = 1536
ATT_HEADS = 16
ATT_KV_HEADS = 4
ATT_HEAD_DIM = 64
KV_WIDTH = 256
IDX_HEADS = 16
IDX_DIM = 64
TOPK_MAX = 256
RET_HEADS = 4
RET_DK = 128
RET_DV = 256
ROPE_BASE = 10000.0
N_BRANCH = 3
D_FF = 3584
N_EXPERTS = 8
EPS = 1e-6
IN_SPLITS = (1024, 1536, 16, 1024, 256, 256, 1024, 16, 64, 512, 512, 1024, 1024, 3072)

LANES = 128
SUBLANES = 8
VMEM_LIMIT = 48 * 1024 * 1024
MOE_VMEM_LIMIT = 56 * 1024 * 1024

CHUNK = 128
INT_MIN = -(2 ** 31)
KEY_POS_INF = 0x7F800000
KEY_NEG_INF = -2139095041
NEG_BIG = -1e30


def _cparams(*sem):
    return pltpu.CompilerParams(dimension_semantics=sem, vmem_limit_bytes=VMEM_LIMIT)


def _dot(a, b):
    return jnp.dot(a, b, preferred_element_type=F32)


def _dot_nt(a, b):
    return lax.dot_general(a, b, (((1,), (1,)), ((), ())), preferred_element_type=F32)


def _dot_tn(a, b):
    return lax.dot_general(a, b, (((0,), (0,)), ((), ())), preferred_element_type=F32)


def _split3(x):
    hi = x.astype(BF16)
    r = x - hi.astype(F32)
    mid = r.astype(BF16)
    lo = (r - mid.astype(F32)).astype(BF16)
    return hi, mid, lo


def _silu(x):
    return x * jax.nn.sigmoid(x)


def _softplus(x):
    return jnp.maximum(x, 0.0) + jnp.log1p(jnp.exp(-jnp.abs(x)))


def _float_key(x):
    x = jnp.where(x == 0.0, 0.0, x)
    b = lax.bitcast_convert_type(x, I32)
    return jnp.where(b >= 0, b, b ^ jnp.int32(0x7FFFFFFF))


def _rms_matmul_kernel(x_ref, g_ref, w_ref, o_ref):
    x = x_ref[...]
    ms = jnp.mean(x * x, axis=-1, keepdims=True)
    xn = (x * lax.rsqrt(ms + EPS) * g_ref[...]).astype(BF16)
    o_ref[...] = _dot(xn, w_ref[...]).astype(o_ref.dtype)


def _rms_matmul_heads_kernel(x_ref, g_ref, w_ref, o_ref, *, n_heads, head_dim):
    x = x_ref[...]
    ms = jnp.mean(x * x, axis=-1, keepdims=True)
    xn = (x * lax.rsqrt(ms + EPS) * g_ref[...]).astype(BF16)
    res = _dot(xn, w_ref[...])
    for h in range(n_heads):
        o_ref[h] = res[:, h * head_dim:(h + 1) * head_dim].astype(o_ref.dtype)


def _pick(n, prefs):
    for p in prefs:
        if n % p == 0:
            return p
    return n


def rms_matmul(x, g, w, out_dtype=F32):
    M, K = x.shape
    N = w.shape[1]
    tm = _pick(M, (512, 256, 128))
    return pl.pallas_call(
        _rms_matmul_kernel,
        grid=(M // tm,),
        in_specs=[pl.BlockSpec((tm, K), lambda i: (i, 0)),
                  pl.BlockSpec((1, K), lambda i: (0, 0)),
                  pl.BlockSpec((K, N), lambda i: (0, 0))],
        out_specs=pl.BlockSpec((tm, N), lambda i: (i, 0)),
        out_shape=jax.ShapeDtypeStruct((M, N), out_dtype),
        compiler_params=_cparams("parallel"),
        name="rms_matmul",
    )(x, g.reshape(1, K), w)


def rms_matmul_heads(x, g, w, head_dim):
    M, K = x.shape
    N = w.shape[1]
    n_heads = N // head_dim
    tm = _pick(M, (512, 256, 128))
    kern = functools.partial(_rms_matmul_heads_kernel, n_heads=n_heads, head_dim=head_dim)
    return pl.pallas_call(
        kern,
        grid=(M // tm,),
        in_specs=[pl.BlockSpec((tm, K), lambda i: (i, 0)),
                  pl.BlockSpec((1, K), lambda i: (0, 0)),
                  pl.BlockSpec((K, N), lambda i: (0, 0))],
        out_specs=pl.BlockSpec((n_heads, tm, head_dim), lambda i: (0, i, 0)),
        out_shape=jax.ShapeDtypeStruct((n_heads, M, head_dim), BF16),
        compiler_params=_cparams("parallel"),
        name="rms_matmul_heads",
    )(x, g.reshape(1, K), w)


def _ssd_kernel(zxd_ref, dtT_ref, conv0_ref, s0_ref, cw_ref, cb_ref, dtb_ref, alog_ref, dsk_ref,
                ng_ref, dtbT_ref, alogT_ref, y_ref, convn_ref, sn_ref, xpad_ref, st_ref,
                *, Q, last_valid, nc):
    c = pl.program_id(1)
    GW = SSM_INNER // SSM_GROUPS

    @pl.when(c == 0)
    def _():
        xpad_ref[0:8, :] = jnp.zeros((8, CONV_DIM), F32)
        xpad_ref[5:8, :] = conv0_ref[0]
        st_ref[...] = s0_ref[0]

    blk = zxd_ref[0]
    z = blk[:, :SSM_INNER]
    xbc = blk[:, SSM_INNER:SSM_INNER + CONV_DIM]
    dtr = blk[:, SSM_INNER + CONV_DIM:]
    xpad_ref[8:8 + Q, :] = xbc
    cw = cw_ref[...]
    conv = (xpad_ref[5:5 + Q, :] * cw[0:1] + xpad_ref[6:6 + Q, :] * cw[1:2]
            + xpad_ref[7:7 + Q, :] * cw[2:3] + xbc * cw[3:4]) + cb_ref[...]
    xc = _silu(conv)
    xs = xc[:, :SSM_INNER]
    Bm = xc[:, SSM_INNER:SSM_INNER + SSM_GROUPS * SSM_STATE]
    Cm = xc[:, SSM_INNER + SSM_GROUPS * SSM_STATE:]

    row = lax.broadcasted_iota(I32, (Q, 1), 0)
    colq = lax.broadcasted_iota(I32, (1, Q), 1)
    assert last_valid == Q or nc == 1
    lv = last_valid
    valid = row < lv
    tril = (lax.broadcasted_iota(I32, (Q, Q), 0) >= lax.broadcasted_iota(I32, (Q, Q), 1))
    tril_b = jnp.where(tril, 1.0, 0.0).astype(BF16)
    triu_b = jnp.where(lax.broadcasted_iota(I32, (Q, Q), 0) <= lax.broadcasted_iota(I32, (Q, Q), 1),
                       1.0, 0.0).astype(BF16)

    nega = -jnp.exp(alog_ref[...])
    dt = _softplus(dtr + dtb_ref[...])
    la = jnp.where(valid, dt * nega, 0.0)
    cum = sum(_dot(tril_b, p) for p in _split3(la))
    laT = jnp.where(colq < lv, _softplus(dtT_ref[0] + dtbT_ref[...]) * (-jnp.exp(alogT_ref[...])), 0.0)
    cumT = sum(_dot(p, triu_b) for p in _split3(laT))

    ecum = jnp.exp(cum)
    cl = cum[lv - 1:lv, :]
    xdt = xs * dt
    xtail = jnp.where(valid, xdt * jnp.exp(cl - cum), 0.0)
    cdecay = jnp.exp(cl)

    lane = lax.broadcasted_iota(I32, (1, LANES), 1)
    y_groups = []
    for g in range(SSM_GROUPS):
        l0 = g * GW
        Cg = Cm[:, g * SSM_STATE:(g + 1) * SSM_STATE].astype(BF16)
        Bg = Bm[:, g * SSM_STATE:(g + 1) * SSM_STATE].astype(BF16)
        G = _dot_nt(Cg, Bg)
        st_g = st_ref[:, l0:l0 + GW]
        inter = _dot(Cg, st_g.astype(BF16)) * ecum[:, l0:l0 + GW]
        local = _dot_tn(Bg, xtail[:, l0:l0 + GW].astype(BF16))
        st_ref[:, l0:l0 + GW] = st_g * cdecay[:, l0:l0 + GW] + local
        pairs = []
        for p in range(GW // LANES):
            xp = xdt[:, l0 + p * LANES:l0 + (p + 1) * LANES]
            acc = None
            for hh in range(2):
                h = (l0 + p * LANES) // SSM_HEAD_DIM + hh
                ccol = cum[:, h * SSM_HEAD_DIM:h * SSM_HEAD_DIM + 1]
                diff = ccol - cumT[h:h + 1, :]
                dm = jnp.exp(jnp.where(tril, diff, -jnp.inf))
                s = (G * dm).astype(BF16)
                half = (lane >= hh * SSM_HEAD_DIM) & (lane < (hh + 1) * SSM_HEAD_DIM)
                part = _dot(s, jnp.where(half, xp, 0.0).astype(BF16))
                acc = part if acc is None else acc + part
            pairs.append(acc)
        y_groups.append(jnp.concatenate(pairs, axis=1) + inter)
    y = jnp.concatenate(y_groups, axis=1)

    y = (y + dsk_ref[...] * xs) * _silu(z)
    outs = []
    for g in range(SSM_GROUPS):
        seg = y[:, g * GW:(g + 1) * GW]
        ms = jnp.mean(seg * seg, axis=-1, keepdims=True)
        outs.append(seg * lax.rsqrt(ms + EPS) * ng_ref[:, g * GW:(g + 1) * GW])
    y_ref[0] = jnp.concatenate(outs, axis=1)

    @pl.when(c == nc - 1)
    def _():
        convn_ref[0] = xpad_ref[5 + last_valid:8 + last_valid, :]
        sn_ref[0] = st_ref[...]

    xpad_ref[0:8, :] = xpad_ref[Q:Q + 8, :]


def ssd_branch(zxd, dtT, conv0, s0, conv_w, conv_b, dt_bias, a_log, d_skip, norm_g, last_valid):
    b, L, W = zxd.shape
    Q = CHUNK
    nc = L // Q
    rep = lambda v: jnp.repeat(v.astype(F32), SSM_HEAD_DIM).reshape(1, SSM_INNER)
    col = lambda v: v.astype(F32).reshape(SSM_HEADS, 1)
    full = lambda shape: pl.BlockSpec(shape, lambda i, c: (0,) * len(shape))
    kern = functools.partial(_ssd_kernel, Q=Q, last_valid=last_valid, nc=nc)
    return pl.pallas_call(
        kern,
        grid=(b, nc),
        in_specs=[pl.BlockSpec((1, Q, W), lambda i, c: (i, c, 0)),
                  pl.BlockSpec((1, SSM_HEADS, Q), lambda i, c: (i, 0, c)),
                  pl.BlockSpec((1, CONV_WIDTH - 1, CONV_DIM), lambda i, c: (i, 0, 0)),
                  pl.BlockSpec((1, SSM_STATE, SSM_INNER), lambda i, c: (i, 0, 0)),
                  full((CONV_WIDTH, CONV_DIM)), full((1, CONV_DIM)),
                  full((1, SSM_INNER)), full((1, SSM_INNER)), full((1, SSM_INNER)), full((1, SSM_INNER)),
                  full((SSM_HEADS, 1)), full((SSM_HEADS, 1))],
        out_specs=[pl.BlockSpec((1, Q, SSM_INNER), lambda i, c: (i, c, 0)),
                   pl.BlockSpec((1, CONV_WIDTH - 1, CONV_DIM), lambda i, c: (i, 0, 0)),
                   pl.BlockSpec((1, SSM_STATE, SSM_INNER), lambda i, c: (i, 0, 0))],
        out_shape=[jax.ShapeDtypeStruct((b, L, SSM_INNER), F32),
                   jax.ShapeDtypeStruct((b, CONV_WIDTH - 1, CONV_DIM), F32),
                   jax.ShapeDtypeStruct((b, SSM_STATE, SSM_INNER), F32)],
        scratch_shapes=[pltpu.VMEM((Q + 8, CONV_DIM), F32), pltpu.VMEM((SSM_STATE, SSM_INNER), F32)],
        compiler_params=_cparams("parallel", "arbitrary"),
        name="ssd_branch",
    )(zxd, dtT, conv0, s0, conv_w.astype(F32), conv_b.reshape(1, CONV_DIM).astype(F32),
      rep(dt_bias), rep(a_log), rep(d_skip), norm_g.reshape(1, SSM_INNER).astype(F32),
      col(dt_bias), col(a_log))


def _ret_kernel(x_ref, cos_ref, sin_ref, r0_ref, y_ref, rn_ref, st_ref, *, Q, last_valid, nc):
    c = pl.program_id(1)

    @pl.when(c == 0)
    def _():
        st_ref[...] = r0_ref[0]

    blk = x_ref[0]
    cos2 = cos_ref[...]
    sin2 = sin_ref[...]
    ri = lax.broadcasted_iota(I32, (Q, Q), 0)
    ci = lax.broadcasted_iota(I32, (Q, Q), 1)
    dij = (ri - ci).astype(F32)
    row = lax.broadcasted_iota(I32, (Q, 1), 0)
    assert last_valid == Q or nc == 1
    lv = last_valid
    rowf = row.astype(F32)
    lvf = float(lv)
    KO = RET_HEADS * RET_DK
    outs = []
    for h in range(RET_HEADS):
        lg = math.log1p(-2.0 ** (-5.0 - h))
        qh = blk[:, h * RET_DK:(h + 1) * RET_DK]
        kh = blk[:, KO + h * RET_DK:KO + (h + 1) * RET_DK]
        vh = blk[:, 2 * KO + h * RET_DV:2 * KO + (h + 1) * RET_DV].astype(BF16)
        gh = blk[:, 2 * KO + RET_HEADS * RET_DV + h * RET_DV:2 * KO + RET_HEADS * RET_DV + (h + 1) * RET_DV]
        qr = qh * cos2 + pltpu.roll(qh, RET_DK // 2, 1) * sin2
        kr = (kh * cos2 + pltpu.roll(kh, RET_DK // 2, 1) * sin2) * (RET_DK ** -0.5)
        qb = qr.astype(BF16)
        dm = jnp.exp(jnp.where(ri >= ci, dij * lg, -jnp.inf))
        s = (_dot_nt(qb, kr.astype(BF16)) * dm).astype(BF16)
        st_h = st_ref[h]
        y = _dot(s, vh) + jnp.exp((rowf + 1.0) * lg) * _dot(qb, st_h.astype(BF16))
        ktail = jnp.where(row < lv, kr * jnp.exp((lvf - 1.0 - rowf) * lg), 0.0)
        st_ref[h] = st_h * math.exp(lvf * lg) + _dot_tn(ktail.astype(BF16), vh)
        ms = jnp.mean(y * y, axis=-1, keepdims=True)
        outs.append(y * lax.rsqrt(ms + EPS) * _silu(gh))
    y_ref[0] = jnp.concatenate(outs, axis=1)

    @pl.when(c == nc - 1)
    def _():
        rn_ref[0] = st_ref[...]


def ret_branch(x, cos2, sin2, r0, last_valid):
    b, L, W = x.shape
    Q = CHUNK
    nc = L // Q
    kern = functools.partial(_ret_kernel, Q=Q, last_valid=last_valid, nc=nc)
    return pl.pallas_call(
        kern,
        grid=(b, nc),
        in_specs=[pl.BlockSpec((1, Q, W), lambda i, c: (i, c, 0)),
                  pl.BlockSpec((Q, RET_DK), lambda i, c: (c, 0)),
                  pl.BlockSpec((Q, RET_DK), lambda i, c: (c, 0)),
                  pl.BlockSpec((1, RET_HEADS, RET_DK, RET_DV), lambda i, c: (i, 0, 0, 0))],
        out_specs=[pl.BlockSpec((1, Q, RET_HEADS * RET_DV), lambda i, c: (i, c, 0)),
                   pl.BlockSpec((1, RET_HEADS, RET_DK, RET_DV), lambda i, c: (i, 0, 0, 0))],
        out_shape=[jax.ShapeDtypeStruct((b, L, RET_HEADS * RET_DV), F32),
                   jax.ShapeDtypeStruct((b, RET_HEADS, RET_DK, RET_DV), F32)],
        scratch_shapes=[pltpu.VMEM((RET_HEADS, RET_DK, RET_DV), F32)],
        compiler_params=_cparams("parallel", "arbitrary"),
        name="ret_branch",
    )(x, cos2, sin2, r0)


def _kth_largest_key(count_ge, k, shape):
    def body(t, prefix):
        bit = lax.shift_left(jnp.int32(1), jnp.int32(31) - t)
        cand = prefix | bit
        cnt = count_ge(cand ^ jnp.int32(INT_MIN))
        return jnp.where(cnt >= k, cand, prefix)

    prefix = lax.fori_loop(0, 32, body, jnp.zeros(shape, I32))
    return prefix ^ jnp.int32(INT_MIN)


def _dsa_prompt_kernel(q_ref, iq_ref, iwt_ref, ik_ref, k_ref, vt_ref, o_ref, keys_ref, jb_ref,
                       m_ref, acc_ref, *, tq, ck, cka, L, k_sel):
    i = pl.program_id(1)
    G = ATT_HEADS // ATT_KV_HEADS
    PART = 4 * SUBLANES
    nk = lax.div((i + 1) * tq + (ck - 1), ck)
    qpos = i * tq + lax.broadcasted_iota(I32, (1, tq), 1)
    rowk = lax.broadcasted_iota(I32, (ck, 1), 0)
    iw = iwt_ref[0] * (IDX_HEADS ** -0.5 * IDX_DIM ** -0.5)

    def score_body(kc, carry):
        off = pl.multiple_of(kc * ck, ck)
        ikc = ik_ref[0, pl.ds(off, ck), :]
        acc = jnp.zeros((ck, tq), F32)
        for h4 in range(0, IDX_HEADS, G):
            d = _dot_nt(ikc, iq_ref[h4:h4 + G].reshape(G * tq, IDX_DIM))
            for j in range(G):
                acc = acc + jnp.maximum(d[:, j * tq:(j + 1) * tq], 0.0) * iw[h4 + j:h4 + j + 1, :]
        key = jnp.where(off + rowk <= qpos, _float_key(acc), jnp.int32(INT_MIN))
        keys_ref[pl.ds(off, ck), :] = key
        return carry

    lax.fori_loop(0, nk, score_body, 0)

    def count(pred):
        def body(kc, part):
            off = pl.multiple_of(kc * ck, ck)
            hit = jnp.where(pred(keys_ref[pl.ds(off, ck), :], off + rowk), 1.0, 0.0)
            return part + jnp.sum(hit.reshape(ck // PART, PART, tq), axis=0)
        part = lax.fori_loop(0, nk, body, jnp.zeros((PART, tq), F32))
        return jnp.sum(part, axis=0, keepdims=True)

    kf = float(k_sel)
    thr = _kth_largest_key(lambda cand: count(lambda kk, pos: kk >= cand), kf, (1, tq))
    thr = jnp.maximum(thr, jnp.int32(KEY_NEG_INF + 1))
    n_ge = count(lambda kk, pos: kk >= thr)
    jb_ref[...] = jnp.full((1, tq), L, I32)

    @pl.when(jnp.max(n_ge) > kf)
    def _():
        need = kf - count(lambda kk, pos: kk > thr)
        nb = max(1, (L - 1).bit_length())

        def body(t, pfx):
            cand = pfx | lax.shift_left(jnp.int32(1), jnp.int32(nb - 1) - t)
            cnt = count(lambda kk, pos: (kk == thr) & (pos < cand))
            return jnp.where(cnt < need, cand, pfx)

        pfx = lax.fori_loop(0, nb, body, jnp.zeros((1, tq), I32))
        jb_ref[...] = jnp.where(n_ge > kf, pfx + 1, L)

    jb = jb_ref[...]

    m_ref[...] = jnp.full(m_ref.shape, NEG_BIG, F32)
    acc_ref[...] = jnp.zeros(acc_ref.shape, F32)
    rowa = lax.broadcasted_iota(I32, (cka, 1), 0)

    def att_body(kc, carry):
        off = pl.multiple_of(kc * cka, cka)
        kk = keys_ref[pl.ds(off, cka), :]
        sel = (kk >= thr) & (kk < jnp.int32(KEY_POS_INF)) & ((kk > thr) | (off + rowa < jb))
        bias = jnp.where(sel, 0.0, -jnp.inf)
        bias = jnp.concatenate([bias] * G, axis=1)
        for g in range(ATT_KV_HEADS):
            kg = k_ref[0, g, pl.ds(off, cka), :]
            vg = vt_ref[0, g, :, pl.ds(off, cka)]
            qg = q_ref[g * G:(g + 1) * G].reshape(G * tq, ATT_HEAD_DIM)
            s = _dot_nt(kg, qg) + bias
            m = m_ref[g]
            smax = jnp.max(jnp.max(s.reshape(cka // PART, PART, G * tq), axis=0), axis=0, keepdims=True)
            m_new = jnp.maximum(m, smax)
            p = jnp.exp(s - m_new)
            acc_ref[g] = jnp.exp(m - m_new) * acc_ref[g] + _dot(vg, p.astype(BF16))
            m_ref[g] = m_new
        return carry

    lax.fori_loop(0, lax.div((i + 1) * tq + (cka - 1), cka), att_body, 0)
    for g in range(ATT_KV_HEADS):
        acc = acc_ref[g]
        og = acc[:ATT_HEAD_DIM] / acc[ATT_HEAD_DIM:ATT_HEAD_DIM + 1]
        for hh in range(G):
            o_ref[0, g * G + hh] = og[:, hh * tq:(hh + 1) * tq]


V_ROWS = ATT_HEAD_DIM + 16


def dsa_prompt(qiq, iw_t, ik, k_hm, v_t):
    b, _, L, _ = k_hm.shape
    tq = min(128, L)
    nq = L // tq
    ck = min(512, L)
    cka = ck
    gq = (ATT_HEADS // ATT_KV_HEADS) * tq
    k_sel = max(1, min(TOPK_MAX, L // 4))
    kern = functools.partial(_dsa_prompt_kernel, tq=tq, ck=ck, cka=cka, L=L, k_sel=k_sel)
    return pl.pallas_call(
        kern,
        grid=(b, L // tq),
        in_specs=[pl.BlockSpec((ATT_HEADS, tq, ATT_HEAD_DIM), lambda bi, i: (0, bi * nq + i, 0)),
                  pl.BlockSpec((IDX_HEADS, tq, IDX_DIM), lambda bi, i: (1, bi * nq + i, 0)),
                  pl.BlockSpec((1, IDX_HEADS, tq), lambda bi, i: (bi, 0, i)),
                  pl.BlockSpec((1, L, IDX_DIM), lambda bi, i: (bi, 0, 0)),
                  pl.BlockSpec((1, ATT_KV_HEADS, L, ATT_HEAD_DIM), lambda bi, i: (bi, 0, 0, 0)),
                  pl.BlockSpec((1, ATT_KV_HEADS, V_ROWS, L), lambda bi, i: (bi, 0, 0, 0))],
        out_specs=pl.BlockSpec((1, ATT_HEADS, ATT_HEAD_DIM, tq), lambda bi, i: (bi, 0, 0, i)),
        out_shape=jax.ShapeDtypeStruct((b, ATT_HEADS, ATT_HEAD_DIM, L), F32),
        scratch_shapes=[pltpu.VMEM((L, tq), I32), pltpu.VMEM((1, tq), I32),
                        pltpu.VMEM((ATT_KV_HEADS, 1, gq), F32),
                        pltpu.VMEM((ATT_KV_HEADS, V_ROWS, gq), F32)],
        compiler_params=_cparams("parallel", "arbitrary"),
        name="dsa_prompt",
    )(qiq, qiq, iw_t, ik, k_hm, v_t)


TQ8 = SUBLANES


def _sample_score_keys(d, iww):
    n = d.shape[1]
    r = jnp.maximum(d, 0.0) * (iww[:, 0:1] * (IDX_HEADS ** -0.5 * IDX_DIM ** -0.5))
    return _float_key(jnp.sum(r.reshape(TQ8, IDX_HEADS, n), axis=1))


def _dsa_sample_scores_kernel(pt_ref, *refs, n_pages):
    page_refs = refs[:n_pages]
    iq_ref, iww_ref, keys_ref = refs[n_pages:]
    ik_t = jnp.concatenate([r[0, 0] for r in page_refs], axis=1).astype(BF16)
    keys_ref[0] = _sample_score_keys(_dot(iq_ref[0], ik_t), iww_ref[0])


def _dsa_sample_thr_kernel(kp_ref, iq_ref, iww_ref, ikn_ref, thr_ref, jb_ref, kn_ref,
                           *, nb_seq, T, past, k_sel):
    R8 = nb_seq * TQ8
    kp = kp_ref[...].reshape(R8, past)
    rowi = lax.broadcasted_iota(I32, (TQ8, LANES), 0)
    col1 = lax.broadcasted_iota(I32, (TQ8, LANES), 1)
    kn = jnp.concatenate(
        [jnp.where((col1 <= rowi) & (col1 < T),
                   _sample_score_keys(_dot_nt(iq_ref[s], ikn_ref[s]), iww_ref[s]), jnp.int32(INT_MIN))
         for s in range(nb_seq)], axis=0)
    coln = lax.broadcasted_iota(I32, (R8, LANES), 1)
    colp = lax.broadcasted_iota(I32, (R8, past), 1)

    def count(pred):
        hit = jnp.where(pred(kp, colp), 1.0, 0.0)
        part = jnp.where(pred(kn, coln + past), 1.0, 0.0)
        for t in range(past // LANES):
            part = part + hit[:, t * LANES:(t + 1) * LANES]
        return jnp.sum(part, axis=1, keepdims=True)

    kf = float(k_sel)
    thr = _kth_largest_key(lambda cand: count(lambda kk, col: kk >= cand), kf, (R8, 1))
    thr = jnp.maximum(thr, jnp.int32(KEY_NEG_INF + 1))
    need = kf - count(lambda kk, col: kk > thr)
    nb = (past + LANES - 1).bit_length()

    def body(t, pfx):
        cand = pfx | lax.shift_left(jnp.int32(1), jnp.int32(nb - 1) - t)
        cnt = count(lambda kk, col: (kk == thr) & (col < cand))
        return jnp.where(cnt < need, cand, pfx)

    pfx = lax.fori_loop(0, nb, body, jnp.zeros((R8, 1), I32))
    thr_ref[...] = jnp.broadcast_to(thr, (R8, LANES)).reshape(nb_seq, TQ8, LANES)
    jb_ref[...] = jnp.broadcast_to(pfx + 1, (R8, LANES)).reshape(nb_seq, TQ8, LANES)
    kn_ref[...] = kn.reshape(nb_seq, TQ8, LANES)


def _dsa_sample_attn_kernel(pt_ref, *refs, n_pages, T, past):
    k_refs = refs[:n_pages]
    v_refs = refs[n_pages:2 * n_pages]
    (q_ref, kp_ref, thr_ref, jb_ref, kn_ref, knew_ref, vnew_ref,
     o_ref, m_ref, l_ref, acc_ref) = refs[2 * n_pages:]
    p = pl.program_id(1)
    R = T * ATT_HEADS

    @pl.when(p == 0)
    def _():
        m_ref[...] = jnp.full((R, 1), NEG_BIG, F32)
        l_ref[...] = jnp.zeros((R, 1), F32)
        acc_ref[...] = jnp.zeros((R, KV_WIDTH), F32)

    thr = thr_ref[0][0:T, 0:1]
    jb = jb_ref[0][0:T, 0:1]

    def step(kk, col0, s, pv):
        n = kk.shape[1]
        col = col0 + lax.broadcasted_iota(I32, (1, n), 1)
        sel = (kk >= thr) & (kk < jnp.int32(KEY_POS_INF)) & ((kk > thr) | (col < jb))
        bias = jnp.where(sel, 0.0, -jnp.inf)
        s = (s.reshape(T, ATT_HEADS, n) + bias[:, None, :]).reshape(R, n)
        m = m_ref[...]
        m_new = jnp.maximum(m, jnp.max(s, axis=1, keepdims=True))
        alpha = jnp.exp(m - m_new)
        pr = jnp.exp(s - m_new)
        l_ref[...] = alpha * l_ref[...] + jnp.sum(pr, axis=1, keepdims=True)
        acc_ref[...] = alpha * acc_ref[...] + pv(pr.astype(BF16))
        m_ref[...] = m_new

    k_t = jnp.concatenate([r[0, 0] for r in k_refs], axis=1).astype(BF16)
    v_t = jnp.concatenate([r[0, 0] for r in v_refs], axis=1).astype(BF16)
    step(kp_ref[0][0:T], p * (n_pages * PAGE_SIZE), _dot(q_ref[0], k_t), lambda pr: _dot_nt(pr, v_t))

    @pl.when(p == pl.num_programs(1) - 1)
    def _():
        step(kn_ref[0][0:T], past, _dot_nt(q_ref[0], knew_ref[0]), lambda pr: _dot(pr, vnew_ref[0]))
        o_ref[0] = acc_ref[...] / l_ref[...]


def _pages_per_step(npages, want):
    while npages % want:
        want //= 2
    return want


def dsa_sample(q, k, v, iq, iw, ik, ck_t, cv_t, cik_t, layer, page_table):
    b, T, _ = q.shape
    assert T <= TQ8
    npages = page_table.shape[1]
    past = npages * PAGE_SIZE
    k_sel = max(1, min(TOPK_MAX, (past + T) // 4))
    R = T * ATT_HEADS
    R8 = TQ8 * IDX_HEADS
    pad_to = lambda a, n: jnp.pad(a, ((0, 0), (0, n - a.shape[1]), (0, 0)))
    iq_rows = pad_to(iq.reshape(b, T * IDX_HEADS, IDX_DIM), R8).astype(BF16)
    iww = jnp.broadcast_to(pad_to(iw.reshape(b, T * IDX_HEADS, 1), R8), (b, R8, LANES)).astype(F32)
    ik_new, k_new, v_new = (pad_to(a, LANES).astype(BF16) for a in (ik, k, v))
    head_group = jnp.arange(ATT_HEADS) // (ATT_HEADS // ATT_KV_HEADS)
    onehot = (head_group[:, None] == jnp.arange(ATT_KV_HEADS)[None, :]).astype(F32)
    q_bd = (q.reshape(b, T, ATT_HEADS, 1, ATT_HEAD_DIM) * onehot[None, None, :, :, None])
    q_bd = q_bd.reshape(b, R, KV_WIDTH).astype(BF16)

    def page_spec(rows, per_step, j):
        return pl.BlockSpec((1, 1, rows, PAGE_SIZE),
                            lambda bi, p, pt: (layer, pt[bi, p * per_step + j], 0, 0))

    fixed = lambda shape: pl.BlockSpec((1,) + shape, lambda bi, p, pt: (bi, 0, 0))

    ps = _pages_per_step(npages, 32)
    keys_past = pl.pallas_call(
        functools.partial(_dsa_sample_scores_kernel, n_pages=ps),
        grid_spec=pltpu.PrefetchScalarGridSpec(
            num_scalar_prefetch=1, grid=(b, npages // ps),
            in_specs=[page_spec(IDX_DIM, ps, j) for j in range(ps)]
            + [fixed((R8, IDX_DIM)), fixed((R8, LANES))],
            out_specs=pl.BlockSpec((1, TQ8, ps * PAGE_SIZE), lambda bi, p, pt: (bi, 0, p))),
        out_shape=jax.ShapeDtypeStruct((b, TQ8, past), I32),
        compiler_params=_cparams("parallel", "arbitrary"),
        name="dsa_sample_scores",
    )(page_table, *([cik_t] * ps), iq_rows, iww)

    nb_seq = SUBLANES if b % SUBLANES == 0 else 1
    seq_spec = lambda r, w: pl.BlockSpec((nb_seq, r, w), lambda bi: (bi, 0, 0))
    thr, jb, keys_new = pl.pallas_call(
        functools.partial(_dsa_sample_thr_kernel, nb_seq=nb_seq, T=T, past=past, k_sel=k_sel),
        grid=(b // nb_seq,),
        in_specs=[seq_spec(TQ8, past), seq_spec(R8, IDX_DIM), seq_spec(R8, LANES),
                  seq_spec(LANES, IDX_DIM)],
        out_specs=[seq_spec(TQ8, LANES)] * 3,
        out_shape=[jax.ShapeDtypeStruct((b, TQ8, LANES), I32)] * 3,
        compiler_params=_cparams("parallel"),
        name="dsa_sample_threshold",
    )(keys_past, iq_rows, iww, ik_new)

    pa = _pages_per_step(npages, 16)
    o = pl.pallas_call(
        functools.partial(_dsa_sample_attn_kernel, n_pages=pa, T=T, past=past),
        grid_spec=pltpu.PrefetchScalarGridSpec(
            num_scalar_prefetch=1, grid=(b, npages // pa),
            in_specs=[page_spec(KV_WIDTH, pa, j) for j in range(pa)] * 2
            + [fixed((R, KV_WIDTH)),
               pl.BlockSpec((1, TQ8, pa * PAGE_SIZE), lambda bi, p, pt: (bi, 0, p)),
               fixed((TQ8, LANES)), fixed((TQ8, LANES)), fixed((TQ8, LANES)),
               fixed((LANES, KV_WIDTH)), fixed((LANES, KV_WIDTH))],
            out_specs=fixed((R, KV_WIDTH)),
            scratch_shapes=[pltpu.VMEM((R, 1), F32), pltpu.VMEM((R, 1), F32),
                            pltpu.VMEM((R, KV_WIDTH), F32)]),
        out_shape=jax.ShapeDtypeStruct((b, R, KV_WIDTH), F32),
        compiler_params=_cparams("parallel", "arbitrary"),
        name="dsa_sample_attention",
    )(page_table, *([ck_t] * pa), *([cv_t] * pa), q_bd, keys_past, thr, jb, keys_new, k_new, v_new)

    o = o.reshape(b, T, ATT_HEADS, ATT_KV_HEADS, ATT_HEAD_DIM)
    o = jnp.sum(o * onehot[None, None, :, :, None], axis=3)
    return o.reshape(b, T, ATT_HEADS * ATT_HEAD_DIM)


def _merge_kernel(ys_ref, ya_ref, yr_ref, g_ref, h_ref, wb_ref, wo_ref, o_ref):
    acc = None
    for n, y_ref in enumerate((ys_ref, ya_ref, yr_ref)):
        pr = _dot(y_ref[...].astype(BF16), wb_ref[n])
        t = jax.nn.sigmoid(g_ref[:, n * D_MODEL:(n + 1) * D_MODEL]) * pr
        acc = t if acc is None else acc + t
    o_ref[...] = h_ref[...] + _dot(acc.astype(BF16), wo_ref[...])


def merge_branches(ys, ya, yr, gates, h, wb, wo):
    M = h.shape[0]
    tm = _pick(M, (256, 128))
    rows = lambda w: pl.BlockSpec((tm, w), lambda i: (i, 0))
    return pl.pallas_call(
        _merge_kernel,
        grid=(M // tm,),
        in_specs=[rows(D_MODEL), rows(D_MODEL), rows(D_MODEL), rows(N_BRANCH * D_MODEL), rows(D_MODEL),
                  pl.BlockSpec((N_BRANCH, D_MODEL, D_MODEL), lambda i: (0, 0, 0)),
                  pl.BlockSpec((D_MODEL, D_MODEL), lambda i: (0, 0))],
        out_specs=rows(D_MODEL),
        out_shape=jax.ShapeDtypeStruct((M, D_MODEL), F32),
        compiler_params=_cparams("parallel"),
        name="merge_branches",
    )(ys, ya, yr, gates, h, wb, wo)


def _ffn_kernel(x_ref, g_ref, wg_ref, wu_ref, wd_ref, o_ref, xn_ref, acc_ref):
    j = pl.program_id(1)

    @pl.when(j == 0)
    def _():
        x = x_ref[...]
        ms = jnp.mean(x * x, axis=-1, keepdims=True)
        xn_ref[...] = (x * lax.rsqrt(ms + EPS) * g_ref[...]).astype(BF16)
        acc_ref[...] = jnp.zeros_like(acc_ref)

    xn = xn_ref[...]
    a = _silu(_dot(xn, wg_ref[...])) * _dot(xn, wu_ref[...])
    acc_ref[...] += _dot(a.astype(BF16), wd_ref[...])

    @pl.when(j == pl.num_programs(1) - 1)
    def _():
        o_ref[...] = x_ref[...] + acc_ref[...]


def ffn_dense(x, g, wg, wu, wd):
    M = x.shape[0]
    tm = _pick(M, (1024, 512, 256, 128))
    tf = 512
    return pl.pallas_call(
        _ffn_kernel,
        grid=(M // tm, D_FF // tf),
        in_specs=[pl.BlockSpec((tm, D_MODEL), lambda i, j: (i, 0)),
                  pl.BlockSpec((1, D_MODEL), lambda i, j: (0, 0)),
                  pl.BlockSpec((D_MODEL, tf), lambda i, j: (0, j)),
                  pl.BlockSpec((D_MODEL, tf), lambda i, j: (0, j)),
                  pl.BlockSpec((tf, D_MODEL), lambda i, j: (j, 0))],
        out_specs=pl.BlockSpec((tm, D_MODEL), lambda i, j: (i, 0)),
        out_shape=jax.ShapeDtypeStruct((M, D_MODEL), F32),
        scratch_shapes=[pltpu.VMEM((tm, D_MODEL), BF16), pltpu.VMEM((tm, D_MODEL), F32)],
        compiler_params=_cparams("parallel", "arbitrary"),
        name="ffn_dense",
    )(x, g.reshape(1, D_MODEL), wg, wu, wd)


def _router_kernel(x_ref, g_ref, wr_ref, gate_ref):
    x = x_ref[...]
    ms = jnp.mean(x * x, axis=-1, keepdims=True)
    xn = x * lax.rsqrt(ms + EPS) * g_ref[...]
    logits = jnp.dot(xn, wr_ref[...], preferred_element_type=F32, precision=lax.Precision.HIGHEST)
    lane = lax.broadcasted_iota(I32, logits.shape, 1)
    logits = jnp.where(lane < N_EXPERTS, logits, -jnp.inf)
    v1 = jnp.max(logits, axis=1, keepdims=True)
    i1 = jnp.min(jnp.where(logits == v1, lane, LANES), axis=1, keepdims=True)
    rest = jnp.where(lane == i1, -jnp.inf, logits)
    v2 = jnp.max(rest, axis=1, keepdims=True)
    i2 = jnp.min(jnp.where(rest == v2, lane, LANES), axis=1, keepdims=True)
    e2 = jnp.exp(v2 - v1)
    w1 = 1.0 / (1.0 + e2)
    w2 = e2 / (1.0 + e2)
    gate_ref[...] = jnp.where(lane == i1, w1, 0.0) + jnp.where(lane == i2, w2, 0.0)


def moe_router(x, g, wr):
    M = x.shape[0]
    tm = _pick(M, (512, 256, 128))
    wr_pad = jnp.pad(wr.astype(F32), ((0, 0), (0, LANES - N_EXPERTS)))
    return pl.pallas_call(
        _router_kernel,
        grid=(M // tm,),
        in_specs=[pl.BlockSpec((tm, D_MODEL), lambda i: (i, 0)),
                  pl.BlockSpec((1, D_MODEL), lambda i: (0, 0)),
                  pl.BlockSpec((D_MODEL, LANES), lambda i: (0, 0))],
        out_specs=pl.BlockSpec((tm, LANES), lambda i: (i, 0)),
        out_shape=jax.ShapeDtypeStruct((M, LANES), F32),
        compiler_params=_cparams("parallel"),
        name="moe_router",
    )(x, g.reshape(1, D_MODEL), wr_pad)


MOE_ROWS = 288


def _moe_kernel(x_ref, g_ref, gate_ref, gatet_ref, wg_ref, wu_ref, wd_ref, o_ref,
                xn_ref, rkc_ref, rkr_ref, xe_ref, ye_ref, cnt_ref, *, tm, R, nch):
    e = pl.program_id(1)
    j = pl.program_id(2)
    last_j = pl.num_programs(2) - 1

    @pl.when((e == 0) & (j == 0))
    def _():
        x = x_ref[...]
        ms = jnp.mean(x * x, axis=-1, keepdims=True)
        xn_ref[...] = (x * lax.rsqrt(ms + EPS) * g_ref[...]).astype(BF16)
        o_ref[...] = x
        ti = lax.broadcasted_iota(I32, (tm, tm), 0)
        tj = lax.broadcasted_iota(I32, (tm, tm), 1)
        flags = jnp.where(gate_ref[...] > 0.0, 1.0, 0.0).astype(BF16)
        flags_t = jnp.where(gatet_ref[...] > 0.0, 1.0, 0.0).astype(BF16)
        rkc_ref[...] = _dot(jnp.where(ti > tj, 1.0, 0.0).astype(BF16), flags)
        rkr_ref[...] = _dot(flags_t, jnp.where(ti < tj, 1.0, 0.0).astype(BF16))

    @pl.when(j == 0)
    def _():
        rank_row = rkr_ref[pl.ds(e, 1), :]
        flag_row = gatet_ref[pl.ds(e, 1), :] > 0.0
        cnt = jnp.sum(jnp.where(flag_row, 1, 0))
        cnt_ref[0] = cnt
        ye_ref[...] = jnp.zeros_like(ye_ref)
        for c in range(nch):
            @pl.when(c * R < cnt)
            def _():
                slot = (c * R + lax.broadcasted_iota(I32, (R, 1), 0)).astype(F32)
                sel = jnp.where((rank_row == slot) & flag_row, 1.0, 0.0).astype(BF16)
                xe_ref[c * R:(c + 1) * R, :] = _dot(sel, xn_ref[...]).astype(BF16)

    cnt = cnt_ref[0]
    for c in range(nch):
        @pl.when(c * R < cnt)
        def _():
            xe = xe_ref[c * R:(c + 1) * R, :]
            a = _silu(_dot(xe, wg_ref[0])) * _dot(xe, wu_ref[0])
            ye_ref[c * R:(c + 1) * R, :] += _dot(a.astype(BF16), wd_ref[0])

    @pl.when(j == last_j)
    def _():
        lane = lax.broadcasted_iota(I32, (tm, LANES), 1)
        gcol = jnp.sum(jnp.where(lane == e, gate_ref[...], 0.0), axis=1, keepdims=True)
        rcol = jnp.sum(jnp.where(lane == e, rkc_ref[...], 0.0), axis=1, keepdims=True)
        for c in range(nch):
            @pl.when(c * R < cnt)
            def _():
                slot = (c * R + lax.broadcasted_iota(I32, (1, R), 1)).astype(F32)
                sel_t = jnp.where((rcol == slot) & (gcol > 0.0), 1.0, 0.0).astype(BF16)
                ye = ye_ref[c * R:(c + 1) * R, :]
                hi = ye.astype(BF16)
                lo = (ye - hi.astype(F32)).astype(BF16)
                o_ref[...] += gcol * (_dot(sel_t, hi) + _dot(sel_t, lo))


def moe_ffn(x, g, gate, wg, wu, wd):
    M = x.shape[0]
    tm = _pick(M, (1024, 512, 256, 128))
    tf = 896
    R = min(MOE_ROWS, tm)
    nch = -(-tm // R)
    kern = functools.partial(_moe_kernel, tm=tm, R=R, nch=nch)
    return pl.pallas_call(
        kern,
        grid=(M // tm, N_EXPERTS, D_FF // tf),
        in_specs=[pl.BlockSpec((tm, D_MODEL), lambda i, e, j: (i, 0)),
                  pl.BlockSpec((1, D_MODEL), lambda i, e, j: (0, 0)),
                  pl.BlockSpec((tm, LANES), lambda i, e, j: (i, 0)),
                  pl.BlockSpec((LANES, tm), lambda i, e, j: (0, i)),
                  pl.BlockSpec((1, D_MODEL, tf), lambda i, e, j: (e, 0, j)),
                  pl.BlockSpec((1, D_MODEL, tf), lambda i, e, j: (e, 0, j)),
                  pl.BlockSpec((1, tf, D_MODEL), lambda i, e, j: (e, j, 0))],
        out_specs=pl.BlockSpec((tm, D_MODEL), lambda i, e, j: (i, 0)),
        out_shape=jax.ShapeDtypeStruct((M, D_MODEL), F32),
        scratch_shapes=[pltpu.VMEM((tm, D_MODEL), BF16),
                        pltpu.VMEM((tm, LANES), F32), pltpu.VMEM((LANES, tm), F32),
                        pltpu.VMEM((nch * R, D_MODEL), BF16), pltpu.VMEM((nch * R, D_MODEL), F32),
                        pltpu.SMEM((1,), I32)],
        compiler_params=pltpu.CompilerParams(
            dimension_semantics=("parallel", "arbitrary", "arbitrary"), vmem_limit_bytes=MOE_VMEM_LIMIT),
        name="moe_ffn",
    )(x, g.reshape(1, D_MODEL), gate, gate.T, wg, wu, wd)


def _rmsnorm_kernel(x_ref, g_ref, o_ref):
    x = x_ref[...]
    ms = jnp.mean(x * x, axis=-1, keepdims=True)
    o_ref[...] = x * lax.rsqrt(ms + EPS) * g_ref[...]


def rmsnorm(x, g):
    M = x.shape[0]
    tm = _pick(M, (1024, 512, 256, 128))
    return pl.pallas_call(
        _rmsnorm_kernel,
        grid=(M // tm,),
        in_specs=[pl.BlockSpec((tm, D_MODEL), lambda i: (i, 0)),
                  pl.BlockSpec((1, D_MODEL), lambda i: (0, 0))],
        out_specs=pl.BlockSpec((tm, D_MODEL), lambda i: (i, 0)),
        out_shape=jax.ShapeDtypeStruct((M, D_MODEL), F32),
        compiler_params=_cparams("parallel"),
        name="final_rmsnorm",
    )(x, g.reshape(1, D_MODEL))


def _split_w_in(w_in):
    offs = np.cumsum((0,) + IN_SPLITS)
    seg = {n: w_in[:, offs[i]:offs[i + 1]] for i, n in enumerate(
        ("z", "xbc", "dt", "q", "k", "v", "iq", "iw", "ik", "rq", "rk", "rv", "rg", "gates"))}
    cat = lambda *names: jnp.concatenate([seg[n] if isinstance(n, str) else n for n in names], axis=1)
    dt_wide = jnp.repeat(seg["dt"], SSM_HEAD_DIM, axis=1)
    small = cat("dt", "iw", jnp.zeros((D_MODEL, LANES - 2 * IDX_HEADS - IDX_DIM), w_in.dtype), "ik")
    q_scaled = seg["q"] * (ATT_HEAD_DIM ** -0.5)
    groups = dict(ssm=cat("z", "xbc", dt_wide), qiq=cat(q_scaled, "iq"), kvs=cat("k", "v", small),
                  ret=cat("rq", "rk", "rv", "rg"), gate=seg["gates"])
    return {n: w.astype(BF16) for n, w in groups.items()}


def _rope_tables(pos):
    half = RET_DK // 2
    inv = ROPE_BASE ** (-jnp.arange(half, dtype=F32) / half)
    ang = pos.astype(F32)[:, None] * inv[None, :]
    cos, sin = jnp.cos(ang), jnp.sin(ang)
    return jnp.concatenate([cos, cos], axis=1), jnp.concatenate([-sin, sin], axis=1)


def _pad_rows(a, L):
    return jnp.pad(a, ((0, 0), (0, L - a.shape[1]), (0, 0)))


def _mixers(h, pos0, conv0, ssm0, ret0, attend, lw):
    b, L, _ = h.shape
    hf = h.reshape(b * L, D_MODEL)
    proj = {n: rms_matmul(hf, lw["norm_g"], w).reshape(b, L, -1)
            for n, w in lw["w_in"].items() if n != "qiq"}
    qiq = rms_matmul_heads(hf, lw["norm_g"], lw["w_in"]["qiq"], ATT_HEAD_DIM)
    k = proj["kvs"][..., :KV_WIDTH]
    v = proj["kvs"][..., KV_WIDTH:2 * KV_WIDTH]
    small = proj["kvs"][..., 2 * KV_WIDTH:]
    ik = small[..., LANES - IDX_DIM:]

    Lp = -(-L // CHUNK) * CHUNK
    last_valid = L - (Lp - CHUNK)
    dtT = jnp.swapaxes(_pad_rows(small[..., :SSM_HEADS], Lp), 1, 2)
    s0 = jnp.transpose(ssm0, (0, 2, 1, 3)).reshape(b, SSM_STATE, SSM_INNER)
    ys, conv_new, s_new = ssd_branch(_pad_rows(proj["ssm"], Lp), dtT, conv0, s0, lw["conv_w"], lw["conv_b"],
                                     lw["dt_bias"], lw["a_log"], lw["d_skip"], lw["ssm_norm_g"], last_valid)
    ssm_new = jnp.transpose(s_new.reshape(b, SSM_STATE, SSM_HEADS, SSM_HEAD_DIM), (0, 2, 1, 3))

    cos2, sin2 = _rope_tables(pos0 + jnp.arange(Lp))
    yr, ret_new = ret_branch(_pad_rows(proj["ret"], Lp), cos2, sin2, ret0, last_valid)

    ya = attend(qiq, k, v, small, ik, b, L)

    ys = ys[:, :L].reshape(b * L, -1)
    yr = yr[:, :L].reshape(b * L, -1)
    out = merge_branches(ys, ya.reshape(b * L, -1), yr, proj["gate"].reshape(b * L, -1), hf,
                         lw["w_branch"], lw["w_out"])
    return out.reshape(b, L, D_MODEL), (k.reshape(b, L, ATT_KV_HEADS, ATT_HEAD_DIM),
                                        v.reshape(b, L, ATT_KV_HEADS, ATT_HEAD_DIM),
                                        ik, ssm_new, conv_new, ret_new)


def _attend_prompt(qiq, k, v, small, ik, b, L):
    k_hm = jnp.transpose(k.reshape(b, L, ATT_KV_HEADS, ATT_HEAD_DIM), (0, 2, 1, 3)).astype(BF16)
    v_t = jnp.transpose(v.reshape(b, L, ATT_KV_HEADS, ATT_HEAD_DIM), (0, 2, 3, 1)).astype(BF16)
    v_t = jnp.concatenate([v_t, jnp.ones((b, ATT_KV_HEADS, V_ROWS - ATT_HEAD_DIM, L), BF16)], axis=2)
    iw_t = jnp.swapaxes(small[..., IDX_HEADS:2 * IDX_HEADS], 1, 2)
    o = dsa_prompt(qiq, iw_t, ik.astype(BF16), k_hm, v_t)
    return jnp.transpose(o, (0, 3, 1, 2)).reshape(b, L, ATT_HEADS * ATT_HEAD_DIM)


def _attend_sample(qiq, k, v, small, ik, b, L, *, ck_t, cv_t, cik_t, layer, page_table):
    rows = lambda a: jnp.transpose(a, (1, 0, 2)).reshape(b, L, ATT_HEADS * ATT_HEAD_DIM)
    iw = small[..., IDX_HEADS:2 * IDX_HEADS]
    return dsa_sample(rows(qiq[:ATT_HEADS]), k, v, rows(qiq[ATT_HEADS:]), iw, ik,
                      ck_t, cv_t, cik_t, layer, page_table)


def _channel_mixer(h, l, p):
    b, L, _ = h.shape
    hf = h.reshape(b * L, D_MODEL)
    j = l // 2
    if l % 2 == 0:
        out = ffn_dense(hf, p["norm_ffn_g"][l], p["w_ffn_gate"][j].astype(BF16),
                        p["w_ffn_up"][j].astype(BF16), p["w_ffn_down"][j].astype(BF16))
    else:
        gate = moe_router(hf, p["norm_ffn_g"][l], p["w_router"][j])
        out = moe_ffn(hf, p["norm_ffn_g"][l], gate, p["w_moe_gate"][j].astype(BF16),
                      p["w_moe_up"][j].astype(BF16), p["w_moe_down"][j].astype(BF16))
    return out.reshape(b, L, D_MODEL)


def kernel(x_prompt, x_sample, cache_k, cache_v, cache_idx_k, state_ssm, state_conv, state_ret,
           page_table, norm_mix_g, w_in, conv_w, conv_b, dt_bias, a_log, d_skip, ssm_norm_g,
           w_branch, w_out, norm_ffn_g, w_ffn_gate, w_ffn_up, w_ffn_down, w_router,
           w_moe_gate, w_moe_up, w_moe_down, final_norm_g):
    depth = w_in.shape[0]
    past = page_table.shape[1] * PAGE_SIZE
    bp = x_prompt.shape[0]
    p = dict(norm_ffn_g=norm_ffn_g, w_ffn_gate=w_ffn_gate, w_ffn_up=w_ffn_up, w_ffn_down=w_ffn_down,
             w_router=w_router, w_moe_gate=w_moe_gate, w_moe_up=w_moe_up, w_moe_down=w_moe_down)
    n_phys = cache_k.shape[1]
    ck_t = jnp.transpose(cache_k, (0, 1, 3, 4, 2)).reshape(depth, n_phys, KV_WIDTH, PAGE_SIZE)
    cv_t = jnp.transpose(cache_v, (0, 1, 3, 4, 2)).reshape(depth, n_phys, KV_WIDTH, PAGE_SIZE)
    cik_t = jnp.transpose(cache_idx_k, (0, 1, 3, 2))
    hp, hs = x_prompt, x_sample
    st_p, st_s = [], []
    for l in range(depth):
        lw = dict(norm_g=norm_mix_g[l], w_in=_split_w_in(w_in[l]), conv_w=conv_w[l], conv_b=conv_b[l],
                  dt_bias=dt_bias[l], a_log=a_log[l], d_skip=d_skip[l], ssm_norm_g=ssm_norm_g[l],
                  w_branch=w_branch[l].astype(BF16), w_out=w_out[l].astype(BF16))
        hp, sp = _mixers(hp, 0,
                         jnp.zeros((bp, CONV_WIDTH - 1, CONV_DIM), F32),
                         jnp.zeros((bp, SSM_HEADS, SSM_STATE, SSM_HEAD_DIM), F32),
                         jnp.zeros((bp, RET_HEADS, RET_DK, RET_DV), F32),
                         _attend_prompt, lw)
        attend_s = functools.partial(_attend_sample, ck_t=ck_t, cv_t=cv_t, cik_t=cik_t, layer=l,
                                     page_table=page_table)
        hs, ss = _mixers(hs, past, state_conv[l], state_ssm[l], state_ret[l], attend_s, lw)
        hp = _channel_mixer(hp, l, p)
        hs = _channel_mixer(hs, l, p)
        st_p.append(sp)
        st_s.append(ss)
    y_prompt = rmsnorm(hp.reshape(-1, D_MODEL), final_norm_g).reshape(hp.shape)
    y_sample = rmsnorm(hs.reshape(-1, D_MODEL), final_norm_g).reshape(hs.shape)
    stack = lambda sts, i: jnp.stack([s[i] for s in sts])
    return (y_prompt, y_sample,
            stack(st_p, 0), stack(st_p, 1), stack(st_p, 2), stack(st_p, 3), stack(st_p, 4), stack(st_p, 5),
            stack(st_s, 0), stack(st_s, 1), stack(st_s, 2), stack(st_s, 3), stack(st_s, 4), stack(st_s, 5))
```

```python
import functools
import math

import jax
import jax.numpy as jnp
import numpy as np
from jax import lax
from jax.experimental import pallas as pl
from jax.experimental.pallas import tpu as pltpu

F32 = jnp.float32
BF16 = jnp.bfloat16
I32 = jnp.int32

D_MODEL = 1024
PAGE_SIZE = 128
SSM_HEADS = 16
SSM_HEAD_DIM = 64
SSM_INNER = 1024
SSM_GROUPS = 2
SSM_STATE = 128
CONV_WIDTH = 4
CONV_DIM = 1536
ATT_HEADS = 16
ATT_KV_HEADS = 4
ATT_HEAD_DIM = 64
KV_WIDTH = 256
IDX_HEADS = 16
IDX_DIM = 64
TOPK_MAX = 256
RET_HEADS = 4
RET_DK = 128
RET_DV = 256
ROPE_BASE = 10000.0
N_BRANCH = 3
D_FF = 3584
N_EXPERTS = 8
EPS = 1e-6
IN_SPLITS = (1024, 1536, 16, 1024, 256, 256, 1024, 16, 64, 512, 512, 1024, 1024, 3072)

LANES = 128
SUBLANES = 8
VMEM_LIMIT = 48 * 1024 * 1024
MOE_VMEM_LIMIT = 56 * 1024 * 1024

CHUNK = 128
INT_MIN = -(2 ** 31)
KEY_POS_INF = 0x7F800000
KEY_NEG_INF = -2139095041
NEG_BIG = -1e30


def _cparams(*sem):
    return pltpu.CompilerParams(dimension_semantics=sem, vmem_limit_bytes=VMEM_LIMIT)


def _dot(a, b):
    return jnp.dot(a, b, preferred_element_type=F32)


def _dot_nt(a, b):
    return lax.dot_general(a, b, (((1,), (1,)), ((), ())), preferred_element_type=F32)


def _dot_tn(a, b):
    return lax.dot_general(a, b, (((0,), (0,)), ((), ())), preferred_element_type=F32)


def _split3(x):
    hi = x.astype(BF16)
    r = x - hi.astype(F32)
    mid = r.astype(BF16)
    lo = (r - mid.astype(F32)).astype(BF16)
    return hi, mid, lo


def _silu(x):
    return x * jax.nn.sigmoid(x)


def _softplus(x):
    return jnp.maximum(x, 0.0) + jnp.log1p(jnp.exp(-jnp.abs(x)))


def _float_key(x):
    x = jnp.where(x == 0.0, 0.0, x)
    b = lax.bitcast_convert_type(x, I32)
    return jnp.where(b >= 0, b, b ^ jnp.int32(0x7FFFFFFF))


def _rms_matmul_kernel(x_ref, g_ref, w_ref, o_ref):
    x = x_ref[...]
    ms = jnp.mean(x * x, axis=-1, keepdims=True)
    xn = (x * lax.rsqrt(ms + EPS) * g_ref[...]).astype(BF16)
    o_ref[...] = _dot(xn, w_ref[...]).astype(o_ref.dtype)


def _rms_matmul_heads_kernel(x_ref, g_ref, w_ref, o_ref, *, n_heads, head_dim):
    x = x_ref[...]
    ms = jnp.mean(x * x, axis=-1, keepdims=True)
    xn = (x * lax.rsqrt(ms + EPS) * g_ref[...]).astype(BF16)
    res = _dot(xn, w_ref[...])
    for h in range(n_heads):
        o_ref[h] = res[:, h * head_dim:(h + 1) * head_dim].astype(o_ref.dtype)


def _pick(n, prefs):
    for p in prefs:
        if n % p == 0:
            return p
    return n


def rms_matmul(x, g, w, out_dtype=F32):
    M, K = x.shape
    N = w.shape[1]
    tm = _pick(M, (512, 256, 128))
    return pl.pallas_call(
        _rms_matmul_kernel,
        grid=(M // tm,),
        in_specs=[pl.BlockSpec((tm, K), lambda i: (i, 0)),
                  pl.BlockSpec((1, K), lambda i: (0, 0)),
                  pl.BlockSpec((K, N), lambda i: (0, 0))],
        out_specs=pl.BlockSpec((tm, N), lambda i: (i, 0)),
        out_shape=jax.ShapeDtypeStruct((M, N), out_dtype),
        compiler_params=_cparams("parallel"),
        name="rms_matmul",
    )(x, g.reshape(1, K), w)


def rms_matmul_heads(x, g, w, head_dim):
    M, K = x.shape
    N = w.shape[1]
    n_heads = N // head_dim
    tm = _pick(M, (512, 256, 128))
    kern = functools.partial(_rms_matmul_heads_kernel, n_heads=n_heads, head_dim=head_dim)
    return pl.pallas_call(
        kern,
        grid=(M // tm,),
        in_specs=[pl.BlockSpec((tm, K), lambda i: (i, 0)),
                  pl.BlockSpec((1, K), lambda i: (0, 0)),
                  pl.BlockSpec((K, N), lambda i: (0, 0))],
        out_specs=pl.BlockSpec((n_heads, tm, head_dim), lambda i: (0, i, 0)),
        out_shape=jax.ShapeDtypeStruct((n_heads, M, head_dim), BF16),
        compiler_params=_cparams("parallel"),
        name="rms_matmul_heads",
    )(x, g.reshape(1, K), w)


def _ssd_kernel(zxd_ref, dtT_ref, conv0_ref, s0_ref, cw_ref, cb_ref, dtb_ref, alog_ref, dsk_ref,
                ng_ref, dtbT_ref, alogT_ref, y_ref, convn_ref, sn_ref, xpad_ref, st_ref,
                *, Q, last_valid, nc):
    c = pl.program_id(1)
    GW = SSM_INNER // SSM_GROUPS

    @pl.when(c == 0)
    def _():
        xpad_ref[0:8, :] = jnp.zeros((8, CONV_DIM), F32)
        xpad_ref[5:8, :] = conv0_ref[0]
        st_ref[...] = s0_ref[0]

    blk = zxd_ref[0]
    z = blk[:, :SSM_INNER]
    xbc = blk[:, SSM_INNER:SSM_INNER + CONV_DIM]
    dtr = blk[:, SSM_INNER + CONV_DIM:]
    xpad_ref[8:8 + Q, :] = xbc
    cw = cw_ref[...]
    conv = (xpad_ref[5:5 + Q, :] * cw[0:1] + xpad_ref[6:6 + Q, :] * cw[1:2]
            + xpad_ref[7:7 + Q, :] * cw[2:3] + xbc * cw[3:4]) + cb_ref[...]
    xc = _silu(conv)
    xs = xc[:, :SSM_INNER]
    Bm = xc[:, SSM_INNER:SSM_INNER + SSM_GROUPS * SSM_STATE]
    Cm = xc[:, SSM_INNER + SSM_GROUPS * SSM_STATE:]

    row = lax.broadcasted_iota(I32, (Q, 1), 0)
    colq = lax.broadcasted_iota(I32, (1, Q), 1)
    assert last_valid == Q or nc == 1
    lv = last_valid
    valid = row < lv
    tril = (lax.broadcasted_iota(I32, (Q, Q), 0) >= lax.broadcasted_iota(I32, (Q, Q), 1))
    tril_b = jnp.where(tril, 1.0, 0.0).astype(BF16)
    triu_b = jnp.where(lax.broadcasted_iota(I32, (Q, Q), 0) <= lax.broadcasted_iota(I32, (Q, Q), 1),
                       1.0, 0.0).astype(BF16)

    nega = -jnp.exp(alog_ref[...])
    dt = _softplus(dtr + dtb_ref[...])
    la = jnp.where(valid, dt * nega, 0.0)
    cum = sum(_dot(tril_b, p) for p in _split3(la))
    laT = jnp.where(colq < lv, _softplus(dtT_ref[0] + dtbT_ref[...]) * (-jnp.exp(alogT_ref[...])), 0.0)
    cumT = sum(_dot(p, triu_b) for p in _split3(laT))

    ecum = jnp.exp(cum)
    cl = cum[lv - 1:lv, :]
    xdt = xs * dt
    xtail = jnp.where(valid, xdt * jnp.exp(cl - cum), 0.0)
    cdecay = jnp.exp(cl)

    lane = lax.broadcasted_iota(I32, (1, LANES), 1)
    y_groups = []
    for g in range(SSM_GROUPS):
        l0 = g * GW
        Cg = Cm[:, g * SSM_STATE:(g + 1) * SSM_STATE].astype(BF16)
        Bg = Bm[:, g * SSM_STATE:(g + 1) * SSM_STATE].astype(BF16)
        G = _dot_nt(Cg, Bg)
        st_g = st_ref[:, l0:l0 + GW]
        inter = _dot(Cg, st_g.astype(BF16)) * ecum[:, l0:l0 + GW]
        local = _dot_tn(Bg, xtail[:, l0:l0 + GW].astype(BF16))
        st_ref[:, l0:l0 + GW] = st_g * cdecay[:, l0:l0 + GW] + local
        pairs = []
        for p in range(GW // LANES):
            xp = xdt[:, l0 + p * LANES:l0 + (p + 1) * LANES]
            acc = None
            for hh in range(2):
                h = (l0 + p * LANES) // SSM_HEAD_DIM + hh
                ccol = cum[:, h * SSM_HEAD_DIM:h * SSM_HEAD_DIM + 1]
                diff = ccol - cumT[h:h + 1, :]
                dm = jnp.exp(jnp.where(tril, diff, -jnp.inf))
                s = (G * dm).astype(BF16)
                half = (lane >= hh * SSM_HEAD_DIM) & (lane < (hh + 1) * SSM_HEAD_DIM)
                part = _dot(s, jnp.where(half, xp, 0.0).astype(BF16))
                acc = part if acc is None else acc + part
            pairs.append(acc)
        y_groups.append(jnp.concatenate(pairs, axis=1) + inter)
    y = jnp.concatenate(y_groups, axis=1)

    y = (y + dsk_ref[...] * xs) * _silu(z)
    outs = []
    for g in range(SSM_GROUPS):
        seg = y[:, g * GW:(g + 1) * GW]
        ms = jnp.mean(seg * seg, axis=-1, keepdims=True)
        outs.append(seg * lax.rsqrt(ms + EPS) * ng_ref[:, g * GW:(g + 1) * GW])
    y_ref[0] = jnp.concatenate(outs, axis=1)

    @pl.when(c == nc - 1)
    def _():
        convn_ref[0] = xpad_ref[5 + last_valid:8 + last_valid, :]
        sn_ref[0] = st_ref[...]

    xpad_ref[0:8, :] = xpad_ref[Q:Q + 8, :]


def ssd_branch(zxd, dtT, conv0, s0, conv_w, conv_b, dt_bias, a_log, d_skip, norm_g, last_valid):
    b, L, W = zxd.shape
    Q = CHUNK
    nc = L // Q
    rep = lambda v: jnp.repeat(v.astype(F32), SSM_HEAD_DIM).reshape(1, SSM_INNER)
    col = lambda v: v.astype(F32).reshape(SSM_HEADS, 1)
    full = lambda shape: pl.BlockSpec(shape, lambda i, c: (0,) * len(shape))
    kern = functools.partial(_ssd_kernel, Q=Q, last_valid=last_valid, nc=nc)
    return pl.pallas_call(
        kern,
        grid=(b, nc),
        in_specs=[pl.BlockSpec((1, Q, W), lambda i, c: (i, c, 0)),
                  pl.BlockSpec((1, SSM_HEADS, Q), lambda i, c: (i, 0, c)),
                  pl.BlockSpec((1, CONV_WIDTH - 1, CONV_DIM), lambda i, c: (i, 0, 0)),
                  pl.BlockSpec((1, SSM_STATE, SSM_INNER), lambda i, c: (i, 0, 0)),
                  full((CONV_WIDTH, CONV_DIM)), full((1, CONV_DIM)),
                  full((1, SSM_INNER)), full((1, SSM_INNER)), full((1, SSM_INNER)), full((1, SSM_INNER)),
                  full((SSM_HEADS, 1)), full((SSM_HEADS, 1))],
        out_specs=[pl.BlockSpec((1, Q, SSM_INNER), lambda i, c: (i, c, 0)),
                   pl.BlockSpec((1, CONV_WIDTH - 1, CONV_DIM), lambda i, c: (i, 0, 0)),
                   pl.BlockSpec((1, SSM_STATE, SSM_INNER), lambda i, c: (i, 0, 0))],
        out_shape=[jax.ShapeDtypeStruct((b, L, SSM_INNER), F32),
                   jax.ShapeDtypeStruct((b, CONV_WIDTH - 1, CONV_DIM), F32),
                   jax.ShapeDtypeStruct((b, SSM_STATE, SSM_INNER), F32)],
        scratch_shapes=[pltpu.VMEM((Q + 8, CONV_DIM), F32), pltpu.VMEM((SSM_STATE, SSM_INNER), F32)],
        compiler_params=_cparams("parallel", "arbitrary"),
        name="ssd_branch",
    )(zxd, dtT, conv0, s0, conv_w.astype(F32), conv_b.reshape(1, CONV_DIM).astype(F32),
      rep(dt_bias), rep(a_log), rep(d_skip), norm_g.reshape(1, SSM_INNER).astype(F32),
      col(dt_bias), col(a_log))


def _ret_kernel(x_ref, cos_ref, sin_ref, r0_ref, y_ref, rn_ref, st_ref, *, Q, last_valid, nc):
    c = pl.program_id(1)

    @pl.when(c == 0)
    def _():
        st_ref[...] = r0_ref[0]

    blk = x_ref[0]
    cos2 = cos_ref[...]
    sin2 = sin_ref[...]
    ri = lax.broadcasted_iota(I32, (Q, Q), 0)
    ci = lax.broadcasted_iota(I32, (Q, Q), 1)
    dij = (ri - ci).astype(F32)
    row = lax.broadcasted_iota(I32, (Q, 1), 0)
    assert last_valid == Q or nc == 1
    lv = last_valid
    rowf = row.astype(F32)
    lvf = float(lv)
    KO = RET_HEADS * RET_DK
    outs = []
    for h in range(RET_HEADS):
        lg = math.log1p(-2.0 ** (-5.0 - h))
        qh = blk[:, h * RET_DK:(h + 1) * RET_DK]
        kh = blk[:, KO + h * RET_DK:KO + (h + 1) * RET_DK]
        vh = blk[:, 2 * KO + h * RET_DV:2 * KO + (h + 1) * RET_DV].astype(BF16)
        gh = blk[:, 2 * KO + RET_HEADS * RET_DV + h * RET_DV:2 * KO + RET_HEADS * RET_DV + (h + 1) * RET_DV]
        qr = qh * cos2 + pltpu.roll(qh, RET_DK // 2, 1) * sin2
        kr = (kh * cos2 + pltpu.roll(kh, RET_DK // 2, 1) * sin2) * (RET_DK ** -0.5)
        qb = qr.astype(BF16)
        dm = jnp.exp(jnp.where(ri >= ci, dij * lg, -jnp.inf))
        s = (_dot_nt(qb, kr.astype(BF16)) * dm).astype(BF16)
        st_h = st_ref[h]
        y = _dot(s, vh) + jnp.exp((rowf + 1.0) * lg) * _dot(qb, st_h.astype(BF16))
        ktail = jnp.where(row < lv, kr * jnp.exp((lvf - 1.0 - rowf) * lg), 0.0)
        st_ref[h] = st_h * math.exp(lvf * lg) + _dot_tn(ktail.astype(BF16), vh)
        ms = jnp.mean(y * y, axis=-1, keepdims=True)
        outs.append(y * lax.rsqrt(ms + EPS) * _silu(gh))
    y_ref[0] = jnp.concatenate(outs, axis=1)

    @pl.when(c == nc - 1)
    def _():
        rn_ref[0] = st_ref[...]


def ret_branch(x, cos2, sin2, r0, last_valid):
    b, L, W = x.shape
    Q = CHUNK
    nc = L // Q
    kern = functools.partial(_ret_kernel, Q=Q, last_valid=last_valid, nc=nc)
    return pl.pallas_call(
        kern,
        grid=(b, nc),
        in_specs=[pl.BlockSpec((1, Q, W), lambda i, c: (i, c, 0)),
                  pl.BlockSpec((Q, RET_DK), lambda i, c: (c, 0)),
                  pl.BlockSpec((Q, RET_DK), lambda i, c: (c, 0)),
                  pl.BlockSpec((1, RET_HEADS, RET_DK, RET_DV), lambda i, c: (i, 0, 0, 0))],
        out_specs=[pl.BlockSpec((1, Q, RET_HEADS * RET_DV), lambda i, c: (i, c, 0)),
                   pl.BlockSpec((1, RET_HEADS, RET_DK, RET_DV), lambda i, c: (i, 0, 0, 0))],
        out_shape=[jax.ShapeDtypeStruct((b, L, RET_HEADS * RET_DV), F32),
                   jax.ShapeDtypeStruct((b, RET_HEADS, RET_DK, RET_DV), F32)],
        scratch_shapes=[pltpu.VMEM((RET_HEADS, RET_DK, RET_DV), F32)],
        compiler_params=_cparams("parallel", "arbitrary"),
        name="ret_branch",
    )(x, cos2, sin2, r0)


def _kth_largest_key(count_ge, k, shape):
    def body(t, prefix):
        bit = lax.shift_left(jnp.int32(1), jnp.int32(31) - t)
        cand = prefix | bit
        cnt = count_ge(cand ^ jnp.int32(INT_MIN))
        return jnp.where(cnt >= k, cand, prefix)

    prefix = lax.fori_loop(0, 32, body, jnp.zeros(shape, I32))
    return prefix ^ jnp.int32(INT_MIN)


def _dsa_prompt_kernel(q_ref, iq_ref, iwt_ref, ik_ref, k_ref, vt_ref, o_ref, keys_ref, jb_ref,
                       m_ref, acc_ref, *, tq, ck, cka, L, k_sel):
    i = pl.program_id(1)
    G = ATT_HEADS // ATT_KV_HEADS
    PART = 4 * SUBLANES
    nk = lax.div((i + 1) * tq + (ck - 1), ck)
    qpos = i * tq + lax.broadcasted_iota(I32, (1, tq), 1)
    rowk = lax.broadcasted_iota(I32, (ck, 1), 0)
    iw = iwt_ref[0] * (IDX_HEADS ** -0.5 * IDX_DIM ** -0.5)

    def score_body(kc, carry):
        off = pl.multiple_of(kc * ck, ck)
        ikc = ik_ref[0, pl.ds(off, ck), :]
        acc = jnp.zeros((ck, tq), F32)
        for h4 in range(0, IDX_HEADS, G):
            d = _dot_nt(ikc, iq_ref[h4:h4 + G].reshape(G * tq, IDX_DIM))
            for j in range(G):
                acc = acc + jnp.maximum(d[:, j * tq:(j + 1) * tq], 0.0) * iw[h4 + j:h4 + j + 1, :]
        key = jnp.where(off + rowk <= qpos, _float_key(acc), jnp.int32(INT_MIN))
        keys_ref[pl.ds(off, ck), :] = key
        return carry

    lax.fori_loop(0, nk, score_body, 0)

    def count(pred):
        def body(kc, part):
            off = pl.multiple_of(kc * ck, ck)
            hit = jnp.where(pred(keys_ref[pl.ds(off, ck), :], off + rowk), 1.0, 0.0)
            return part + jnp.sum(hit.reshape(ck // PART, PART, tq), axis=0)
        part = lax.fori_loop(0, nk, body, jnp.zeros((PART, tq), F32))
        return jnp.sum(part, axis=0, keepdims=True)

    kf = float(k_sel)
    thr = _kth_largest_key(lambda cand: count(lambda kk, pos: kk >= cand), kf, (1, tq))
    thr = jnp.maximum(thr, jnp.int32(KEY_NEG_INF + 1))
    n_ge = count(lambda kk, pos: kk >= thr)
    jb_ref[...] = jnp.full((1, tq), L, I32)

    @pl.when(jnp.max(n_ge) > kf)
    def _():
        need = kf - count(lambda kk, pos: kk > thr)
        nb = max(1, (L - 1).bit_length())

        def body(t, pfx):
            cand = pfx | lax.shift_left(jnp.int32(1), jnp.int32(nb - 1) - t)
            cnt = count(lambda kk, pos: (kk == thr) & (pos < cand))
            return jnp.where(cnt < need, cand, pfx)

        pfx = lax.fori_loop(0, nb, body, jnp.zeros((1, tq), I32))
        jb_ref[...] = jnp.where(n_ge > kf, pfx + 1, L)

    jb = jb_ref[...]

    m_ref[...] = jnp.full(m_ref.shape, NEG_BIG, F32)
    acc_ref[...] = jnp.zeros(acc_ref.shape, F32)
    rowa = lax.broadcasted_iota(I32, (cka, 1), 0)

    def att_body(kc, carry):
        off = pl.multiple_of(kc * cka, cka)
        kk = keys_ref[pl.ds(off, cka), :]
        sel = (kk >= thr) & (kk < jnp.int32(KEY_POS_INF)) & ((kk > thr) | (off + rowa < jb))
        bias = jnp.where(sel, 0.0, -jnp.inf)
        bias = jnp.concatenate([bias] * ATT_HEADS, axis=1)
        s = jnp.concatenate(
            [_dot_nt(k_ref[0, g, pl.ds(off, cka), :], q_ref[g * G:(g + 1) * G].reshape(G * tq, ATT_HEAD_DIM))
             for g in range(ATT_KV_HEADS)], axis=1) + bias
        m = m_ref[...]
        smax = jnp.max(jnp.max(s.reshape(cka // PART, PART, ATT_HEADS * tq), axis=0), axis=0, keepdims=True)
        m_new = jnp.maximum(m, smax)
        p = jnp.exp(s - m_new).astype(BF16)
        pv = jnp.concatenate(
            [_dot(vt_ref[0, g, :, pl.ds(off, cka)], p[:, g * G * tq:(g + 1) * G * tq])
             for g in range(ATT_KV_HEADS)], axis=1)
        acc_ref[...] = jnp.exp(m - m_new) * acc_ref[...] + pv
        m_ref[...] = m_new
        return carry

    lax.fori_loop(0, lax.div((i + 1) * tq + (cka - 1), cka), att_body, 0)
    acc = acc_ref[...]
    o = acc[:ATT_HEAD_DIM] / acc[ATT_HEAD_DIM:ATT_HEAD_DIM + 1]
    for h in range(ATT_HEADS):
        o_ref[0, h] = o[:, h * tq:(h + 1) * tq]


V_ROWS = ATT_HEAD_DIM + 16


def dsa_prompt(qiq, iw_t, ik, k_hm, v_t):
    b, _, L, _ = k_hm.shape
    tq = min(128, L)
    nq = L // tq
    ck = min(512, L)
    cka = ck
    gq = (ATT_HEADS // ATT_KV_HEADS) * tq
    k_sel = max(1, min(TOPK_MAX, L // 4))
    kern = functools.partial(_dsa_prompt_kernel, tq=tq, ck=ck, cka=cka, L=L, k_sel=k_sel)
    return pl.pallas_call(
        kern,
        grid=(b, L // tq),
        in_specs=[pl.BlockSpec((ATT_HEADS, tq, ATT_HEAD_DIM), lambda bi, i: (0, bi * nq + i, 0)),
                  pl.BlockSpec((IDX_HEADS, tq, IDX_DIM), lambda bi, i: (1, bi * nq + i, 0)),
                  pl.BlockSpec((1, IDX_HEADS, tq), lambda bi, i: (bi, 0, i)),
                  pl.BlockSpec((1, L, IDX_DIM), lambda bi, i: (bi, 0, 0)),
                  pl.BlockSpec((1, ATT_KV_HEADS, L, ATT_HEAD_DIM), lambda bi, i: (bi, 0, 0, 0)),
                  pl.BlockSpec((1, ATT_KV_HEADS, V_ROWS, L), lambda bi, i: (bi, 0, 0, 0))],
        out_specs=pl.BlockSpec((1, ATT_HEADS, ATT_HEAD_DIM, tq), lambda bi, i: (bi, 0, 0, i)),
        out_shape=jax.ShapeDtypeStruct((b, ATT_HEADS, ATT_HEAD_DIM, L), F32),
        scratch_shapes=[pltpu.VMEM((L, tq), I32), pltpu.VMEM((1, tq), I32),
                        pltpu.VMEM((1, ATT_HEADS * tq), F32),
                        pltpu.VMEM((V_ROWS, ATT_HEADS * tq), F32)],
        compiler_params=_cparams("parallel", "arbitrary"),
        name="dsa_prompt",
    )(qiq, qiq, iw_t, ik, k_hm, v_t)


TQ8 = SUBLANES


def _sample_score_keys(d, iww):
    n = d.shape[1]
    r = jnp.maximum(d, 0.0) * (iww[:, 0:1] * (IDX_HEADS ** -0.5 * IDX_DIM ** -0.5))
    return _float_key(jnp.sum(r.reshape(TQ8, IDX_HEADS, n), axis=1))


def _dsa_sample_scores_kernel(pt_ref, *refs, n_pages):
    page_refs = refs[:n_pages]
    iq_ref, iww_ref, keys_ref = refs[n_pages:]
    ik_t = jnp.concatenate([r[0, 0] for r in page_refs], axis=1).astype(BF16)
    keys_ref[0] = _sample_score_keys(_dot(iq_ref[0], ik_t), iww_ref[0])


def _dsa_sample_thr_kernel(kp_ref, iq_ref, iww_ref, ikn_ref, thr_ref, jb_ref, kn_ref,
                           *, nb_seq, T, past, k_sel):
    R8 = nb_seq * TQ8
    kp = kp_ref[...].reshape(R8, past)
    rowi = lax.broadcasted_iota(I32, (TQ8, LANES), 0)
    col1 = lax.broadcasted_iota(I32, (TQ8, LANES), 1)
    kn = jnp.concatenate(
        [jnp.where((col1 <= rowi) & (col1 < T),
                   _sample_score_keys(_dot_nt(iq_ref[s], ikn_ref[s]), iww_ref[s]), jnp.int32(INT_MIN))
         for s in range(nb_seq)], axis=0)
    coln = lax.broadcasted_iota(I32, (R8, LANES), 1)
    colp = lax.broadcasted_iota(I32, (R8, past), 1)

    def count(pred):
        hit = jnp.where(pred(kp, colp), 1.0, 0.0)
        part = jnp.where(pred(kn, coln + past), 1.0, 0.0)
        for t in range(past // LANES):
            part = part + hit[:, t * LANES:(t + 1) * LANES]
        return jnp.sum(part, axis=1, keepdims=True)

    kf = float(k_sel)
    thr = _kth_largest_key(lambda cand: count(lambda kk, col: kk >= cand), kf, (R8, 1))
    thr = jnp.maximum(thr, jnp.int32(KEY_NEG_INF + 1))
    need = kf - count(lambda kk, col: kk > thr)
    nb = (past + LANES - 1).bit_length()

    def body(t, pfx):
        cand = pfx | lax.shift_left(jnp.int32(1), jnp.int32(nb - 1) - t)
        cnt = count(lambda kk, col: (kk == thr) & (col < cand))
        return jnp.where(cnt < need, cand, pfx)

    pfx = lax.fori_loop(0, nb, body, jnp.zeros((R8, 1), I32))
    thr_ref[...] = jnp.broadcast_to(thr, (R8, LANES)).reshape(nb_seq, TQ8, LANES)
    jb_ref[...] = jnp.broadcast_to(pfx + 1, (R8, LANES)).reshape(nb_seq, TQ8, LANES)
    kn_ref[...] = kn.reshape(nb_seq, TQ8, LANES)


def _dsa_sample_attn_kernel(pt_ref, *refs, n_pages, T, past):
    k_refs = refs[:n_pages]
    v_refs = refs[n_pages:2 * n_pages]
    (q_ref, kp_ref, thr_ref, jb_ref, kn_ref, knew_ref, vnew_ref,
     o_ref, m_ref, l_ref, acc_ref) = refs[2 * n_pages:]
    p = pl.program_id(1)
    R = T * ATT_HEADS

    @pl.when(p == 0)
    def _():
        m_ref[...] = jnp.full((R, 1), NEG_BIG, F32)
        l_ref[...] = jnp.zeros((R, 1), F32)
        acc_ref[...] = jnp.zeros((R, KV_WIDTH), F32)

    thr = thr_ref[0][0:T, 0:1]
    jb = jb_ref[0][0:T, 0:1]

    def step(kk, col0, s, pv):
        n = kk.shape[1]
        col = col0 + lax.broadcasted_iota(I32, (1, n), 1)
        sel = (kk >= thr) & (kk < jnp.int32(KEY_POS_INF)) & ((kk > thr) | (col < jb))
        bias = jnp.where(sel, 0.0, -jnp.inf)
        s = (s.reshape(T, ATT_HEADS, n) + bias[:, None, :]).reshape(R, n)
        m = m_ref[...]
        m_new = jnp.maximum(m, jnp.max(s, axis=1, keepdims=True))
        alpha = jnp.exp(m - m_new)
        pr = jnp.exp(s - m_new)
        l_ref[...] = alpha * l_ref[...] + jnp.sum(pr, axis=1, keepdims=True)
        acc_ref[...] = alpha * acc_ref[...] + pv(pr.astype(BF16))
        m_ref[...] = m_new

    k_t = jnp.concatenate([r[0, 0] for r in k_refs], axis=1).astype(BF16)
    v_t = jnp.concatenate([r[0, 0] for r in v_refs], axis=1).astype(BF16)
    step(kp_ref[0][0:T], p * (n_pages * PAGE_SIZE), _dot(q_ref[0], k_t), lambda pr: _dot_nt(pr, v_t))

    @pl.when(p == pl.num_programs(1) - 1)
    def _():
        step(kn_ref[0][0:T], past, _dot_nt(q_ref[0], knew_ref[0]), lambda pr: _dot(pr, vnew_ref[0]))
        o_ref[0] = acc_ref[...] / l_ref[...]


def _pages_per_step(npages, want):
    while npages % want:
        want //= 2
    return want


def dsa_sample(q, k, v, iq, iw, ik, ck_t, cv_t, cik_t, layer, page_table):
    b, T, _ = q.shape
    assert T <= TQ8
    npages = page_table.shape[1]
    past = npages * PAGE_SIZE
    k_sel = max(1, min(TOPK_MAX, (past + T) // 4))
    R = T * ATT_HEADS
    R8 = TQ8 * IDX_HEADS
    pad_to = lambda a, n: jnp.pad(a, ((0, 0), (0, n - a.shape[1]), (0, 0)))
    iq_rows = pad_to(iq.reshape(b, T * IDX_HEADS, IDX_DIM), R8).astype(BF16)
    iww = jnp.broadcast_to(pad_to(iw.reshape(b, T * IDX_HEADS, 1), R8), (b, R8, LANES)).astype(F32)
    ik_new, k_new, v_new = (pad_to(a, LANES).astype(BF16) for a in (ik, k, v))
    head_group = jnp.arange(ATT_HEADS) // (ATT_HEADS // ATT_KV_HEADS)
    onehot = (head_group[:, None] == jnp.arange(ATT_KV_HEADS)[None, :]).astype(F32)
    q_bd = (q.reshape(b, T, ATT_HEADS, 1, ATT_HEAD_DIM) * onehot[None, None, :, :, None])
    q_bd = q_bd.reshape(b, R, KV_WIDTH).astype(BF16)

    def page_spec(rows, per_step, j):
        return pl.BlockSpec((1, 1, rows, PAGE_SIZE),
                            lambda bi, p, pt: (layer, pt[bi, p * per_step + j], 0, 0))

    fixed = lambda shape: pl.BlockSpec((1,) + shape, lambda bi, p, pt: (bi, 0, 0))

    ps = _pages_per_step(npages, 32)
    keys_past = pl.pallas_call(
        functools.partial(_dsa_sample_scores_kernel, n_pages=ps),
        grid_spec=pltpu.PrefetchScalarGridSpec(
            num_scalar_prefetch=1, grid=(b, npages // ps),
            in_specs=[page_spec(IDX_DIM, ps, j) for j in range(ps)]
            + [fixed((R8, IDX_DIM)), fixed((R8, LANES))],
            out_specs=pl.BlockSpec((1, TQ8, ps * PAGE_SIZE), lambda bi, p, pt: (bi, 0, p))),
        out_shape=jax.ShapeDtypeStruct((b, TQ8, past), I32),
        compiler_params=_cparams("parallel", "arbitrary"),
        name="dsa_sample_scores",
    )(page_table, *([cik_t] * ps), iq_rows, iww)

    nb_seq = SUBLANES if b % SUBLANES == 0 else 1
    seq_spec = lambda r, w: pl.BlockSpec((nb_seq, r, w), lambda bi: (bi, 0, 0))
    thr, jb, keys_new = pl.pallas_call(
        functools.partial(_dsa_sample_thr_kernel, nb_seq=nb_seq, T=T, past=past, k_sel=k_sel),
        grid=(b // nb_seq,),
        in_specs=[seq_spec(TQ8, past), seq_spec(R8, IDX_DIM), seq_spec(R8, LANES),
                  seq_spec(LANES, IDX_DIM)],
        out_specs=[seq_spec(TQ8, LANES)] * 3,
        out_shape=[jax.ShapeDtypeStruct((b, TQ8, LANES), I32)] * 3,
        compiler_params=_cparams("parallel"),
        name="dsa_sample_threshold",
    )(keys_past, iq_rows, iww, ik_new)

    pa = _pages_per_step(npages, 16)
    o = pl.pallas_call(
        functools.partial(_dsa_sample_attn_kernel, n_pages=pa, T=T, past=past),
        grid_spec=pltpu.PrefetchScalarGridSpec(
            num_scalar_prefetch=1, grid=(b, npages // pa),
            in_specs=[page_spec(KV_WIDTH, pa, j) for j in range(pa)] * 2
            + [fixed((R, KV_WIDTH)),
               pl.BlockSpec((1, TQ8, pa * PAGE_SIZE), lambda bi, p, pt: (bi, 0, p)),
               fixed((TQ8, LANES)), fixed((TQ8, LANES)), fixed((TQ8, LANES)),
               fixed((LANES, KV_WIDTH)), fixed((LANES, KV_WIDTH))],
            out_specs=fixed((R, KV_WIDTH)),
            scratch_shapes=[pltpu.VMEM((R, 1), F32), pltpu.VMEM((R, 1), F32),
                            pltpu.VMEM((R, KV_WIDTH), F32)]),
        out_shape=jax.ShapeDtypeStruct((b, R, KV_WIDTH), F32),
        compiler_params=_cparams("parallel", "arbitrary"),
        name="dsa_sample_attention",
    )(page_table, *([ck_t] * pa), *([cv_t] * pa), q_bd, keys_past, thr, jb, keys_new, k_new, v_new)

    o = o.reshape(b, T, ATT_HEADS, ATT_KV_HEADS, ATT_HEAD_DIM)
    o = jnp.sum(o * onehot[None, None, :, :, None], axis=3)
    return o.reshape(b, T, ATT_HEADS * ATT_HEAD_DIM)


def _merge_kernel(ys_ref, ya_ref, yr_ref, g_ref, h_ref, wb_ref, wo_ref, o_ref):
    acc = None
    for n, y_ref in enumerate((ys_ref, ya_ref, yr_ref)):
        pr = _dot(y_ref[...].astype(BF16), wb_ref[n])
        t = jax.nn.sigmoid(g_ref[:, n * D_MODEL:(n + 1) * D_MODEL]) * pr
        acc = t if acc is None else acc + t
    o_ref[...] = h_ref[...] + _dot(acc.astype(BF16), wo_ref[...])


def merge_branches(ys, ya, yr, gates, h, wb, wo):
    M = h.shape[0]
    tm = _pick(M, (256, 128))
    rows = lambda w: pl.BlockSpec((tm, w), lambda i: (i, 0))
    return pl.pallas_call(
        _merge_kernel,
        grid=(M // tm,),
        in_specs=[rows(D_MODEL), rows(D_MODEL), rows(D_MODEL), rows(N_BRANCH * D_MODEL), rows(D_MODEL),
                  pl.BlockSpec((N_BRANCH, D_MODEL, D_MODEL), lambda i: (0, 0, 0)),
                  pl.BlockSpec((D_MODEL, D_MODEL), lambda i: (0, 0))],
        out_specs=rows(D_MODEL),
        out_shape=jax.ShapeDtypeStruct((M, D_MODEL), F32),
        compiler_params=_cparams("parallel"),
        name="merge_branches",
    )(ys, ya, yr, gates, h, wb, wo)


def _ffn_kernel(x_ref, g_ref, wg_ref, wu_ref, wd_ref, o_ref, xn_ref, acc_ref):
    j = pl.program_id(1)

    @pl.when(j == 0)
    def _():
        x = x_ref[...]
        ms = jnp.mean(x * x, axis=-1, keepdims=True)
        xn_ref[...] = (x * lax.rsqrt(ms + EPS) * g_ref[...]).astype(BF16)
        acc_ref[...] = jnp.zeros_like(acc_ref)

    xn = xn_ref[...]
    a = _silu(_dot(xn, wg_ref[...])) * _dot(xn, wu_ref[...])
    acc_ref[...] += _dot(a.astype(BF16), wd_ref[...])

    @pl.when(j == pl.num_programs(1) - 1)
    def _():
        o_ref[...] = x_ref[...] + acc_ref[...]


def ffn_dense(x, g, wg, wu, wd):
    M = x.shape[0]
    tm = _pick(M, (1024, 512, 256, 128))
    tf = 512
    return pl.pallas_call(
        _ffn_kernel,
        grid=(M // tm, D_FF // tf),
        in_specs=[pl.BlockSpec((tm, D_MODEL), lambda i, j: (i, 0)),
                  pl.BlockSpec((1, D_MODEL), lambda i, j: (0, 0)),
                  pl.BlockSpec((D_MODEL, tf), lambda i, j: (0, j)),
                  pl.BlockSpec((D_MODEL, tf), lambda i, j: (0, j)),
                  pl.BlockSpec((tf, D_MODEL), lambda i, j: (j, 0))],
        out_specs=pl.BlockSpec((tm, D_MODEL), lambda i, j: (i, 0)),
        out_shape=jax.ShapeDtypeStruct((M, D_MODEL), F32),
        scratch_shapes=[pltpu.VMEM((tm, D_MODEL), BF16), pltpu.VMEM((tm, D_MODEL), F32)],
        compiler_params=_cparams("parallel", "arbitrary"),
        name="ffn_dense",
    )(x, g.reshape(1, D_MODEL), wg, wu, wd)


def _router_kernel(x_ref, g_ref, wr_ref, gate_ref):
    x = x_ref[...]
    ms = jnp.mean(x * x, axis=-1, keepdims=True)
    xn = x * lax.rsqrt(ms + EPS) * g_ref[...]
    logits = jnp.dot(xn, wr_ref[...], preferred_element_type=F32, precision=lax.Precision.HIGHEST)
    lane = lax.broadcasted_iota(I32, logits.shape, 1)
    logits = jnp.where(lane < N_EXPERTS, logits, -jnp.inf)
    v1 = jnp.max(logits, axis=1, keepdims=True)
    i1 = jnp.min(jnp.where(logits == v1, lane, LANES), axis=1, keepdims=True)
    rest = jnp.where(lane == i1, -jnp.inf, logits)
    v2 = jnp.max(rest, axis=1, keepdims=True)
    i2 = jnp.min(jnp.where(rest == v2, lane, LANES), axis=1, keepdims=True)
    e2 = jnp.exp(v2 - v1)
    w1 = 1.0 / (1.0 + e2)
    w2 = e2 / (1.0 + e2)
    gate_ref[...] = jnp.where(lane == i1, w1, 0.0) + jnp.where(lane == i2, w2, 0.0)


def moe_router(x, g, wr):
    M = x.shape[0]
    tm = _pick(M, (512, 256, 128))
    wr_pad = jnp.pad(wr.astype(F32), ((0, 0), (0, LANES - N_EXPERTS)))
    return pl.pallas_call(
        _router_kernel,
        grid=(M // tm,),
        in_specs=[pl.BlockSpec((tm, D_MODEL), lambda i: (i, 0)),
                  pl.BlockSpec((1, D_MODEL), lambda i: (0, 0)),
                  pl.BlockSpec((D_MODEL, LANES), lambda i: (0, 0))],
        out_specs=pl.BlockSpec((tm, LANES), lambda i: (i, 0)),
        out_shape=jax.ShapeDtypeStruct((M, LANES), F32),
        compiler_params=_cparams("parallel"),
        name="moe_router",
    )(x, g.reshape(1, D_MODEL), wr_pad)


MOE_ROWS = 320


def _moe_kernel(x_ref, g_ref, gate_ref, gatet_ref, wg_ref, wu_ref, wd_ref, o_ref,
                xn_ref, rkc_ref, rkr_ref, xe_ref, ye_ref, cnt_ref, *, tm, R, nch):
    e = pl.program_id(1)
    j = pl.program_id(2)
    last_j = pl.num_programs(2) - 1

    @pl.when((e == 0) & (j == 0))
    def _():
        x = x_ref[...]
        ms = jnp.mean(x * x, axis=-1, keepdims=True)
        xn_ref[...] = (x * lax.rsqrt(ms + EPS) * g_ref[...]).astype(BF16)
        o_ref[...] = x
        ti = lax.broadcasted_iota(I32, (tm, tm), 0)
        tj = lax.broadcasted_iota(I32, (tm, tm), 1)
        flags = jnp.where(gate_ref[...] > 0.0, 1.0, 0.0).astype(BF16)
        flags_t = jnp.where(gatet_ref[...] > 0.0, 1.0, 0.0).astype(BF16)
        rkc_ref[...] = _dot(jnp.where(ti > tj, 1.0, 0.0).astype(BF16), flags)
        rkr_ref[...] = _dot(flags_t, jnp.where(ti < tj, 1.0, 0.0).astype(BF16))

    @pl.when(j == 0)
    def _():
        rank_row = rkr_ref[pl.ds(e, 1), :]
        flag_row = gatet_ref[pl.ds(e, 1), :] > 0.0
        cnt = jnp.sum(jnp.where(flag_row, 1, 0))
        cnt_ref[0] = cnt
        ye_ref[...] = jnp.zeros_like(ye_ref)
        for c in range(nch):
            @pl.when(c * R < cnt)
            def _():
                slot = (c * R + lax.broadcasted_iota(I32, (R, 1), 0)).astype(F32)
                sel = jnp.where((rank_row == slot) & flag_row, 1.0, 0.0).astype(BF16)
                xe_ref[c * R:(c + 1) * R, :] = _dot(sel, xn_ref[...]).astype(BF16)

    cnt = cnt_ref[0]
    for c in range(nch):
        @pl.when(c * R < cnt)
        def _():
            xe = xe_ref[c * R:(c + 1) * R, :]
            a = _silu(_dot(xe, wg_ref[0])) * _dot(xe, wu_ref[0])
            ye_ref[c * R:(c + 1) * R, :] += _dot(a.astype(BF16), wd_ref[0])

    @pl.when(j == last_j)
    def _():
        lane = lax.broadcasted_iota(I32, (tm, LANES), 1)
        gcol = jnp.sum(jnp.where(lane == e, gate_ref[...], 0.0), axis=1, keepdims=True)
        rcol = jnp.sum(jnp.where(lane == e, rkc_ref[...], 0.0), axis=1, keepdims=True)
        for c in range(nch):
            @pl.when(c * R < cnt)
            def _():
                slot = (c * R + lax.broadcasted_iota(I32, (1, R), 1)).astype(F32)
                sel_t = jnp.where((rcol == slot) & (gcol > 0.0), 1.0, 0.0).astype(BF16)
                ye = ye_ref[c * R:(c + 1) * R, :]
                hi = ye.astype(BF16)
                lo = (ye - hi.astype(F32)).astype(BF16)
                o_ref[...] += gcol * (_dot(sel_t, hi) + _dot(sel_t, lo))


def moe_ffn(x, g, gate, wg, wu, wd):
    M = x.shape[0]
    tm = _pick(M, (1024, 512, 256, 128))
    tf = 896
    R = min(MOE_ROWS, tm)
    nch = -(-tm // R)
    kern = functools.partial(_moe_kernel, tm=tm, R=R, nch=nch)
    return pl.pallas_call(
        kern,
        grid=(M // tm, N_EXPERTS, D_FF // tf),
        in_specs=[pl.BlockSpec((tm, D_MODEL), lambda i, e, j: (i, 0)),
                  pl.BlockSpec((1, D_MODEL), lambda i, e, j: (0, 0)),
                  pl.BlockSpec((tm, LANES), lambda i, e, j: (i, 0)),
                  pl.BlockSpec((LANES, tm), lambda i, e, j: (0, i)),
                  pl.BlockSpec((1, D_MODEL, tf), lambda i, e, j: (e, 0, j)),
                  pl.BlockSpec((1, D_MODEL, tf), lambda i, e, j: (e, 0, j)),
                  pl.BlockSpec((1, tf, D_MODEL), lambda i, e, j: (e, j, 0))],
        out_specs=pl.BlockSpec((tm, D_MODEL), lambda i, e, j: (i, 0)),
        out_shape=jax.ShapeDtypeStruct((M, D_MODEL), F32),
        scratch_shapes=[pltpu.VMEM((tm, D_MODEL), BF16),
                        pltpu.VMEM((tm, LANES), F32), pltpu.VMEM((LANES, tm), F32),
                        pltpu.VMEM((nch * R, D_MODEL), BF16), pltpu.VMEM((nch * R, D_MODEL), F32),
                        pltpu.SMEM((1,), I32)],
        compiler_params=pltpu.CompilerParams(
            dimension_semantics=("parallel", "arbitrary", "arbitrary"), vmem_limit_bytes=MOE_VMEM_LIMIT),
        name="moe_ffn",
    )(x, g.reshape(1, D_MODEL), gate, gate.T, wg, wu, wd)


def _rmsnorm_kernel(x_ref, g_ref, o_ref):
    x = x_ref[...]
    ms = jnp.mean(x * x, axis=-1, keepdims=True)
    o_ref[...] = x * lax.rsqrt(ms + EPS) * g_ref[...]


def rmsnorm(x, g):
    M = x.shape[0]
    tm = _pick(M, (1024, 512, 256, 128))
    return pl.pallas_call(
        _rmsnorm_kernel,
        grid=(M // tm,),
        in_specs=[pl.BlockSpec((tm, D_MODEL), lambda i: (i, 0)),
                  pl.BlockSpec((1, D_MODEL), lambda i: (0, 0))],
        out_specs=pl.BlockSpec((tm, D_MODEL), lambda i: (i, 0)),
        out_shape=jax.ShapeDtypeStruct((M, D_MODEL), F32),
        compiler_params=_cparams("parallel"),
        name="final_rmsnorm",
    )(x, g.reshape(1, D_MODEL))


def _split_w_in(w_in):
    offs = np.cumsum((0,) + IN_SPLITS)
    seg = {n: w_in[:, offs[i]:offs[i + 1]] for i, n in enumerate(
        ("z", "xbc", "dt", "q", "k", "v", "iq", "iw", "ik", "rq", "rk", "rv", "rg", "gates"))}
    cat = lambda *names: jnp.concatenate([seg[n] if isinstance(n, str) else n for n in names], axis=1)
    dt_wide = jnp.repeat(seg["dt"], SSM_HEAD_DIM, axis=1)
    small = cat("dt", "iw", jnp.zeros((D_MODEL, LANES - 2 * IDX_HEADS - IDX_DIM), w_in.dtype), "ik")
    q_scaled = seg["q"] * (ATT_HEAD_DIM ** -0.5)
    groups = dict(ssm=cat("z", "xbc", dt_wide), qiq=cat(q_scaled, "iq"), kvs=cat("k", "v", small),
                  ret=cat("rq", "rk", "rv", "rg"), gate=seg["gates"])
    return {n: w.astype(BF16) for n, w in groups.items()}


def _rope_tables(pos):
    half = RET_DK // 2
    inv = ROPE_BASE ** (-jnp.arange(half, dtype=F32) / half)
    ang = pos.astype(F32)[:, None] * inv[None, :]
    cos, sin = jnp.cos(ang), jnp.sin(ang)
    return jnp.concatenate([cos, cos], axis=1), jnp.concatenate([-sin, sin], axis=1)


def _pad_rows(a, L):
    return jnp.pad(a, ((0, 0), (0, L - a.shape[1]), (0, 0)))


def _mixers(h, pos0, conv0, ssm0, ret0, attend, lw):
    b, L, _ = h.shape
    hf = h.reshape(b * L, D_MODEL)
    proj = {n: rms_matmul(hf, lw["norm_g"], w).reshape(b, L, -1)
            for n, w in lw["w_in"].items() if n != "qiq"}
    qiq = rms_matmul_heads(hf, lw["norm_g"], lw["w_in"]["qiq"], ATT_HEAD_DIM)
    k = proj["kvs"][..., :KV_WIDTH]
    v = proj["kvs"][..., KV_WIDTH:2 * KV_WIDTH]
    small = proj["kvs"][..., 2 * KV_WIDTH:]
    ik = small[..., LANES - IDX_DIM:]

    Lp = -(-L // CHUNK) * CHUNK
    last_valid = L - (Lp - CHUNK)
    dtT = jnp.swapaxes(_pad_rows(small[..., :SSM_HEADS], Lp), 1, 2)
    s0 = jnp.transpose(ssm0, (0, 2, 1, 3)).reshape(b, SSM_STATE, SSM_INNER)
    ys, conv_new, s_new = ssd_branch(_pad_rows(proj["ssm"], Lp), dtT, conv0, s0, lw["conv_w"], lw["conv_b"],
                                     lw["dt_bias"], lw["a_log"], lw["d_skip"], lw["ssm_norm_g"], last_valid)
    ssm_new = jnp.transpose(s_new.reshape(b, SSM_STATE, SSM_HEADS, SSM_HEAD_DIM), (0, 2, 1, 3))

    cos2, sin2 = _rope_tables(pos0 + jnp.arange(Lp))
    yr, ret_new = ret_branch(_pad_rows(proj["ret"], Lp), cos2, sin2, ret0, last_valid)

    ya = attend(qiq, k, v, small, ik, b, L)

    ys = ys[:, :L].reshape(b * L, -1)
    yr = yr[:, :L].reshape(b * L, -1)
    out = merge_branches(ys, ya.reshape(b * L, -1), yr, proj["gate"].reshape(b * L, -1), hf,
                         lw["w_branch"], lw["w_out"])
    return out.reshape(b, L, D_MODEL), (k.reshape(b, L, ATT_KV_HEADS, ATT_HEAD_DIM),
                                        v.reshape(b, L, ATT_KV_HEADS, ATT_HEAD_DIM),
                                        ik, ssm_new, conv_new, ret_new)


def _attend_prompt(qiq, k, v, small, ik, b, L):
    k_hm = jnp.transpose(k.reshape(b, L, ATT_KV_HEADS, ATT_HEAD_DIM), (0, 2, 1, 3)).astype(BF16)
    v_t = jnp.transpose(v.reshape(b, L, ATT_KV_HEADS, ATT_HEAD_DIM), (0, 2, 3, 1)).astype(BF16)
    v_t = jnp.concatenate([v_t, jnp.ones((b, ATT_KV_HEADS, V_ROWS - ATT_HEAD_DIM, L), BF16)], axis=2)
    iw_t = jnp.swapaxes(small[..., IDX_HEADS:2 * IDX_HEADS], 1, 2)
    o = dsa_prompt(qiq, iw_t, ik.astype(BF16), k_hm, v_t)
    return jnp.transpose(o, (0, 3, 1, 2)).reshape(b, L, ATT_HEADS * ATT_HEAD_DIM)


def _attend_sample(qiq, k, v, small, ik, b, L, *, ck_t, cv_t, cik_t, layer, page_table):
    rows = lambda a: jnp.transpose(a, (1, 0, 2)).reshape(b, L, ATT_HEADS * ATT_HEAD_DIM)
    iw = small[..., IDX_HEADS:2 * IDX_HEADS]
    return dsa_sample(rows(qiq[:ATT_HEADS]), k, v, rows(qiq[ATT_HEADS:]), iw, ik,
                      ck_t, cv_t, cik_t, layer, page_table)


def _channel_mixer(h, l, p):
    b, L, _ = h.shape
    hf = h.reshape(b * L, D_MODEL)
    j = l // 2
    if l % 2 == 0:
        out = ffn_dense(hf, p["norm_ffn_g"][l], p["w_ffn_gate"][j].astype(BF16),
                        p["w_ffn_up"][j].astype(BF16), p["w_ffn_down"][j].astype(BF16))
    else:
        gate = moe_router(hf, p["norm_ffn_g"][l], p["w_router"][j])
        out = moe_ffn(hf, p["norm_ffn_g"][l], gate, p["w_moe_gate"][j].astype(BF16),
                      p["w_moe_up"][j].astype(BF16), p["w_moe_down"][j].astype(BF16))
    return out.reshape(b, L, D_MODEL)


def kernel(x_prompt, x_sample, cache_k, cache_v, cache_idx_k, state_ssm, state_conv, state_ret,
           page_table, norm_mix_g, w_in, conv_w, conv_b, dt_bias, a_log, d_skip, ssm_norm_g,
           w_branch, w_out, norm_ffn_g, w_ffn_gate, w_ffn_up, w_ffn_down, w_router,
           w_moe_gate, w_moe_up, w_moe_down, final_norm_g):
    depth = w_in.shape[0]
    past = page_table.shape[1] * PAGE_SIZE
    bp = x_prompt.shape[0]
    p = dict(norm_ffn_g=norm_ffn_g, w_ffn_gate=w_ffn_gate, w_ffn_up=w_ffn_up, w_ffn_down=w_ffn_down,
             w_router=w_router, w_moe_gate=w_moe_gate, w_moe_up=w_moe_up, w_moe_down=w_moe_down)
    n_phys = cache_k.shape[1]
    ck_t = jnp.transpose(cache_k, (0, 1, 3, 4, 2)).reshape(depth, n_phys, KV_WIDTH, PAGE_SIZE)
    cv_t = jnp.transpose(cache_v, (0, 1, 3, 4, 2)).reshape(depth, n_phys, KV_WIDTH, PAGE_SIZE)
    cik_t = jnp.transpose(cache_idx_k, (0, 1, 3, 2))
    hp, hs = x_prompt, x_sample
    st_p, st_s = [], []
    for l in range(depth):
        lw = dict(norm_g=norm_mix_g[l], w_in=_split_w_in(w_in[l]), conv_w=conv_w[l], conv_b=conv_b[l],
                  dt_bias=dt_bias[l], a_log=a_log[l], d_skip=d_skip[l], ssm_norm_g=ssm_norm_g[l],
                  w_branch=w_branch[l].astype(BF16), w_out=w_out[l].astype(BF16))
        hp, sp = _mixers(hp, 0,
                         jnp.zeros((bp, CONV_WIDTH - 1, CONV_DIM), F32),
                         jnp.zeros((bp, SSM_HEADS, SSM_STATE, SSM_HEAD_DIM), F32),
                         jnp.zeros((bp, RET_HEADS, RET_DK, RET_DV), F32),
                         _attend_prompt, lw)
        attend_s = functools.partial(_attend_sample, ck_t=ck_t, cv_t=cv_t, cik_t=cik_t, layer=l,
                                     page_table=page_table)
        hs, ss = _mixers(hs, past, state_conv[l], state_ssm[l], state_ret[l], attend_s, lw)
        hp = _channel_mixer(hp, l, p)
        hs = _channel_mixer(hs, l, p)
        st_p.append(sp)
        st_s.append(ss)
    y_prompt = rmsnorm(hp.reshape(-1, D_MODEL), final_norm_g).reshape(hp.shape)
    y_sample = rmsnorm(hs.reshape(-1, D_MODEL), final_norm_g).reshape(hs.shape)
    stack = lambda sts, i: jnp.stack([s[i] for s in sts])
    return (y_prompt, y_sample,
            stack(st_p, 0), stack(st_p, 1), stack(st_p, 2), stack(st_p, 3), stack(st_p, 4), stack(st_p, 5),
            stack(st_s, 0), stack(st_s, 1), stack(st_s, 2), stack(st_s, 3), stack(st_s, 4), stack(st_s, 5))
```

```python
import functools
import math

import jax
import jax.numpy as jnp
import numpy as np
from jax import lax
from jax.experimental import pallas as pl
from jax.experimental.pallas import tpu as pltpu

F32 = jnp.float32
BF16 = jnp.bfloat16
I32 = jnp.int32

D_MODEL = 1024
PAGE_SIZE = 128
SSM_HEADS = 16
SSM_HEAD_DIM = 64
SSM_INNER = 1024
SSM_GROUPS = 2
SSM_STATE = 128
CONV_WIDTH = 4
CONV_DIM = 1536
ATT_HEADS = 16
ATT_KV_HEADS = 4
ATT_HEAD_DIM = 64
KV_WIDTH = 256
IDX_HEADS = 16
IDX_DIM = 64
TOPK_MAX = 256
RET_HEADS = 4
RET_DK = 128
RET_DV = 256
ROPE_BASE = 10000.0
N_BRANCH = 3
D_FF = 3584
N_EXPERTS = 8
EPS = 1e-6
IN_SPLITS = (1024, 1536, 16, 1024, 256, 256, 1024, 16, 64, 512, 512, 1024, 1024, 3072)

LANES = 128
SUBLANES = 8
VMEM_LIMIT = 48 * 1024 * 1024
MOE_VMEM_LIMIT = 56 * 1024 * 1024

CHUNK = 128
INT_MIN = -(2 ** 31)
KEY_POS_INF = 0x7F800000
KEY_NEG_INF = -2139095041
NEG_BIG = -1e30


def _cparams(*sem):
    return pltpu.CompilerParams(dimension_semantics=sem, vmem_limit_bytes=VMEM_LIMIT)


def _dot(a, b):
    return jnp.dot(a, b, preferred_element_type=F32)


def _dot_nt(a, b):
    return lax.dot_general(a, b, (((1,), (1,)), ((), ())), preferred_element_type=F32)


def _dot_tn(a, b):
    return lax.dot_general(a, b, (((0,), (0,)), ((), ())), preferred_element_type=F32)


def _split3(x):
    hi = x.astype(BF16)
    r = x - hi.astype(F32)
    mid = r.astype(BF16)
    lo = (r - mid.astype(F32)).astype(BF16)
    return hi, mid, lo


def _silu(x):
    return x * jax.nn.sigmoid(x)


def _softplus(x):
    return jnp.maximum(x, 0.0) + jnp.log1p(jnp.exp(-jnp.abs(x)))


def _float_key(x):
    x = jnp.where(x == 0.0, 0.0, x)
    b = lax.bitcast_convert_type(x, I32)
    return jnp.where(b >= 0, b, b ^ jnp.int32(0x7FFFFFFF))


def _rms_matmul_kernel(x_ref, g_ref, w_ref, o_ref):
    x = x_ref[...]
    ms = jnp.mean(x * x, axis=-1, keepdims=True)
    xn = (x * lax.rsqrt(ms + EPS) * g_ref[...]).astype(BF16)
    o_ref[...] = _dot(xn, w_ref[...]).astype(o_ref.dtype)


def _rms_matmul_heads_kernel(x_ref, g_ref, w_ref, o_ref, *, n_heads, head_dim):
    x = x_ref[...]
    ms = jnp.mean(x * x, axis=-1, keepdims=True)
    xn = (x * lax.rsqrt(ms + EPS) * g_ref[...]).astype(BF16)
    res = _dot(xn, w_ref[...])
    for h in range(n_heads):
        o_ref[h] = res[:, h * head_dim:(h + 1) * head_dim].astype(o_ref.dtype)


def _pick(n, prefs):
    for p in prefs:
        if n % p == 0:
            return p
    return n


def rms_matmul(x, g, w, out_dtype=F32):
    M, K = x.shape
    N = w.shape[1]
    tm = _pick(M, (512, 256, 128))
    return pl.pallas_call(
        _rms_matmul_kernel,
        grid=(M // tm,),
        in_specs=[pl.BlockSpec((tm, K), lambda i: (i, 0)),
                  pl.BlockSpec((1, K), lambda i: (0, 0)),
                  pl.BlockSpec((K, N), lambda i: (0, 0))],
        out_specs=pl.BlockSpec((tm, N), lambda i: (i, 0)),
        out_shape=jax.ShapeDtypeStruct((M, N), out_dtype),
        compiler_params=_cparams("parallel"),
        name="rms_matmul",
    )(x, g.reshape(1, K), w)


def rms_matmul_heads(x, g, w, head_dim):
    M, K = x.shape
    N = w.shape[1]
    n_heads = N // head_dim
    tm = _pick(M, (512, 256, 128))
    kern = functools.partial(_rms_matmul_heads_kernel, n_heads=n_heads, head_dim=head_dim)
    return pl.pallas_call(
        kern,
        grid=(M // tm,),
        in_specs=[pl.BlockSpec((tm, K), lambda i: (i, 0)),
                  pl.BlockSpec((1, K), lambda i: (0, 0)),
                  pl.BlockSpec((K, N), lambda i: (0, 0))],
        out_specs=pl.BlockSpec((n_heads, tm, head_dim), lambda i: (0, i, 0)),
        out_shape=jax.ShapeDtypeStruct((n_heads, M, head_dim), BF16),
        compiler_params=_cparams("parallel"),
        name="rms_matmul_heads",
    )(x, g.reshape(1, K), w)


def _ssd_kernel(zxd_ref, dtT_ref, conv0_ref, s0_ref, cw_ref, cb_ref, dtb_ref, alog_ref, dsk_ref,
                ng_ref, dtbT_ref, alogT_ref, y_ref, convn_ref, sn_ref, xpad_ref, st_ref,
                *, Q, last_valid, nc):
    c = pl.program_id(1)
    GW = SSM_INNER // SSM_GROUPS

    @pl.when(c == 0)
    def _():
        xpad_ref[0:8, :] = jnp.zeros((8, CONV_DIM), F32)
        xpad_ref[5:8, :] = conv0_ref[0]
        st_ref[...] = s0_ref[0]

    blk = zxd_ref[0]
    z = blk[:, :SSM_INNER]
    xbc = blk[:, SSM_INNER:SSM_INNER + CONV_DIM]
    dtr = blk[:, SSM_INNER + CONV_DIM:]
    xpad_ref[8:8 + Q, :] = xbc
    cw = cw_ref[...]
    conv = (xpad_ref[5:5 + Q, :] * cw[0:1] + xpad_ref[6:6 + Q, :] * cw[1:2]
            + xpad_ref[7:7 + Q, :] * cw[2:3] + xbc * cw[3:4]) + cb_ref[...]
    xc = _silu(conv)
    xs = xc[:, :SSM_INNER]
    Bm = xc[:, SSM_INNER:SSM_INNER + SSM_GROUPS * SSM_STATE]
    Cm = xc[:, SSM_INNER + SSM_GROUPS * SSM_STATE:]

    row = lax.broadcasted_iota(I32, (Q, 1), 0)
    colq = lax.broadcasted_iota(I32, (1, Q), 1)
    assert last_valid == Q or nc == 1
    lv = last_valid
    valid = row < lv
    tril = (lax.broadcasted_iota(I32, (Q, Q), 0) >= lax.broadcasted_iota(I32, (Q, Q), 1))
    tril_b = jnp.where(tril, 1.0, 0.0).astype(BF16)
    triu_b = jnp.where(lax.broadcasted_iota(I32, (Q, Q), 0) <= lax.broadcasted_iota(I32, (Q, Q), 1),
                       1.0, 0.0).astype(BF16)

    nega = -jnp.exp(alog_ref[...])
    dt = _softplus(dtr + dtb_ref[...])
    la = jnp.where(valid, dt * nega, 0.0)
    cum = sum(_dot(tril_b, p) for p in _split3(la))
    laT = jnp.where(colq < lv, _softplus(dtT_ref[0] + dtbT_ref[...]) * (-jnp.exp(alogT_ref[...])), 0.0)
    cumT = sum(_dot(p, triu_b) for p in _split3(laT))

    ecum = jnp.exp(cum)
    cl = cum[lv - 1:lv, :]
    xdt = xs * dt
    xtail = jnp.where(valid, xdt * jnp.exp(cl - cum), 0.0)
    cdecay = jnp.exp(cl)

    lane = lax.broadcasted_iota(I32, (1, LANES), 1)
    y_groups = []
    for g in range(SSM_GROUPS):
        l0 = g * GW
        Cg = Cm[:, g * SSM_STATE:(g + 1) * SSM_STATE].astype(BF16)
        Bg = Bm[:, g * SSM_STATE:(g + 1) * SSM_STATE].astype(BF16)
        G = _dot_nt(Cg, Bg)
        st_g = st_ref[:, l0:l0 + GW]
        inter = _dot(Cg, st_g.astype(BF16)) * ecum[:, l0:l0 + GW]
        local = _dot_tn(Bg, xtail[:, l0:l0 + GW].astype(BF16))
        st_ref[:, l0:l0 + GW] = st_g * cdecay[:, l0:l0 + GW] + local
        pairs = []
        for p in range(GW // LANES):
            xp = xdt[:, l0 + p * LANES:l0 + (p + 1) * LANES]
            acc = None
            for hh in range(2):
                h = (l0 + p * LANES) // SSM_HEAD_DIM + hh
                ccol = cum[:, h * SSM_HEAD_DIM:h * SSM_HEAD_DIM + 1]
                diff = ccol - cumT[h:h + 1, :]
                dm = jnp.exp(jnp.where(tril, diff, -jnp.inf))
                s = (G * dm).astype(BF16)
                half = (lane >= hh * SSM_HEAD_DIM) & (lane < (hh + 1) * SSM_HEAD_DIM)
                part = _dot(s, jnp.where(half, xp, 0.0).astype(BF16))
                acc = part if acc is None else acc + part
            pairs.append(acc)
        y_groups.append(jnp.concatenate(pairs, axis=1) + inter)
    y = jnp.concatenate(y_groups, axis=1)

    y = (y + dsk_ref[...] * xs) * _silu(z)
    outs = []
    for g in range(SSM_GROUPS):
        seg = y[:, g * GW:(g + 1) * GW]
        ms = jnp.mean(seg * seg, axis=-1, keepdims=True)
        outs.append(seg * lax.rsqrt(ms + EPS) * ng_ref[:, g * GW:(g + 1) * GW])
    y_ref[0] = jnp.concatenate(outs, axis=1)

    @pl.when(c == nc - 1)
    def _():
        convn_ref[0] = xpad_ref[5 + last_valid:8 + last_valid, :]
        sn_ref[0] = st_ref[...]

    xpad_ref[0:8, :] = xpad_ref[Q:Q + 8, :]


def ssd_branch(zxd, dtT, conv0, s0, conv_w, conv_b, dt_bias, a_log, d_skip, norm_g, last_valid):
    b, L, W = zxd.shape
    Q = CHUNK
    nc = L // Q
    rep = lambda v: jnp.repeat(v.astype(F32), SSM_HEAD_DIM).reshape(1, SSM_INNER)
    col = lambda v: v.astype(F32).reshape(SSM_HEADS, 1)
    full = lambda shape: pl.BlockSpec(shape, lambda i, c: (0,) * len(shape))
    kern = functools.partial(_ssd_kernel, Q=Q, last_valid=last_valid, nc=nc)
    return pl.pallas_call(
        kern,
        grid=(b, nc),
        in_specs=[pl.BlockSpec((1, Q, W), lambda i, c: (i, c, 0)),
                  pl.BlockSpec((1, SSM_HEADS, Q), lambda i, c: (i, 0, c)),
                  pl.BlockSpec((1, CONV_WIDTH - 1, CONV_DIM), lambda i, c: (i, 0, 0)),
                  pl.BlockSpec((1, SSM_STATE, SSM_INNER), lambda i, c: (i, 0, 0)),
                  full((CONV_WIDTH, CONV_DIM)), full((1, CONV_DIM)),
                  full((1, SSM_INNER)), full((1, SSM_INNER)), full((1, SSM_INNER)), full((1, SSM_INNER)),
                  full((SSM_HEADS, 1)), full((SSM_HEADS, 1))],
        out_specs=[pl.BlockSpec((1, Q, SSM_INNER), lambda i, c: (i, c, 0)),
                   pl.BlockSpec((1, CONV_WIDTH - 1, CONV_DIM), lambda i, c: (i, 0, 0)),
                   pl.BlockSpec((1, SSM_STATE, SSM_INNER), lambda i, c: (i, 0, 0))],
        out_shape=[jax.ShapeDtypeStruct((b, L, SSM_INNER), F32),
                   jax.ShapeDtypeStruct((b, CONV_WIDTH - 1, CONV_DIM), F32),
                   jax.ShapeDtypeStruct((b, SSM_STATE, SSM_INNER), F32)],
        scratch_shapes=[pltpu.VMEM((Q + 8, CONV_DIM), F32), pltpu.VMEM((SSM_STATE, SSM_INNER), F32)],
        compiler_params=_cparams("parallel", "arbitrary"),
        name="ssd_branch",
    )(zxd, dtT, conv0, s0, conv_w.astype(F32), conv_b.reshape(1, CONV_DIM).astype(F32),
      rep(dt_bias), rep(a_log), rep(d_skip), norm_g.reshape(1, SSM_INNER).astype(F32),
      col(dt_bias), col(a_log))


def _ret_kernel(x_ref, cos_ref, sin_ref, r0_ref, y_ref, rn_ref, st_ref, *, Q, last_valid, nc):
    c = pl.program_id(1)

    @pl.when(c == 0)
    def _():
        st_ref[...] = r0_ref[0]

    blk = x_ref[0]
    cos2 = cos_ref[...]
    sin2 = sin_ref[...]
    ri = lax.broadcasted_iota(I32, (Q, Q), 0)
    ci = lax.broadcasted_iota(I32, (Q, Q), 1)
    dij = (ri - ci).astype(F32)
    row = lax.broadcasted_iota(I32, (Q, 1), 0)
    assert last_valid == Q or nc == 1
    lv = last_valid
    rowf = row.astype(F32)
    lvf = float(lv)
    KO = RET_HEADS * RET_DK
    outs = []
    for h in range(RET_HEADS):
        lg = math.log1p(-2.0 ** (-5.0 - h))
        qh = blk[:, h * RET_DK:(h + 1) * RET_DK]
        kh = blk[:, KO + h * RET_DK:KO + (h + 1) * RET_DK]
        vh = blk[:, 2 * KO + h * RET_DV:2 * KO + (h + 1) * RET_DV].astype(BF16)
        gh = blk[:, 2 * KO + RET_HEADS * RET_DV + h * RET_DV:2 * KO + RET_HEADS * RET_DV + (h + 1) * RET_DV]
        qr = qh * cos2 + pltpu.roll(qh, RET_DK // 2, 1) * sin2
        kr = (kh * cos2 + pltpu.roll(kh, RET_DK // 2, 1) * sin2) * (RET_DK ** -0.5)
        qb = qr.astype(BF16)
        dm = jnp.exp(jnp.where(ri >= ci, dij * lg, -jnp.inf))
        s = (_dot_nt(qb, kr.astype(BF16)) * dm).astype(BF16)
        st_h = st_ref[h]
        y = _dot(s, vh) + jnp.exp((rowf + 1.0) * lg) * _dot(qb, st_h.astype(BF16))
        ktail = jnp.where(row < lv, kr * jnp.exp((lvf - 1.0 - rowf) * lg), 0.0)
        st_ref[h] = st_h * math.exp(lvf * lg) + _dot_tn(ktail.astype(BF16), vh)
        ms = jnp.mean(y * y, axis=-1, keepdims=True)
        outs.append(y * lax.rsqrt(ms + EPS) * _silu(gh))
    y_ref[0] = jnp.concatenate(outs, axis=1)

    @pl.when(c == nc - 1)
    def _():
        rn_ref[0] = st_ref[...]


def ret_branch(x, cos2, sin2, r0, last_valid):
    b, L, W = x.shape
    Q = CHUNK
    nc = L // Q
    kern = functools.partial(_ret_kernel, Q=Q, last_valid=last_valid, nc=nc)
    return pl.pallas_call(
        kern,
        grid=(b, nc),
        in_specs=[pl.BlockSpec((1, Q, W), lambda i, c: (i, c, 0)),
                  pl.BlockSpec((Q, RET_DK), lambda i, c: (c, 0)),
                  pl.BlockSpec((Q, RET_DK), lambda i, c: (c, 0)),
                  pl.BlockSpec((1, RET_HEADS, RET_DK, RET_DV), lambda i, c: (i, 0, 0, 0))],
        out_specs=[pl.BlockSpec((1, Q, RET_HEADS * RET_DV), lambda i, c: (i, c, 0)),
                   pl.BlockSpec((1, RET_HEADS, RET_DK, RET_DV), lambda i, c: (i, 0, 0, 0))],
        out_shape=[jax.ShapeDtypeStruct((b, L, RET_HEADS * RET_DV), F32),
                   jax.ShapeDtypeStruct((b, RET_HEADS, RET_DK, RET_DV), F32)],
        scratch_shapes=[pltpu.VMEM((RET_HEADS, RET_DK, RET_DV), F32)],
        compiler_params=_cparams("parallel", "arbitrary"),
        name="ret_branch",
    )(x, cos2, sin2, r0)


def _kth_largest_key(count_ge, k, shape):
    def body(t, prefix):
        bit = lax.shift_left(jnp.int32(1), jnp.int32(31) - t)
        cand = prefix | bit
        cnt = count_ge(cand ^ jnp.int32(INT_MIN))
        return jnp.where(cnt >= k, cand, prefix)

    prefix = lax.fori_loop(0, 32, body, jnp.zeros(shape, I32))
    return prefix ^ jnp.int32(INT_MIN)


def _dsa_prompt_kernel(q_ref, iq_ref, iwt_ref, ik_ref, k_ref, vt_ref, o_ref, keys_ref, jb_ref,
                       m_ref, acc_ref, *, tq, ck, cka, L, k_sel):
    i = pl.program_id(1)
    G = ATT_HEADS // ATT_KV_HEADS
    PART = 4 * SUBLANES
    nk = lax.div((i + 1) * tq + (ck - 1), ck)
    qpos = i * tq + lax.broadcasted_iota(I32, (1, tq), 1)
    rowk = lax.broadcasted_iota(I32, (ck, 1), 0)
    iw = iwt_ref[0] * (IDX_HEADS ** -0.5 * IDX_DIM ** -0.5)

    def score_body(kc, carry):
        off = pl.multiple_of(kc * ck, ck)
        ikc = ik_ref[0, pl.ds(off, ck), :]
        acc = jnp.zeros((ck, tq), F32)
        for h4 in range(0, IDX_HEADS, G):
            d = _dot_nt(ikc, iq_ref[h4:h4 + G].reshape(G * tq, IDX_DIM))
            for j in range(G):
                acc = acc + jnp.maximum(d[:, j * tq:(j + 1) * tq], 0.0) * iw[h4 + j:h4 + j + 1, :]
        key = jnp.where(off + rowk <= qpos, _float_key(acc), jnp.int32(INT_MIN))
        keys_ref[pl.ds(off, ck), :] = key
        return carry

    lax.fori_loop(0, nk, score_body, 0)

    def count(pred):
        def body(kc, part):
            off = pl.multiple_of(kc * ck, ck)
            hit = jnp.where(pred(keys_ref[pl.ds(off, ck), :], off + rowk), 1.0, 0.0)
            return part + jnp.sum(hit.reshape(ck // PART, PART, tq), axis=0)
        part = lax.fori_loop(0, nk, body, jnp.zeros((PART, tq), F32))
        return jnp.sum(part, axis=0, keepdims=True)

    kf = float(k_sel)
    thr = _kth_largest_key(lambda cand: count(lambda kk, pos: kk >= cand), kf, (1, tq))
    thr = jnp.maximum(thr, jnp.int32(KEY_NEG_INF + 1))
    n_ge = count(lambda kk, pos: kk >= thr)
    jb_ref[...] = jnp.full((1, tq), L, I32)

    @pl.when(jnp.max(n_ge) > kf)
    def _():
        need = kf - count(lambda kk, pos: kk > thr)
        nb = max(1, (L - 1).bit_length())

        def body(t, pfx):
            cand = pfx | lax.shift_left(jnp.int32(1), jnp.int32(nb - 1) - t)
            cnt = count(lambda kk, pos: (kk == thr) & (pos < cand))
            return jnp.where(cnt < need, cand, pfx)

        pfx = lax.fori_loop(0, nb, body, jnp.zeros((1, tq), I32))
        jb_ref[...] = jnp.where(n_ge > kf, pfx + 1, L)

    jb = jb_ref[...]

    m_ref[...] = jnp.full(m_ref.shape, NEG_BIG, F32)
    acc_ref[...] = jnp.zeros(acc_ref.shape, F32)
    rowa = lax.broadcasted_iota(I32, (cka, 1), 0)

    def att_body(kc, carry):
        off = pl.multiple_of(kc * cka, cka)
        kk = keys_ref[pl.ds(off, cka), :]
        sel = (kk >= thr) & (kk < jnp.int32(KEY_POS_INF)) & ((kk > thr) | (off + rowa < jb))
        bias = jnp.where(sel, 0.0, -jnp.inf)
        bias = jnp.concatenate([bias] * ATT_HEADS, axis=1)
        s = jnp.concatenate(
            [_dot_nt(k_ref[0, g, pl.ds(off, cka), :], q_ref[g * G:(g + 1) * G].reshape(G * tq, ATT_HEAD_DIM))
             for g in range(ATT_KV_HEADS)], axis=1) + bias
        m = m_ref[...]
        smax = jnp.max(jnp.max(s.reshape(cka // PART, PART, ATT_HEADS * tq), axis=0), axis=0, keepdims=True)
        m_new = jnp.maximum(m, smax)
        p = jnp.exp(s - m_new).astype(BF16)
        pv = jnp.concatenate(
            [_dot(vt_ref[0, g, :, pl.ds(off, cka)], p[:, g * G * tq:(g + 1) * G * tq])
             for g in range(ATT_KV_HEADS)], axis=1)
        acc_ref[...] = jnp.exp(m - m_new) * acc_ref[...] + pv
        m_ref[...] = m_new
        return carry

    lax.fori_loop(0, lax.div((i + 1) * tq + (cka - 1), cka), att_body, 0)
    acc = acc_ref[...]
    o = acc[:ATT_HEAD_DIM] / acc[ATT_HEAD_DIM:ATT_HEAD_DIM + 1]
    for hp in range(ATT_HEADS // 2):
        pair = jnp.concatenate([o[:, (2 * hp) * tq:(2 * hp + 1) * tq],
                                o[:, (2 * hp + 1) * tq:(2 * hp + 2) * tq]], axis=0)
        o_ref[0, :, hp * LANES:(hp + 1) * LANES] = pair.T


V_ROWS = ATT_HEAD_DIM + 16


def dsa_prompt(qiq, iw_t, ik, k_hm, v_t):
    b, _, L, _ = k_hm.shape
    tq = min(128, L)
    nq = L // tq
    ck = min(512, L)
    cka = ck
    gq = (ATT_HEADS // ATT_KV_HEADS) * tq
    k_sel = max(1, min(TOPK_MAX, L // 4))
    kern = functools.partial(_dsa_prompt_kernel, tq=tq, ck=ck, cka=cka, L=L, k_sel=k_sel)
    return pl.pallas_call(
        kern,
        grid=(b, L // tq),
        in_specs=[pl.BlockSpec((ATT_HEADS, tq, ATT_HEAD_DIM), lambda bi, i: (0, bi * nq + i, 0)),
                  pl.BlockSpec((IDX_HEADS, tq, IDX_DIM), lambda bi, i: (1, bi * nq + i, 0)),
                  pl.BlockSpec((1, IDX_HEADS, tq), lambda bi, i: (bi, 0, i)),
                  pl.BlockSpec((1, L, IDX_DIM), lambda bi, i: (bi, 0, 0)),
                  pl.BlockSpec((1, ATT_KV_HEADS, L, ATT_HEAD_DIM), lambda bi, i: (bi, 0, 0, 0)),
                  pl.BlockSpec((1, ATT_KV_HEADS, V_ROWS, L), lambda bi, i: (bi, 0, 0, 0))],
        out_specs=pl.BlockSpec((1, tq, ATT_HEADS * ATT_HEAD_DIM), lambda bi, i: (bi, i, 0)),
        out_shape=jax.ShapeDtypeStruct((b, L, ATT_HEADS * ATT_HEAD_DIM), F32),
        scratch_shapes=[pltpu.VMEM((L, tq), I32), pltpu.VMEM((1, tq), I32),
                        pltpu.VMEM((1, ATT_HEADS * tq), F32),
                        pltpu.VMEM((V_ROWS, ATT_HEADS * tq), F32)],
        compiler_params=_cparams("parallel", "arbitrary"),
        name="dsa_prompt",
    )(qiq, qiq, iw_t, ik, k_hm, v_t)


TQ8 = SUBLANES


def _sample_score_keys(d, iww):
    n = d.shape[1]
    r = jnp.maximum(d, 0.0) * (iww[:, 0:1] * (IDX_HEADS ** -0.5 * IDX_DIM ** -0.5))
    return _float_key(jnp.sum(r.reshape(TQ8, IDX_HEADS, n), axis=1))


def _dsa_sample_scores_kernel(pt_ref, *refs, n_pages):
    page_refs = refs[:n_pages]
    iq_ref, iww_ref, keys_ref = refs[n_pages:]
    ik_t = jnp.concatenate([r[0, 0] for r in page_refs], axis=1).astype(BF16)
    keys_ref[0] = _sample_score_keys(_dot(iq_ref[0], ik_t), iww_ref[0])


def _dsa_sample_thr_kernel(kp_ref, iq_ref, iww_ref, ikn_ref, thr_ref, jb_ref, kn_ref,
                           *, nb_seq, T, past, k_sel):
    R8 = nb_seq * TQ8
    kp = kp_ref[...].reshape(R8, past)
    rowi = lax.broadcasted_iota(I32, (TQ8, LANES), 0)
    col1 = lax.broadcasted_iota(I32, (TQ8, LANES), 1)
    kn = jnp.concatenate(
        [jnp.where((col1 <= rowi) & (col1 < T),
                   _sample_score_keys(_dot_nt(iq_ref[s], ikn_ref[s]), iww_ref[s]), jnp.int32(INT_MIN))
         for s in range(nb_seq)], axis=0)
    coln = lax.broadcasted_iota(I32, (R8, LANES), 1)
    colp = lax.broadcasted_iota(I32, (R8, past), 1)

    def count(pred):
        hit = jnp.where(pred(kp, colp), 1.0, 0.0)
        part = jnp.where(pred(kn, coln + past), 1.0, 0.0)
        for t in range(past // LANES):
            part = part + hit[:, t * LANES:(t + 1) * LANES]
        return jnp.sum(part, axis=1, keepdims=True)

    kf = float(k_sel)
    thr = _kth_largest_key(lambda cand: count(lambda kk, col: kk >= cand), kf, (R8, 1))
    thr = jnp.maximum(thr, jnp.int32(KEY_NEG_INF + 1))
    need = kf - count(lambda kk, col: kk > thr)
    nb = (past + LANES - 1).bit_length()

    def body(t, pfx):
        cand = pfx | lax.shift_left(jnp.int32(1), jnp.int32(nb - 1) - t)
        cnt = count(lambda kk, col: (kk == thr) & (col < cand))
        return jnp.where(cnt < need, cand, pfx)

    pfx = lax.fori_loop(0, nb, body, jnp.zeros((R8, 1), I32))
    thr_ref[...] = jnp.broadcast_to(thr, (R8, LANES)).reshape(nb_seq, TQ8, LANES)
    jb_ref[...] = jnp.broadcast_to(pfx + 1, (R8, LANES)).reshape(nb_seq, TQ8, LANES)
    kn_ref[...] = kn.reshape(nb_seq, TQ8, LANES)


def _dsa_sample_attn_kernel(pt_ref, *refs, n_pages, T, past):
    k_refs = refs[:n_pages]
    v_refs = refs[n_pages:2 * n_pages]
    (q_ref, kp_ref, thr_ref, jb_ref, kn_ref, knew_ref, vnew_ref,
     o_ref, m_ref, l_ref, acc_ref) = refs[2 * n_pages:]
    p = pl.program_id(1)
    R = T * ATT_HEADS

    @pl.when(p == 0)
    def _():
        m_ref[...] = jnp.full((R, 1), NEG_BIG, F32)
        l_ref[...] = jnp.zeros((R, 1), F32)
        acc_ref[...] = jnp.zeros((R, KV_WIDTH), F32)

    thr = thr_ref[0][0:T, 0:1]
    jb = jb_ref[0][0:T, 0:1]

    def step(kk, col0, s, pv):
        n = kk.shape[1]
        col = col0 + lax.broadcasted_iota(I32, (1, n), 1)
        sel = (kk >= thr) & (kk < jnp.int32(KEY_POS_INF)) & ((kk > thr) | (col < jb))
        bias = jnp.where(sel, 0.0, -jnp.inf)
        s = (s.reshape(T, ATT_HEADS, n) + bias[:, None, :]).reshape(R, n)
        m = m_ref[...]
        m_new = jnp.maximum(m, jnp.max(s, axis=1, keepdims=True))
        alpha = jnp.exp(m - m_new)
        pr = jnp.exp(s - m_new)
        l_ref[...] = alpha * l_ref[...] + jnp.sum(pr, axis=1, keepdims=True)
        acc_ref[...] = alpha * acc_ref[...] + pv(pr.astype(BF16))
        m_ref[...] = m_new

    k_t = jnp.concatenate([r[0, 0] for r in k_refs], axis=1).astype(BF16)
    v_t = jnp.concatenate([r[0, 0] for r in v_refs], axis=1).astype(BF16)
    step(kp_ref[0][0:T], p * (n_pages * PAGE_SIZE), _dot(q_ref[0], k_t), lambda pr: _dot_nt(pr, v_t))

    @pl.when(p == pl.num_programs(1) - 1)
    def _():
        step(kn_ref[0][0:T], past, _dot_nt(q_ref[0], knew_ref[0]), lambda pr: _dot(pr, vnew_ref[0]))
        o_ref[0] = acc_ref[...] / l_ref[...]


def _pages_per_step(npages, want):
    while npages % want:
        want //= 2
    return want


def dsa_sample(q, k, v, iq, iw, ik, ck_t, cv_t, cik_t, layer, page_table):
    b, T, _ = q.shape
    assert T <= TQ8
    npages = page_table.shape[1]
    past = npages * PAGE_SIZE
    k_sel = max(1, min(TOPK_MAX, (past + T) // 4))
    R = T * ATT_HEADS
    R8 = TQ8 * IDX_HEADS
    pad_to = lambda a, n: jnp.pad(a, ((0, 0), (0, n - a.shape[1]), (0, 0)))
    iq_rows = pad_to(iq.reshape(b, T * IDX_HEADS, IDX_DIM), R8).astype(BF16)
    iww = jnp.broadcast_to(pad_to(iw.reshape(b, T * IDX_HEADS, 1), R8), (b, R8, LANES)).astype(F32)
    ik_new, k_new, v_new = (pad_to(a, LANES).astype(BF16) for a in (ik, k, v))
    head_group = jnp.arange(ATT_HEADS) // (ATT_HEADS // ATT_KV_HEADS)
    onehot = (head_group[:, None] == jnp.arange(ATT_KV_HEADS)[None, :]).astype(F32)
    q_bd = (q.reshape(b, T, ATT_HEADS, 1, ATT_HEAD_DIM) * onehot[None, None, :, :, None])
    q_bd = q_bd.reshape(b, R, KV_WIDTH).astype(BF16)

    def page_spec(rows, per_step, j):
        return pl.BlockSpec((1, 1, rows, PAGE_SIZE),
                            lambda bi, p, pt: (layer, pt[bi, p * per_step + j], 0, 0))

    fixed = lambda shape: pl.BlockSpec((1,) + shape, lambda bi, p, pt: (bi, 0, 0))

    ps = _pages_per_step(npages, 32)
    keys_past = pl.pallas_call(
        functools.partial(_dsa_sample_scores_kernel, n_pages=ps),
        grid_spec=pltpu.PrefetchScalarGridSpec(
            num_scalar_prefetch=1, grid=(b, npages // ps),
            in_specs=[page_spec(IDX_DIM, ps, j) for j in range(ps)]
            + [fixed((R8, IDX_DIM)), fixed((R8, LANES))],
            out_specs=pl.BlockSpec((1, TQ8, ps * PAGE_SIZE), lambda bi, p, pt: (bi, 0, p))),
        out_shape=jax.ShapeDtypeStruct((b, TQ8, past), I32),
        compiler_params=_cparams("parallel", "arbitrary"),
        name="dsa_sample_scores",
    )(page_table, *([cik_t] * ps), iq_rows, iww)

    nb_seq = SUBLANES if b % SUBLANES == 0 else 1
    seq_spec = lambda r, w: pl.BlockSpec((nb_seq, r, w), lambda bi: (bi, 0, 0))
    thr, jb, keys_new = pl.pallas_call(
        functools.partial(_dsa_sample_thr_kernel, nb_seq=nb_seq, T=T, past=past, k_sel=k_sel),
        grid=(b // nb_seq,),
        in_specs=[seq_spec(TQ8, past), seq_spec(R8, IDX_DIM), seq_spec(R8, LANES),
                  seq_spec(LANES, IDX_DIM)],
        out_specs=[seq_spec(TQ8, LANES)] * 3,
        out_shape=[jax.ShapeDtypeStruct((b, TQ8, LANES), I32)] * 3,
        compiler_params=_cparams("parallel"),
        name="dsa_sample_threshold",
    )(keys_past, iq_rows, iww, ik_new)

    pa = _pages_per_step(npages, 16)
    o = pl.pallas_call(
        functools.partial(_dsa_sample_attn_kernel, n_pages=pa, T=T, past=past),
        grid_spec=pltpu.PrefetchScalarGridSpec(
            num_scalar_prefetch=1, grid=(b, npages // pa),
            in_specs=[page_spec(KV_WIDTH, pa, j) for j in range(pa)] * 2
            + [fixed((R, KV_WIDTH)),
               pl.BlockSpec((1, TQ8, pa * PAGE_SIZE), lambda bi, p, pt: (bi, 0, p)),
               fixed((TQ8, LANES)), fixed((TQ8, LANES)), fixed((TQ8, LANES)),
               fixed((LANES, KV_WIDTH)), fixed((LANES, KV_WIDTH))],
            out_specs=fixed((R, KV_WIDTH)),
            scratch_shapes=[pltpu.VMEM((R, 1), F32), pltpu.VMEM((R, 1), F32),
                            pltpu.VMEM((R, KV_WIDTH), F32)]),
        out_shape=jax.ShapeDtypeStruct((b, R, KV_WIDTH), F32),
        compiler_params=_cparams("parallel", "arbitrary"),
        name="dsa_sample_attention",
    )(page_table, *([ck_t] * pa), *([cv_t] * pa), q_bd, keys_past, thr, jb, keys_new, k_new, v_new)

    o = o.reshape(b, T, ATT_HEADS, ATT_KV_HEADS, ATT_HEAD_DIM)
    o = jnp.sum(o * onehot[None, None, :, :, None], axis=3)
    return o.reshape(b, T, ATT_HEADS * ATT_HEAD_DIM)


def _merge_kernel(ys_ref, ya_ref, yr_ref, g_ref, h_ref, wb_ref, wo_ref, o_ref):
    acc = None
    for n, y_ref in enumerate((ys_ref, ya_ref, yr_ref)):
        pr = _dot(y_ref[...].astype(BF16), wb_ref[n])
        t = jax.nn.sigmoid(g_ref[:, n * D_MODEL:(n + 1) * D_MODEL]) * pr
        acc = t if acc is None else acc + t
    o_ref[...] = h_ref[...] + _dot(acc.astype(BF16), wo_ref[...])


def merge_branches(ys, ya, yr, gates, h, wb, wo):
    M = h.shape[0]
    tm = _pick(M, (256, 128))
    rows = lambda w: pl.BlockSpec((tm, w), lambda i: (i, 0))
    return pl.pallas_call(
        _merge_kernel,
        grid=(M // tm,),
        in_specs=[rows(D_MODEL), rows(D_MODEL), rows(D_MODEL), rows(N_BRANCH * D_MODEL), rows(D_MODEL),
                  pl.BlockSpec((N_BRANCH, D_MODEL, D_MODEL), lambda i: (0, 0, 0)),
                  pl.BlockSpec((D_MODEL, D_MODEL), lambda i: (0, 0))],
        out_specs=rows(D_MODEL),
        out_shape=jax.ShapeDtypeStruct((M, D_MODEL), F32),
        compiler_params=_cparams("parallel"),
        name="merge_branches",
    )(ys, ya, yr, gates, h, wb, wo)


def _ffn_kernel(x_ref, g_ref, wg_ref, wu_ref, wd_ref, o_ref, xn_ref, acc_ref):
    j = pl.program_id(1)

    @pl.when(j == 0)
    def _():
        x = x_ref[...]
        ms = jnp.mean(x * x, axis=-1, keepdims=True)
        xn_ref[...] = (x * lax.rsqrt(ms + EPS) * g_ref[...]).astype(BF16)
        acc_ref[...] = jnp.zeros_like(acc_ref)

    xn = xn_ref[...]
    a = _silu(_dot(xn, wg_ref[...])) * _dot(xn, wu_ref[...])
    acc_ref[...] += _dot(a.astype(BF16), wd_ref[...])

    @pl.when(j == pl.num_programs(1) - 1)
    def _():
        o_ref[...] = x_ref[...] + acc_ref[...]


def ffn_dense(x, g, wg, wu, wd):
    M = x.shape[0]
    tm = _pick(M, (1024, 512, 256, 128))
    tf = 512
    return pl.pallas_call(
        _ffn_kernel,
        grid=(M // tm, D_FF // tf),
        in_specs=[pl.BlockSpec((tm, D_MODEL), lambda i, j: (i, 0)),
                  pl.BlockSpec((1, D_MODEL), lambda i, j: (0, 0)),
                  pl.BlockSpec((D_MODEL, tf), lambda i, j: (0, j)),
                  pl.BlockSpec((D_MODEL, tf), lambda i, j: (0, j)),
                  pl.BlockSpec((tf, D_MODEL), lambda i, j: (j, 0))],
        out_specs=pl.BlockSpec((tm, D_MODEL), lambda i, j: (i, 0)),
        out_shape=jax.ShapeDtypeStruct((M, D_MODEL), F32),
        scratch_shapes=[pltpu.VMEM((tm, D_MODEL), BF16), pltpu.VMEM((tm, D_MODEL), F32)],
        compiler_params=_cparams("parallel", "arbitrary"),
        name="ffn_dense",
    )(x, g.reshape(1, D_MODEL), wg, wu, wd)


def _router_kernel(x_ref, g_ref, wr_ref, gate_ref):
    x = x_ref[...]
    ms = jnp.mean(x * x, axis=-1, keepdims=True)
    xn = x * lax.rsqrt(ms + EPS) * g_ref[...]
    logits = jnp.dot(xn, wr_ref[...], preferred_element_type=F32, precision=lax.Precision.HIGHEST)
    lane = lax.broadcasted_iota(I32, logits.shape, 1)
    logits = jnp.where(lane < N_EXPERTS, logits, -jnp.inf)
    v1 = jnp.max(logits, axis=1, keepdims=True)
    i1 = jnp.min(jnp.where(logits == v1, lane, LANES), axis=1, keepdims=True)
    rest = jnp.where(lane == i1, -jnp.inf, logits)
    v2 = jnp.max(rest, axis=1, keepdims=True)
    i2 = jnp.min(jnp.where(rest == v2, lane, LANES), axis=1, keepdims=True)
    e2 = jnp.exp(v2 - v1)
    w1 = 1.0 / (1.0 + e2)
    w2 = e2 / (1.0 + e2)
    gate_ref[...] = jnp.where(lane == i1, w1, 0.0) + jnp.where(lane == i2, w2, 0.0)


def moe_router(x, g, wr):
    M = x.shape[0]
    tm = _pick(M, (512, 256, 128))
    wr_pad = jnp.pad(wr.astype(F32), ((0, 0), (0, LANES - N_EXPERTS)))
    return pl.pallas_call(
        _router_kernel,
        grid=(M // tm,),
        in_specs=[pl.BlockSpec((tm, D_MODEL), lambda i: (i, 0)),
                  pl.BlockSpec((1, D_MODEL), lambda i: (0, 0)),
                  pl.BlockSpec((D_MODEL, LANES), lambda i: (0, 0))],
        out_specs=pl.BlockSpec((tm, LANES), lambda i: (i, 0)),
        out_shape=jax.ShapeDtypeStruct((M, LANES), F32),
        compiler_params=_cparams("parallel"),
        name="moe_router",
    )(x, g.reshape(1, D_MODEL), wr_pad)


MOE_BLOCK_ROWS = (192, 256, 320, 384, 512)


def _moe_kernel(x_ref, g_ref, gate_ref, gatet_ref, wg_ref, wu_ref, wd_ref, o_ref,
                xn_ref, rkc_ref, rkr_ref, xe_ref, ye_ref, cnt_ref, *, tm, sizes):
    e = pl.program_id(1)
    j = pl.program_id(2)
    last_j = pl.num_programs(2) - 1

    def for_block_size(cnt, body):
        lo = 0
        for s in sizes:
            pl.when((cnt > lo) & (cnt <= s))(functools.partial(body, s))
            lo = s

    @pl.when((e == 0) & (j == 0))
    def _():
        x = x_ref[...]
        ms = jnp.mean(x * x, axis=-1, keepdims=True)
        xn_ref[...] = (x * lax.rsqrt(ms + EPS) * g_ref[...]).astype(BF16)
        o_ref[...] = x
        ti = lax.broadcasted_iota(I32, (tm, tm), 0)
        tj = lax.broadcasted_iota(I32, (tm, tm), 1)
        flags = jnp.where(gate_ref[...] > 0.0, 1.0, 0.0).astype(BF16)
        flags_t = jnp.where(gatet_ref[...] > 0.0, 1.0, 0.0).astype(BF16)
        rkc_ref[...] = _dot(jnp.where(ti > tj, 1.0, 0.0).astype(BF16), flags)
        rkr_ref[...] = _dot(flags_t, jnp.where(ti < tj, 1.0, 0.0).astype(BF16))

    @pl.when(j == 0)
    def _():
        rank_row = rkr_ref[pl.ds(e, 1), :]
        flag_row = gatet_ref[pl.ds(e, 1), :] > 0.0
        cnt = jnp.sum(jnp.where(flag_row, 1, 0))
        cnt_ref[0] = cnt

        def gather(s):
            slot = lax.broadcasted_iota(I32, (s, 1), 0).astype(F32)
            sel = jnp.where((rank_row == slot) & flag_row, 1.0, 0.0).astype(BF16)
            xe_ref[0:s, :] = _dot(sel, xn_ref[...]).astype(BF16)
            ye_ref[0:s, :] = jnp.zeros((s, D_MODEL), F32)

        for_block_size(cnt, gather)

    cnt = cnt_ref[0]

    def expert(s):
        xe = xe_ref[0:s, :]
        a = _silu(_dot(xe, wg_ref[0])) * _dot(xe, wu_ref[0])
        ye_ref[0:s, :] += _dot(a.astype(BF16), wd_ref[0])

    for_block_size(cnt, expert)

    @pl.when(j == last_j)
    def _():
        lane = lax.broadcasted_iota(I32, (tm, LANES), 1)
        gcol = jnp.sum(jnp.where(lane == e, gate_ref[...], 0.0), axis=1, keepdims=True)
        rcol = jnp.sum(jnp.where(lane == e, rkc_ref[...], 0.0), axis=1, keepdims=True)

        def scatter(s):
            slot = lax.broadcasted_iota(I32, (1, s), 1).astype(F32)
            sel_t = jnp.where((rcol == slot) & (gcol > 0.0), 1.0, 0.0).astype(BF16)
            ye = ye_ref[0:s, :]
            hi = ye.astype(BF16)
            lo = (ye - hi.astype(F32)).astype(BF16)
            o_ref[...] += gcol * (_dot(sel_t, hi) + _dot(sel_t, lo))

        for_block_size(cnt, scatter)


def moe_ffn(x, g, gate, wg, wu, wd):
    M = x.shape[0]
    tm = _pick(M, (1024, 512, 256, 128))
    tf = 896
    sizes = tuple(s for s in MOE_BLOCK_ROWS if s < tm) + (tm,)
    kern = functools.partial(_moe_kernel, tm=tm, sizes=sizes)
    return pl.pallas_call(
        kern,
        grid=(M // tm, N_EXPERTS, D_FF // tf),
        in_specs=[pl.BlockSpec((tm, D_MODEL), lambda i, e, j: (i, 0)),
                  pl.BlockSpec((1, D_MODEL), lambda i, e, j: (0, 0)),
                  pl.BlockSpec((tm, LANES), lambda i, e, j: (i, 0)),
                  pl.BlockSpec((LANES, tm), lambda i, e, j: (0, i)),
                  pl.BlockSpec((1, D_MODEL, tf), lambda i, e, j: (e, 0, j)),
                  pl.BlockSpec((1, D_MODEL, tf), lambda i, e, j: (e, 0, j)),
                  pl.BlockSpec((1, tf, D_MODEL), lambda i, e, j: (e, j, 0))],
        out_specs=pl.BlockSpec((tm, D_MODEL), lambda i, e, j: (i, 0)),
        out_shape=jax.ShapeDtypeStruct((M, D_MODEL), F32),
        scratch_shapes=[pltpu.VMEM((tm, D_MODEL), BF16),
                        pltpu.VMEM((tm, LANES), F32), pltpu.VMEM((LANES, tm), F32),
                        pltpu.VMEM((tm, D_MODEL), BF16), pltpu.VMEM((tm, D_MODEL), F32),
                        pltpu.SMEM((1,), I32)],
        compiler_params=pltpu.CompilerParams(
            dimension_semantics=("parallel", "arbitrary", "arbitrary"), vmem_limit_bytes=MOE_VMEM_LIMIT),
        name="moe_ffn",
    )(x, g.reshape(1, D_MODEL), gate, gate.T, wg, wu, wd)


def _rmsnorm_kernel(x_ref, g_ref, o_ref):
    x = x_ref[...]
    ms = jnp.mean(x * x, axis=-1, keepdims=True)
    o_ref[...] = x * lax.rsqrt(ms + EPS) * g_ref[...]


def rmsnorm(x, g):
    M = x.shape[0]
    tm = _pick(M, (1024, 512, 256, 128))
    return pl.pallas_call(
        _rmsnorm_kernel,
        grid=(M // tm,),
        in_specs=[pl.BlockSpec((tm, D_MODEL), lambda i: (i, 0)),
                  pl.BlockSpec((1, D_MODEL), lambda i: (0, 0))],
        out_specs=pl.BlockSpec((tm, D_MODEL), lambda i: (i, 0)),
        out_shape=jax.ShapeDtypeStruct((M, D_MODEL), F32),
        compiler_params=_cparams("parallel"),
        name="final_rmsnorm",
    )(x, g.reshape(1, D_MODEL))


def _split_w_in(w_in):
    offs = np.cumsum((0,) + IN_SPLITS)
    seg = {n: w_in[:, offs[i]:offs[i + 1]] for i, n in enumerate(
        ("z", "xbc", "dt", "q", "k", "v", "iq", "iw", "ik", "rq", "rk", "rv", "rg", "gates"))}
    cat = lambda *names: jnp.concatenate([seg[n] if isinstance(n, str) else n for n in names], axis=1)
    dt_wide = jnp.repeat(seg["dt"], SSM_HEAD_DIM, axis=1)
    small = cat("dt", "iw", jnp.zeros((D_MODEL, LANES - 2 * IDX_HEADS - IDX_DIM), w_in.dtype), "ik")
    q_scaled = seg["q"] * (ATT_HEAD_DIM ** -0.5)
    groups = dict(ssm=cat("z", "xbc", dt_wide), qiq=cat(q_scaled, "iq"), kvs=cat("k", "v", small),
                  ret=cat("rq", "rk", "rv", "rg"), gate=seg["gates"])
    return {n: w.astype(BF16) for n, w in groups.items()}


def _rope_tables(pos):
    half = RET_DK // 2
    inv = ROPE_BASE ** (-jnp.arange(half, dtype=F32) / half)
    ang = pos.astype(F32)[:, None] * inv[None, :]
    cos, sin = jnp.cos(ang), jnp.sin(ang)
    return jnp.concatenate([cos, cos], axis=1), jnp.concatenate([-sin, sin], axis=1)


def _pad_rows(a, L):
    return jnp.pad(a, ((0, 0), (0, L - a.shape[1]), (0, 0)))


def _mixers(h, pos0, conv0, ssm0, ret0, attend, lw):
    b, L, _ = h.shape
    hf = h.reshape(b * L, D_MODEL)
    proj = {n: rms_matmul(hf, lw["norm_g"], w).reshape(b, L, -1)
            for n, w in lw["w_in"].items() if n != "qiq"}
    qiq = rms_matmul_heads(hf, lw["norm_g"], lw["w_in"]["qiq"], ATT_HEAD_DIM)
    k = proj["kvs"][..., :KV_WIDTH]
    v = proj["kvs"][..., KV_WIDTH:2 * KV_WIDTH]
    small = proj["kvs"][..., 2 * KV_WIDTH:]
    ik = small[..., LANES - IDX_DIM:]

    Lp = -(-L // CHUNK) * CHUNK
    last_valid = L - (Lp - CHUNK)
    dtT = jnp.swapaxes(_pad_rows(small[..., :SSM_HEADS], Lp), 1, 2)
    s0 = jnp.transpose(ssm0, (0, 2, 1, 3)).reshape(b, SSM_STATE, SSM_INNER)
    ys, conv_new, s_new = ssd_branch(_pad_rows(proj["ssm"], Lp), dtT, conv0, s0, lw["conv_w"], lw["conv_b"],
                                     lw["dt_bias"], lw["a_log"], lw["d_skip"], lw["ssm_norm_g"], last_valid)
    ssm_new = jnp.transpose(s_new.reshape(b, SSM_STATE, SSM_HEADS, SSM_HEAD_DIM), (0, 2, 1, 3))

    cos2, sin2 = _rope_tables(pos0 + jnp.arange(Lp))
    yr, ret_new = ret_branch(_pad_rows(proj["ret"], Lp), cos2, sin2, ret0, last_valid)

    ya = attend(qiq, k, v, small, ik, b, L)

    ys = ys[:, :L].reshape(b * L, -1)
    yr = yr[:, :L].reshape(b * L, -1)
    out = merge_branches(ys, ya.reshape(b * L, -1), yr, proj["gate"].reshape(b * L, -1), hf,
                         lw["w_branch"], lw["w_out"])
    return out.reshape(b, L, D_MODEL), (k.reshape(b, L, ATT_KV_HEADS, ATT_HEAD_DIM),
                                        v.reshape(b, L, ATT_KV_HEADS, ATT_HEAD_DIM),
                                        ik, ssm_new, conv_new, ret_new)


def _attend_prompt(qiq, k, v, small, ik, b, L):
    k_hm = jnp.transpose(k.reshape(b, L, ATT_KV_HEADS, ATT_HEAD_DIM), (0, 2, 1, 3)).astype(BF16)
    v_t = jnp.transpose(v.reshape(b, L, ATT_KV_HEADS, ATT_HEAD_DIM), (0, 2, 3, 1)).astype(BF16)
    v_t = jnp.concatenate([v_t, jnp.ones((b, ATT_KV_HEADS, V_ROWS - ATT_HEAD_DIM, L), BF16)], axis=2)
    iw_t = jnp.swapaxes(small[..., IDX_HEADS:2 * IDX_HEADS], 1, 2)
    return dsa_prompt(qiq, iw_t, ik.astype(BF16), k_hm, v_t)


def _attend_sample(qiq, k, v, small, ik, b, L, *, ck_t, cv_t, cik_t, layer, page_table):
    rows = lambda a: jnp.transpose(a, (1, 0, 2)).reshape(b, L, ATT_HEADS * ATT_HEAD_DIM)
    iw = small[..., IDX_HEADS:2 * IDX_HEADS]
    return dsa_sample(rows(qiq[:ATT_HEADS]), k, v, rows(qiq[ATT_HEADS:]), iw, ik,
                      ck_t, cv_t, cik_t, layer, page_table)


def _channel_mixer(h, l, p):
    b, L, _ = h.shape
    hf = h.reshape(b * L, D_MODEL)
    j = l // 2
    if l % 2 == 0:
        out = ffn_dense(hf, p["norm_ffn_g"][l], p["w_ffn_gate"][j].astype(BF16),
                        p["w_ffn_up"][j].astype(BF16), p["w_ffn_down"][j].astype(BF16))
    else:
        gate = moe_router(hf, p["norm_ffn_g"][l], p["w_router"][j])
        out = moe_ffn(hf, p["norm_ffn_g"][l], gate, p["w_moe_gate"][j].astype(BF16),
                      p["w_moe_up"][j].astype(BF16), p["w_moe_down"][j].astype(BF16))
    return out.reshape(b, L, D_MODEL)


def kernel(x_prompt, x_sample, cache_k, cache_v, cache_idx_k, state_ssm, state_conv, state_ret,
           page_table, norm_mix_g, w_in, conv_w, conv_b, dt_bias, a_log, d_skip, ssm_norm_g,
           w_branch, w_out, norm_ffn_g, w_ffn_gate, w_ffn_up, w_ffn_down, w_router,
           w_moe_gate, w_moe_up, w_moe_down, final_norm_g):
    depth = w_in.shape[0]
    past = page_table.shape[1] * PAGE_SIZE
    bp = x_prompt.shape[0]
    p = dict(norm_ffn_g=norm_ffn_g, w_ffn_gate=w_ffn_gate, w_ffn_up=w_ffn_up, w_ffn_down=w_ffn_down,
             w_router=w_router, w_moe_gate=w_moe_gate, w_moe_up=w_moe_up, w_moe_down=w_moe_down)
    n_phys = cache_k.shape[1]
    ck_t = jnp.transpose(cache_k, (0, 1, 3, 4, 2)).reshape(depth, n_phys, KV_WIDTH, PAGE_SIZE)
    cv_t = jnp.transpose(cache_v, (0, 1, 3, 4, 2)).reshape(depth, n_phys, KV_WIDTH, PAGE_SIZE)
    cik_t = jnp.transpose(cache_idx_k, (0, 1, 3, 2))
    hp, hs = x_prompt, x_sample
    st_p, st_s = [], []
    for l in range(depth):
        lw = dict(norm_g=norm_mix_g[l], w_in=_split_w_in(w_in[l]), conv_w=conv_w[l], conv_b=conv_b[l],
                  dt_bias=dt_bias[l], a_log=a_log[l], d_skip=d_skip[l], ssm_norm_g=ssm_norm_g[l],
                  w_branch=w_branch[l].astype(BF16), w_out=w_out[l].astype(BF16))
        hp, sp = _mixers(hp, 0,
                         jnp.zeros((bp, CONV_WIDTH - 1, CONV_DIM), F32),
                         jnp.zeros((bp, SSM_HEADS, SSM_STATE, SSM_HEAD_DIM), F32),
                         jnp.zeros((bp, RET_HEADS, RET_DK, RET_DV), F32),
                         _attend_prompt, lw)
        attend_s = functools.partial(_attend_sample, ck_t=ck_t, cv_t=cv_t, cik_t=cik_t, layer=l,
                                     page_table=page_table)
        hs, ss = _mixers(hs, past, state_conv[l], state_ssm[l], state_ret[l], attend_s, lw)
        hp = _channel_mixer(hp, l, p)
        hs = _channel_mixer(hs, l, p)
        st_p.append(sp)
        st_s.append(ss)
    y_prompt = rmsnorm(hp.reshape(-1, D_MODEL), final_norm_g).reshape(hp.shape)
    y_sample = rmsnorm(hs.reshape(-1, D_MODEL), final_norm_g).reshape(hs.shape)
    stack = lambda sts, i: jnp.stack([s[i] for s in sts])
    return (y_prompt, y_sample,
            stack(st_p, 0), stack(st_p, 1), stack(st_p, 2), stack(st_p, 3), stack(st_p, 4), stack(st_p, 5),
            stack(st_s, 0), stack(st_s, 1), stack(st_s, 2), stack(st_s, 3), stack(st_s, 4), stack(st_s, 5))
```

```python
import functools
import math

import jax
import jax.numpy as jnp
import numpy as np
from jax import lax
from jax.experimental import pallas as pl
from jax.experimental.pallas import tpu as pltpu

F32 = jnp.float32
BF16 = jnp.bfloat16
I32 = jnp.int32
I16 = jnp.int16
HALF16 = 1 << 15

D_MODEL = 1024
PAGE_SIZE = 128
SSM_HEADS = 16
SSM_HEAD_DIM = 64
SSM_INNER = 1024
SSM_GROUPS = 2
SSM_STATE = 128
CONV_WIDTH = 4
CONV_DIM = 1536
ATT_HEADS = 16
ATT_KV_HEADS = 4
ATT_HEAD_DIM = 64
KV_WIDTH = 256
IDX_HEADS = 16
IDX_DIM = 64
TOPK_MAX = 256
RET_HEADS = 4
RET_DK = 128
RET_DV = 256
ROPE_BASE = 10000.0
N_BRANCH = 3
D_FF = 3584
N_EXPERTS = 8
EPS = 1e-6
IN_SPLITS = (1024, 1536, 16, 1024, 256, 256, 1024, 16, 64, 512, 512, 1024, 1024, 3072)

LANES = 128
SUBLANES = 8
VMEM_LIMIT = 48 * 1024 * 1024
MOE_VMEM_LIMIT = 56 * 1024 * 1024

CHUNK = 128
INT_MIN = -(2 ** 31)
KEY_POS_INF = 0x7F800000
KEY_NEG_INF = -2139095041
NEG_BIG = -1e30


def _cparams(*sem):
    return pltpu.CompilerParams(dimension_semantics=sem, vmem_limit_bytes=VMEM_LIMIT)


def _dot(a, b):
    return jnp.dot(a, b, preferred_element_type=F32)


def _dot_nt(a, b):
    return lax.dot_general(a, b, (((1,), (1,)), ((), ())), preferred_element_type=F32)


def _dot_tn(a, b):
    return lax.dot_general(a, b, (((0,), (0,)), ((), ())), preferred_element_type=F32)


def _split3(x):
    hi = x.astype(BF16)
    r = x - hi.astype(F32)
    mid = r.astype(BF16)
    lo = (r - mid.astype(F32)).astype(BF16)
    return hi, mid, lo


def _silu(x):
    return x * jax.nn.sigmoid(x)


def _softplus(x):
    return jnp.maximum(x, 0.0) + jnp.log1p(jnp.exp(-jnp.abs(x)))


def _float_key(x):
    x = jnp.where(x == 0.0, 0.0, x)
    b = lax.bitcast_convert_type(x, I32)
    return jnp.where(b >= 0, b, b ^ jnp.int32(0x7FFFFFFF))


def _rms_matmul_kernel(x_ref, g_ref, w_ref, o_ref):
    x = x_ref[...]
    ms = jnp.mean(x * x, axis=-1, keepdims=True)
    xn = (x * lax.rsqrt(ms + EPS) * g_ref[...]).astype(BF16)
    o_ref[...] = _dot(xn, w_ref[...]).astype(o_ref.dtype)


def _rms_matmul_heads_kernel(x_ref, g_ref, w_ref, o_ref, *, n_heads, head_dim):
    x = x_ref[...]
    ms = jnp.mean(x * x, axis=-1, keepdims=True)
    xn = (x * lax.rsqrt(ms + EPS) * g_ref[...]).astype(BF16)
    res = _dot(xn, w_ref[...])
    for h in range(n_heads):
        o_ref[h] = res[:, h * head_dim:(h + 1) * head_dim].astype(o_ref.dtype)


def _pick(n, prefs):
    for p in prefs:
        if n % p == 0:
            return p
    return n


def rms_matmul(x, g, w, out_dtype=F32):
    M, K = x.shape
    N = w.shape[1]
    tm = _pick(M, (512, 256, 128))
    return pl.pallas_call(
        _rms_matmul_kernel,
        grid=(M // tm,),
        in_specs=[pl.BlockSpec((tm, K), lambda i: (i, 0)),
                  pl.BlockSpec((1, K), lambda i: (0, 0)),
                  pl.BlockSpec((K, N), lambda i: (0, 0))],
        out_specs=pl.BlockSpec((tm, N), lambda i: (i, 0)),
        out_shape=jax.ShapeDtypeStruct((M, N), out_dtype),
        compiler_params=_cparams("parallel"),
        name="rms_matmul",
    )(x, g.reshape(1, K), w)


def rms_matmul_heads(x, g, w, head_dim):
    M, K = x.shape
    N = w.shape[1]
    n_heads = N // head_dim
    tm = _pick(M, (512, 256, 128))
    kern = functools.partial(_rms_matmul_heads_kernel, n_heads=n_heads, head_dim=head_dim)
    return pl.pallas_call(
        kern,
        grid=(M // tm,),
        in_specs=[pl.BlockSpec((tm, K), lambda i: (i, 0)),
                  pl.BlockSpec((1, K), lambda i: (0, 0)),
                  pl.BlockSpec((K, N), lambda i: (0, 0))],
        out_specs=pl.BlockSpec((n_heads, tm, head_dim), lambda i: (0, i, 0)),
        out_shape=jax.ShapeDtypeStruct((n_heads, M, head_dim), BF16),
        compiler_params=_cparams("parallel"),
        name="rms_matmul_heads",
    )(x, g.reshape(1, K), w)


def _ssd_kernel(zxd_ref, dtT_ref, conv0_ref, s0_ref, cw_ref, cb_ref, dtb_ref, alog_ref, dsk_ref,
                ng_ref, dtbT_ref, alogT_ref, y_ref, convn_ref, sn_ref, xpad_ref, st_ref,
                *, Q, last_valid, nc):
    c = pl.program_id(1)
    GW = SSM_INNER // SSM_GROUPS

    @pl.when(c == 0)
    def _():
        xpad_ref[0:8, :] = jnp.zeros((8, CONV_DIM), F32)
        xpad_ref[5:8, :] = conv0_ref[0]
        st_ref[...] = s0_ref[0]

    blk = zxd_ref[0]
    z = blk[:, :SSM_INNER]
    xbc = blk[:, SSM_INNER:SSM_INNER + CONV_DIM]
    dtr = blk[:, SSM_INNER + CONV_DIM:]
    xpad_ref[8:8 + Q, :] = xbc
    cw = cw_ref[...]
    conv = (xpad_ref[5:5 + Q, :] * cw[0:1] + xpad_ref[6:6 + Q, :] * cw[1:2]
            + xpad_ref[7:7 + Q, :] * cw[2:3] + xbc * cw[3:4]) + cb_ref[...]
    xc = _silu(conv)
    xs = xc[:, :SSM_INNER]
    Bm = xc[:, SSM_INNER:SSM_INNER + SSM_GROUPS * SSM_STATE]
    Cm = xc[:, SSM_INNER + SSM_GROUPS * SSM_STATE:]

    row = lax.broadcasted_iota(I32, (Q, 1), 0)
    colq = lax.broadcasted_iota(I32, (1, Q), 1)
    assert last_valid == Q or nc == 1
    lv = last_valid
    valid = row < lv
    tril = (lax.broadcasted_iota(I32, (Q, Q), 0) >= lax.broadcasted_iota(I32, (Q, Q), 1))
    tril_b = jnp.where(tril, 1.0, 0.0).astype(BF16)
    triu_b = jnp.where(lax.broadcasted_iota(I32, (Q, Q), 0) <= lax.broadcasted_iota(I32, (Q, Q), 1),
                       1.0, 0.0).astype(BF16)

    nega = -jnp.exp(alog_ref[...])
    dt = _softplus(dtr + dtb_ref[...])
    la = jnp.where(valid, dt * nega, 0.0)
    cum = sum(_dot(tril_b, p) for p in _split3(la))
    laT = jnp.where(colq < lv, _softplus(dtT_ref[0] + dtbT_ref[...]) * (-jnp.exp(alogT_ref[...])), 0.0)
    cumT = sum(_dot(p, triu_b) for p in _split3(laT))

    ecum = jnp.exp(cum)
    cl = cum[lv - 1:lv, :]
    xdt = xs * dt
    xtail = jnp.where(valid, xdt * jnp.exp(cl - cum), 0.0)
    cdecay = jnp.exp(cl)

    lane = lax.broadcasted_iota(I32, (1, LANES), 1)
    y_groups = []
    for g in range(SSM_GROUPS):
        l0 = g * GW
        Cg = Cm[:, g * SSM_STATE:(g + 1) * SSM_STATE].astype(BF16)
        Bg = Bm[:, g * SSM_STATE:(g + 1) * SSM_STATE].astype(BF16)
        G = _dot_nt(Cg, Bg)
        st_g = st_ref[:, l0:l0 + GW]
        inter = _dot(Cg, st_g.astype(BF16)) * ecum[:, l0:l0 + GW]
        local = _dot_tn(Bg, xtail[:, l0:l0 + GW].astype(BF16))
        st_ref[:, l0:l0 + GW] = st_g * cdecay[:, l0:l0 + GW] + local
        pairs = []
        for p in range(GW // LANES):
            xp = xdt[:, l0 + p * LANES:l0 + (p + 1) * LANES]
            acc = None
            for hh in range(2):
                h = (l0 + p * LANES) // SSM_HEAD_DIM + hh
                ccol = cum[:, h * SSM_HEAD_DIM:h * SSM_HEAD_DIM + 1]
                diff = ccol - cumT[h:h + 1, :]
                dm = jnp.exp(jnp.where(tril, diff, -jnp.inf))
                s = (G * dm).astype(BF16)
                half = (lane >= hh * SSM_HEAD_DIM) & (lane < (hh + 1) * SSM_HEAD_DIM)
                part = _dot(s, jnp.where(half, xp, 0.0).astype(BF16))
                acc = part if acc is None else acc + part
            pairs.append(acc)
        y_groups.append(jnp.concatenate(pairs, axis=1) + inter)
    y = jnp.concatenate(y_groups, axis=1)

    y = (y + dsk_ref[...] * xs) * _silu(z)
    outs = []
    for g in range(SSM_GROUPS):
        seg = y[:, g * GW:(g + 1) * GW]
        ms = jnp.mean(seg * seg, axis=-1, keepdims=True)
        outs.append(seg * lax.rsqrt(ms + EPS) * ng_ref[:, g * GW:(g + 1) * GW])
    y_ref[0] = jnp.concatenate(outs, axis=1)

    @pl.when(c == nc - 1)
    def _():
        convn_ref[0] = xpad_ref[5 + last_valid:8 + last_valid, :]
        sn_ref[0] = st_ref[...]

    xpad_ref[0:8, :] = xpad_ref[Q:Q + 8, :]


def ssd_branch(zxd, dtT, conv0, s0, conv_w, conv_b, dt_bias, a_log, d_skip, norm_g, last_valid):
    b, L, W = zxd.shape
    Q = CHUNK
    nc = L // Q
    rep = lambda v: jnp.repeat(v.astype(F32), SSM_HEAD_DIM).reshape(1, SSM_INNER)
    col = lambda v: v.astype(F32).reshape(SSM_HEADS, 1)
    full = lambda shape: pl.BlockSpec(shape, lambda i, c: (0,) * len(shape))
    kern = functools.partial(_ssd_kernel, Q=Q, last_valid=last_valid, nc=nc)
    return pl.pallas_call(
        kern,
        grid=(b, nc),
        in_specs=[pl.BlockSpec((1, Q, W), lambda i, c: (i, c, 0)),
                  pl.BlockSpec((1, SSM_HEADS, Q), lambda i, c: (i, 0, c)),
                  pl.BlockSpec((1, CONV_WIDTH - 1, CONV_DIM), lambda i, c: (i, 0, 0)),
                  pl.BlockSpec((1, SSM_STATE, SSM_INNER), lambda i, c: (i, 0, 0)),
                  full((CONV_WIDTH, CONV_DIM)), full((1, CONV_DIM)),
                  full((1, SSM_INNER)), full((1, SSM_INNER)), full((1, SSM_INNER)), full((1, SSM_INNER)),
                  full((SSM_HEADS, 1)), full((SSM_HEADS, 1))],
        out_specs=[pl.BlockSpec((1, Q, SSM_INNER), lambda i, c: (i, c, 0)),
                   pl.BlockSpec((1, CONV_WIDTH - 1, CONV_DIM), lambda i, c: (i, 0, 0)),
                   pl.BlockSpec((1, SSM_STATE, SSM_INNER), lambda i, c: (i, 0, 0))],
        out_shape=[jax.ShapeDtypeStruct((b, L, SSM_INNER), F32),
                   jax.ShapeDtypeStruct((b, CONV_WIDTH - 1, CONV_DIM), F32),
                   jax.ShapeDtypeStruct((b, SSM_STATE, SSM_INNER), F32)],
        scratch_shapes=[pltpu.VMEM((Q + 8, CONV_DIM), F32), pltpu.VMEM((SSM_STATE, SSM_INNER), F32)],
        compiler_params=_cparams("parallel", "arbitrary"),
        name="ssd_branch",
    )(zxd, dtT, conv0, s0, conv_w.astype(F32), conv_b.reshape(1, CONV_DIM).astype(F32),
      rep(dt_bias), rep(a_log), rep(d_skip), norm_g.reshape(1, SSM_INNER).astype(F32),
      col(dt_bias), col(a_log))


def _ret_kernel(x_ref, cos_ref, sin_ref, r0_ref, y_ref, rn_ref, st_ref, *, Q, last_valid, nc):
    c = pl.program_id(1)

    @pl.when(c == 0)
    def _():
        st_ref[...] = r0_ref[0]

    blk = x_ref[0]
    cos2 = cos_ref[...]
    sin2 = sin_ref[...]
    ri = lax.broadcasted_iota(I32, (Q, Q), 0)
    ci = lax.broadcasted_iota(I32, (Q, Q), 1)
    dij = (ri - ci).astype(F32)
    row = lax.broadcasted_iota(I32, (Q, 1), 0)
    assert last_valid == Q or nc == 1
    lv = last_valid
    rowf = row.astype(F32)
    lvf = float(lv)
    KO = RET_HEADS * RET_DK
    outs = []
    for h in range(RET_HEADS):
        lg = math.log1p(-2.0 ** (-5.0 - h))
        qh = blk[:, h * RET_DK:(h + 1) * RET_DK]
        kh = blk[:, KO + h * RET_DK:KO + (h + 1) * RET_DK]
        vh = blk[:, 2 * KO + h * RET_DV:2 * KO + (h + 1) * RET_DV].astype(BF16)
        gh = blk[:, 2 * KO + RET_HEADS * RET_DV + h * RET_DV:2 * KO + RET_HEADS * RET_DV + (h + 1) * RET_DV]
        qr = qh * cos2 + pltpu.roll(qh, RET_DK // 2, 1) * sin2
        kr = (kh * cos2 + pltpu.roll(kh, RET_DK // 2, 1) * sin2) * (RET_DK ** -0.5)
        qb = qr.astype(BF16)
        dm = jnp.exp(jnp.where(ri >= ci, dij * lg, -jnp.inf))
        s = (_dot_nt(qb, kr.astype(BF16)) * dm).astype(BF16)
        st_h = st_ref[h]
        y = _dot(s, vh) + jnp.exp((rowf + 1.0) * lg) * _dot(qb, st_h.astype(BF16))
        ktail = jnp.where(row < lv, kr * jnp.exp((lvf - 1.0 - rowf) * lg), 0.0)
        st_ref[h] = st_h * math.exp(lvf * lg) + _dot_tn(ktail.astype(BF16), vh)
        ms = jnp.mean(y * y, axis=-1, keepdims=True)
        outs.append(y * lax.rsqrt(ms + EPS) * _silu(gh))
    y_ref[0] = jnp.concatenate(outs, axis=1)

    @pl.when(c == nc - 1)
    def _():
        rn_ref[0] = st_ref[...]


def ret_branch(x, cos2, sin2, r0, last_valid):
    b, L, W = x.shape
    Q = CHUNK
    nc = L // Q
    kern = functools.partial(_ret_kernel, Q=Q, last_valid=last_valid, nc=nc)
    return pl.pallas_call(
        kern,
        grid=(b, nc),
        in_specs=[pl.BlockSpec((1, Q, W), lambda i, c: (i, c, 0)),
                  pl.BlockSpec((Q, RET_DK), lambda i, c: (c, 0)),
                  pl.BlockSpec((Q, RET_DK), lambda i, c: (c, 0)),
                  pl.BlockSpec((1, RET_HEADS, RET_DK, RET_DV), lambda i, c: (i, 0, 0, 0))],
        out_specs=[pl.BlockSpec((1, Q, RET_HEADS * RET_DV), lambda i, c: (i, c, 0)),
                   pl.BlockSpec((1, RET_HEADS, RET_DK, RET_DV), lambda i, c: (i, 0, 0, 0))],
        out_shape=[jax.ShapeDtypeStruct((b, L, RET_HEADS * RET_DV), F32),
                   jax.ShapeDtypeStruct((b, RET_HEADS, RET_DK, RET_DV), F32)],
        scratch_shapes=[pltpu.VMEM((RET_HEADS, RET_DK, RET_DV), F32)],
        compiler_params=_cparams("parallel", "arbitrary"),
        name="ret_branch",
    )(x, cos2, sin2, r0)


def _kth_largest_key(count_ge, k, shape):
    def body(t, prefix):
        bit = lax.shift_left(jnp.int32(1), jnp.int32(31) - t)
        cand = prefix | bit
        cnt = count_ge(cand ^ jnp.int32(INT_MIN))
        return jnp.where(cnt >= k, cand, prefix)

    prefix = lax.fori_loop(0, 32, body, jnp.zeros(shape, I32))
    return prefix ^ jnp.int32(INT_MIN)


def _dsa_prompt_kernel(q_ref, iq_ref, iwt_ref, ik_ref, k_ref, vt_ref, o_ref, keys_ref, khi_ref, klo_ref,
                       jb_ref, m_ref, acc_ref, *, tq, ck, cka, L, k_sel):
    i = pl.program_id(1)
    G = ATT_HEADS // ATT_KV_HEADS
    PART = 4 * SUBLANES
    nk = lax.div((i + 1) * tq + (ck - 1), ck)
    qpos = i * tq + lax.broadcasted_iota(I32, (1, tq), 1)
    rowk = lax.broadcasted_iota(I32, (ck, 1), 0)
    iw = iwt_ref[0] * (IDX_HEADS ** -0.5 * IDX_DIM ** -0.5)

    def score_body(kc, carry):
        off = pl.multiple_of(kc * ck, ck)
        ikc = ik_ref[0, pl.ds(off, ck), :]
        acc = jnp.zeros((ck, tq), F32)
        for h4 in range(0, IDX_HEADS, G):
            d = _dot_nt(ikc, iq_ref[h4:h4 + G].reshape(G * tq, IDX_DIM))
            for j in range(G):
                acc = acc + jnp.maximum(d[:, j * tq:(j + 1) * tq], 0.0) * iw[h4 + j:h4 + j + 1, :]
        key = jnp.where(off + rowk <= qpos, _float_key(acc), jnp.int32(INT_MIN))
        keys_ref[pl.ds(off, ck), :] = key
        khi_ref[pl.ds(off, ck), :] = lax.shift_right_arithmetic(key, 16).astype(I16)
        klo_ref[pl.ds(off, ck), :] = ((key & 0xFFFF) - HALF16).astype(I16)
        return carry

    lax.fori_loop(0, nk, score_body, 0)

    def count(pred):
        def body(kc, part):
            off = pl.multiple_of(kc * ck, ck)
            hit = jnp.where(pred(keys_ref[pl.ds(off, ck), :], off + rowk), 1.0, 0.0)
            return part + jnp.sum(hit.reshape(ck // PART, PART, tq), axis=0)
        part = lax.fori_loop(0, nk, body, jnp.zeros((PART, tq), F32))
        return jnp.sum(part, axis=0, keepdims=True)

    def count16(ref, pred):
        def body(kc, part):
            off = pl.multiple_of(kc * ck, ck)
            hit = jnp.where(pred(ref[pl.ds(off, ck), :]), jnp.int16(1), jnp.int16(0))
            rows = [hit[r:r + PART] for r in range(0, ck, PART)]
            while len(rows) > 1:
                rows = [a + b for a, b in zip(rows[::2], rows[1::2])]
            return part + rows[0]
        part = lax.fori_loop(0, nk, body, jnp.zeros((PART, tq), I16))
        return jnp.sum(part.astype(F32), axis=0, keepdims=True)

    def kth_largest16(ref, k):
        def body(t, prefix):
            cand = prefix | lax.shift_left(jnp.int32(1), jnp.int32(15) - t)
            c16 = (cand - HALF16).astype(I16)
            return jnp.where(count16(ref, lambda kk: kk >= c16) >= k, cand, prefix)
        return lax.fori_loop(0, 16, body, jnp.zeros((1, tq), I32)) - HALF16

    kf = float(k_sel)
    thr_hi = kth_largest16(khi_ref, jnp.full((1, tq), kf, F32))
    hi16 = thr_hi.astype(I16)
    k_lo = kf - count16(khi_ref, lambda kk: kk > hi16)

    def mask_body(kc, carry):
        off = pl.multiple_of(kc * ck, ck)
        klo_ref[pl.ds(off, ck), :] = jnp.where(khi_ref[pl.ds(off, ck), :] == hi16,
                                               klo_ref[pl.ds(off, ck), :], jnp.int16(-HALF16))
        return carry

    lax.fori_loop(0, nk, mask_body, 0)
    thr_lo = kth_largest16(klo_ref, k_lo)
    thr = lax.shift_left(thr_hi, 16) | (thr_lo + HALF16)
    thr = jnp.maximum(thr, jnp.int32(KEY_NEG_INF + 1))
    n_ge = count(lambda kk, pos: kk >= thr)
    jb_ref[...] = jnp.full((1, tq), L, I32)

    @pl.when(jnp.max(n_ge) > kf)
    def _():
        need = kf - count(lambda kk, pos: kk > thr)
        nb = max(1, (L - 1).bit_length())

        def body(t, pfx):
            cand = pfx | lax.shift_left(jnp.int32(1), jnp.int32(nb - 1) - t)
            cnt = count(lambda kk, pos: (kk == thr) & (pos < cand))
            return jnp.where(cnt < need, cand, pfx)

        pfx = lax.fori_loop(0, nb, body, jnp.zeros((1, tq), I32))
        jb_ref[...] = jnp.where(n_ge > kf, pfx + 1, L)

    jb = jb_ref[...]

    m_ref[...] = jnp.full(m_ref.shape, NEG_BIG, F32)
    acc_ref[...] = jnp.zeros(acc_ref.shape, F32)
    rowa = lax.broadcasted_iota(I32, (cka, 1), 0)

    def att_body(kc, carry):
        off = pl.multiple_of(kc * cka, cka)
        kk = keys_ref[pl.ds(off, cka), :]
        sel = (kk >= thr) & (kk < jnp.int32(KEY_POS_INF)) & ((kk > thr) | (off + rowa < jb))
        bias = jnp.where(sel, 0.0, -jnp.inf)
        bias = jnp.concatenate([bias] * ATT_HEADS, axis=1)
        s = jnp.concatenate(
            [_dot_nt(k_ref[0, g, pl.ds(off, cka), :], q_ref[g * G:(g + 1) * G].reshape(G * tq, ATT_HEAD_DIM))
             for g in range(ATT_KV_HEADS)], axis=1) + bias
        m = m_ref[...]
        smax = jnp.max(jnp.max(s.reshape(cka // PART, PART, ATT_HEADS * tq), axis=0), axis=0, keepdims=True)
        m_new = jnp.maximum(m, smax)
        p = jnp.exp(s - m_new).astype(BF16)
        pv = jnp.concatenate(
            [_dot(vt_ref[0, g, :, pl.ds(off, cka)], p[:, g * G * tq:(g + 1) * G * tq])
             for g in range(ATT_KV_HEADS)], axis=1)
        acc_ref[...] = jnp.exp(m - m_new) * acc_ref[...] + pv
        m_ref[...] = m_new
        return carry

    lax.fori_loop(0, lax.div((i + 1) * tq + (cka - 1), cka), att_body, 0)
    acc = acc_ref[...]
    o = acc[:ATT_HEAD_DIM] / acc[ATT_HEAD_DIM:ATT_HEAD_DIM + 1]
    for hp in range(ATT_HEADS // 2):
        pair = jnp.concatenate([o[:, (2 * hp) * tq:(2 * hp + 1) * tq],
                                o[:, (2 * hp + 1) * tq:(2 * hp + 2) * tq]], axis=0)
        o_ref[0, :, hp * LANES:(hp + 1) * LANES] = pair.T


V_ROWS = ATT_HEAD_DIM + 16


def dsa_prompt(qiq, iw_t, ik, k_hm, v_t):
    b, _, L, _ = k_hm.shape
    tq = min(128, L)
    nq = L // tq
    ck = min(512, L)
    cka = ck
    gq = (ATT_HEADS // ATT_KV_HEADS) * tq
    k_sel = max(1, min(TOPK_MAX, L // 4))
    kern = functools.partial(_dsa_prompt_kernel, tq=tq, ck=ck, cka=cka, L=L, k_sel=k_sel)
    return pl.pallas_call(
        kern,
        grid=(b, L // tq),
        in_specs=[pl.BlockSpec((ATT_HEADS, tq, ATT_HEAD_DIM), lambda bi, i: (0, bi * nq + i, 0)),
                  pl.BlockSpec((IDX_HEADS, tq, IDX_DIM), lambda bi, i: (1, bi * nq + i, 0)),
                  pl.BlockSpec((1, IDX_HEADS, tq), lambda bi, i: (bi, 0, i)),
                  pl.BlockSpec((1, L, IDX_DIM), lambda bi, i: (bi, 0, 0)),
                  pl.BlockSpec((1, ATT_KV_HEADS, L, ATT_HEAD_DIM), lambda bi, i: (bi, 0, 0, 0)),
                  pl.BlockSpec((1, ATT_KV_HEADS, V_ROWS, L), lambda bi, i: (bi, 0, 0, 0))],
        out_specs=pl.BlockSpec((1, tq, ATT_HEADS * ATT_HEAD_DIM), lambda bi, i: (bi, i, 0)),
        out_shape=jax.ShapeDtypeStruct((b, L, ATT_HEADS * ATT_HEAD_DIM), F32),
        scratch_shapes=[pltpu.VMEM((L, tq), I32), pltpu.VMEM((L, tq), I16), pltpu.VMEM((L, tq), I16),
                        pltpu.VMEM((1, tq), I32),
                        pltpu.VMEM((1, ATT_HEADS * tq), F32),
                        pltpu.VMEM((V_ROWS, ATT_HEADS * tq), F32)],
        compiler_params=_cparams("parallel", "arbitrary"),
        name="dsa_prompt",
    )(qiq, qiq, iw_t, ik, k_hm, v_t)


TQ8 = SUBLANES


def _sample_score_keys(d, iww):
    n = d.shape[1]
    r = jnp.maximum(d, 0.0) * (iww[:, 0:1] * (IDX_HEADS ** -0.5 * IDX_DIM ** -0.5))
    return _float_key(jnp.sum(r.reshape(TQ8, IDX_HEADS, n), axis=1))


def _dsa_sample_scores_kernel(pt_ref, *refs, n_pages):
    page_refs = refs[:n_pages]
    iq_ref, iww_ref, keys_ref = refs[n_pages:]
    ik_t = jnp.concatenate([r[0, 0] for r in page_refs], axis=1).astype(BF16)
    keys_ref[0] = _sample_score_keys(_dot(iq_ref[0], ik_t), iww_ref[0])


def _dsa_sample_thr_kernel(kp_ref, iq_ref, iww_ref, ikn_ref, thr_ref, jb_ref, kn_ref,
                           *, nb_seq, T, past, k_sel):
    R8 = nb_seq * TQ8
    kp = kp_ref[...].reshape(R8, past)
    rowi = lax.broadcasted_iota(I32, (TQ8, LANES), 0)
    col1 = lax.broadcasted_iota(I32, (TQ8, LANES), 1)
    kn = jnp.concatenate(
        [jnp.where((col1 <= rowi) & (col1 < T),
                   _sample_score_keys(_dot_nt(iq_ref[s], ikn_ref[s]), iww_ref[s]), jnp.int32(INT_MIN))
         for s in range(nb_seq)], axis=0)
    coln = lax.broadcasted_iota(I32, (R8, LANES), 1)
    colp = lax.broadcasted_iota(I32, (R8, past), 1)

    def count(pred):
        hit = jnp.where(pred(kp, colp), 1.0, 0.0)
        part = jnp.where(pred(kn, coln + past), 1.0, 0.0)
        for t in range(past // LANES):
            part = part + hit[:, t * LANES:(t + 1) * LANES]
        return jnp.sum(part, axis=1, keepdims=True)

    kf = float(k_sel)
    thr = _kth_largest_key(lambda cand: count(lambda kk, col: kk >= cand), kf, (R8, 1))
    thr = jnp.maximum(thr, jnp.int32(KEY_NEG_INF + 1))
    need = kf - count(lambda kk, col: kk > thr)
    nb = (past + LANES - 1).bit_length()

    def body(t, pfx):
        cand = pfx | lax.shift_left(jnp.int32(1), jnp.int32(nb - 1) - t)
        cnt = count(lambda kk, col: (kk == thr) & (col < cand))
        return jnp.where(cnt < need, cand, pfx)

    pfx = lax.fori_loop(0, nb, body, jnp.zeros((R8, 1), I32))
    thr_ref[...] = jnp.broadcast_to(thr, (R8, LANES)).reshape(nb_seq, TQ8, LANES)
    jb_ref[...] = jnp.broadcast_to(pfx + 1, (R8, LANES)).reshape(nb_seq, TQ8, LANES)
    kn_ref[...] = kn.reshape(nb_seq, TQ8, LANES)


def _dsa_sample_attn_kernel(pt_ref, *refs, n_pages, T, past):
    k_refs = refs[:n_pages]
    v_refs = refs[n_pages:2 * n_pages]
    (q_ref, kp_ref, thr_ref, jb_ref, kn_ref, knew_ref, vnew_ref,
     o_ref, m_ref, l_ref, acc_ref) = refs[2 * n_pages:]
    p = pl.program_id(1)
    R = T * ATT_HEADS

    @pl.when(p == 0)
    def _():
        m_ref[...] = jnp.full((R, 1), NEG_BIG, F32)
        l_ref[...] = jnp.zeros((R, 1), F32)
        acc_ref[...] = jnp.zeros((R, KV_WIDTH), F32)

    thr = thr_ref[0][0:T, 0:1]
    jb = jb_ref[0][0:T, 0:1]

    def step(kk, col0, s, pv):
        n = kk.shape[1]
        col = col0 + lax.broadcasted_iota(I32, (1, n), 1)
        sel = (kk >= thr) & (kk < jnp.int32(KEY_POS_INF)) & ((kk > thr) | (col < jb))
        bias = jnp.where(sel, 0.0, -jnp.inf)
        s = (s.reshape(T, ATT_HEADS, n) + bias[:, None, :]).reshape(R, n)
        m = m_ref[...]
        m_new = jnp.maximum(m, jnp.max(s, axis=1, keepdims=True))
        alpha = jnp.exp(m - m_new)
        pr = jnp.exp(s - m_new)
        l_ref[...] = alpha * l_ref[...] + jnp.sum(pr, axis=1, keepdims=True)
        acc_ref[...] = alpha * acc_ref[...] + pv(pr.astype(BF16))
        m_ref[...] = m_new

    k_t = jnp.concatenate([r[0, 0] for r in k_refs], axis=1).astype(BF16)
    v_t = jnp.concatenate([r[0, 0] for r in v_refs], axis=1).astype(BF16)
    step(kp_ref[0][0:T], p * (n_pages * PAGE_SIZE), _dot(q_ref[0], k_t), lambda pr: _dot_nt(pr, v_t))

    @pl.when(p == pl.num_programs(1) - 1)
    def _():
        step(kn_ref[0][0:T], past, _dot_nt(q_ref[0], knew_ref[0]), lambda pr: _dot(pr, vnew_ref[0]))
        o_ref[0] = acc_ref[...] / l_ref[...]


def _pages_per_step(npages, want):
    while npages % want:
        want //= 2
    return want


def dsa_sample(q, k, v, iq, iw, ik, ck_t, cv_t, cik_t, layer, page_table):
    b, T, _ = q.shape
    assert T <= TQ8
    npages = page_table.shape[1]
    past = npages * PAGE_SIZE
    k_sel = max(1, min(TOPK_MAX, (past + T) // 4))
    R = T * ATT_HEADS
    R8 = TQ8 * IDX_HEADS
    pad_to = lambda a, n: jnp.pad(a, ((0, 0), (0, n - a.shape[1]), (0, 0)))
    iq_rows = pad_to(iq.reshape(b, T * IDX_HEADS, IDX_DIM), R8).astype(BF16)
    iww = jnp.broadcast_to(pad_to(iw.reshape(b, T * IDX_HEADS, 1), R8), (b, R8, LANES)).astype(F32)
    ik_new, k_new, v_new = (pad_to(a, LANES).astype(BF16) for a in (ik, k, v))
    head_group = jnp.arange(ATT_HEADS) // (ATT_HEADS // ATT_KV_HEADS)
    onehot = (head_group[:, None] == jnp.arange(ATT_KV_HEADS)[None, :]).astype(F32)
    q_bd = (q.reshape(b, T, ATT_HEADS, 1, ATT_HEAD_DIM) * onehot[None, None, :, :, None])
    q_bd = q_bd.reshape(b, R, KV_WIDTH).astype(BF16)

    def page_spec(rows, per_step, j):
        return pl.BlockSpec((1, 1, rows, PAGE_SIZE),
                            lambda bi, p, pt: (layer, pt[bi, p * per_step + j], 0, 0))

    fixed = lambda shape: pl.BlockSpec((1,) + shape, lambda bi, p, pt: (bi, 0, 0))

    ps = _pages_per_step(npages, 32)
    keys_past = pl.pallas_call(
        functools.partial(_dsa_sample_scores_kernel, n_pages=ps),
        grid_spec=pltpu.PrefetchScalarGridSpec(
            num_scalar_prefetch=1, grid=(b, npages // ps),
            in_specs=[page_spec(IDX_DIM, ps, j) for j in range(ps)]
            + [fixed((R8, IDX_DIM)), fixed((R8, LANES))],
            out_specs=pl.BlockSpec((1, TQ8, ps * PAGE_SIZE), lambda bi, p, pt: (bi, 0, p))),
        out_shape=jax.ShapeDtypeStruct((b, TQ8, past), I32),
        compiler_params=_cparams("parallel", "arbitrary"),
        name="dsa_sample_scores",
    )(page_table, *([cik_t] * ps), iq_rows, iww)

    nb_seq = SUBLANES if b % SUBLANES == 0 else 1
    seq_spec = lambda r, w: pl.BlockSpec((nb_seq, r, w), lambda bi: (bi, 0, 0))
    thr, jb, keys_new = pl.pallas_call(
        functools.partial(_dsa_sample_thr_kernel, nb_seq=nb_seq, T=T, past=past, k_sel=k_sel),
        grid=(b // nb_seq,),
        in_specs=[seq_spec(TQ8, past), seq_spec(R8, IDX_DIM), seq_spec(R8, LANES),
                  seq_spec(LANES, IDX_DIM)],
        out_specs=[seq_spec(TQ8, LANES)] * 3,
        out_shape=[jax.ShapeDtypeStruct((b, TQ8, LANES), I32)] * 3,
        compiler_params=_cparams("parallel"),
        name="dsa_sample_threshold",
    )(keys_past, iq_rows, iww, ik_new)

    pa = _pages_per_step(npages, 16)
    o = pl.pallas_call(
        functools.partial(_dsa_sample_attn_kernel, n_pages=pa, T=T, past=past),
        grid_spec=pltpu.PrefetchScalarGridSpec(
            num_scalar_prefetch=1, grid=(b, npages // pa),
            in_specs=[page_spec(KV_WIDTH, pa, j) for j in range(pa)] * 2
            + [fixed((R, KV_WIDTH)),
               pl.BlockSpec((1, TQ8, pa * PAGE_SIZE), lambda bi, p, pt: (bi, 0, p)),
               fixed((TQ8, LANES)), fixed((TQ8, LANES)), fixed((TQ8, LANES)),
               fixed((LANES, KV_WIDTH)), fixed((LANES, KV_WIDTH))],
            out_specs=fixed((R, KV_WIDTH)),
            scratch_shapes=[pltpu.VMEM((R, 1), F32), pltpu.VMEM((R, 1), F32),
                            pltpu.VMEM((R, KV_WIDTH), F32)]),
        out_shape=jax.ShapeDtypeStruct((b, R, KV_WIDTH), F32),
        compiler_params=_cparams("parallel", "arbitrary"),
        name="dsa_sample_attention",
    )(page_table, *([ck_t] * pa), *([cv_t] * pa), q_bd, keys_past, thr, jb, keys_new, k_new, v_new)

    o = o.reshape(b, T, ATT_HEADS, ATT_KV_HEADS, ATT_HEAD_DIM)
    o = jnp.sum(o * onehot[None, None, :, :, None], axis=3)
    return o.reshape(b, T, ATT_HEADS * ATT_HEAD_DIM)


def _merge_kernel(ys_ref, ya_ref, yr_ref, g_ref, h_ref, wb_ref, wo_ref, o_ref):
    acc = None
    for n, y_ref in enumerate((ys_ref, ya_ref, yr_ref)):
        pr = _dot(y_ref[...].astype(BF16), wb_ref[n])
        t = jax.nn.sigmoid(g_ref[:, n * D_MODEL:(n + 1) * D_MODEL]) * pr
        acc = t if acc is None else acc + t
    o_ref[...] = h_ref[...] + _dot(acc.astype(BF16), wo_ref[...])


def merge_branches(ys, ya, yr, gates, h, wb, wo):
    M = h.shape[0]
    tm = _pick(M, (256, 128))
    rows = lambda w: pl.BlockSpec((tm, w), lambda i: (i, 0))
    return pl.pallas_call(
        _merge_kernel,
        grid=(M // tm,),
        in_specs=[rows(D_MODEL), rows(D_MODEL), rows(D_MODEL), rows(N_BRANCH * D_MODEL), rows(D_MODEL),
                  pl.BlockSpec((N_BRANCH, D_MODEL, D_MODEL), lambda i: (0, 0, 0)),
                  pl.BlockSpec((D_MODEL, D_MODEL), lambda i: (0, 0))],
        out_specs=rows(D_MODEL),
        out_shape=jax.ShapeDtypeStruct((M, D_MODEL), F32),
        compiler_params=_cparams("parallel"),
        name="merge_branches",
    )(ys, ya, yr, gates, h, wb, wo)


def _ffn_kernel(x_ref, g_ref, wg_ref, wu_ref, wd_ref, o_ref, xn_ref, acc_ref):
    j = pl.program_id(1)

    @pl.when(j == 0)
    def _():
        x = x_ref[...]
        ms = jnp.mean(x * x, axis=-1, keepdims=True)
        xn_ref[...] = (x * lax.rsqrt(ms + EPS) * g_ref[...]).astype(BF16)
        acc_ref[...] = jnp.zeros_like(acc_ref)

    xn = xn_ref[...]
    a = _silu(_dot(xn, wg_ref[...])) * _dot(xn, wu_ref[...])
    acc_ref[...] += _dot(a.astype(BF16), wd_ref[...])

    @pl.when(j == pl.num_programs(1) - 1)
    def _():
        o_ref[...] = x_ref[...] + acc_ref[...]


def ffn_dense(x, g, wg, wu, wd):
    M = x.shape[0]
    tm = _pick(M, (1024, 512, 256, 128))
    tf = 512
    return pl.pallas_call(
        _ffn_kernel,
        grid=(M // tm, D_FF // tf),
        in_specs=[pl.BlockSpec((tm, D_MODEL), lambda i, j: (i, 0)),
                  pl.BlockSpec((1, D_MODEL), lambda i, j: (0, 0)),
                  pl.BlockSpec((D_MODEL, tf), lambda i, j: (0, j)),
                  pl.BlockSpec((D_MODEL, tf), lambda i, j: (0, j)),
                  pl.BlockSpec((tf, D_MODEL), lambda i, j: (j, 0))],
        out_specs=pl.BlockSpec((tm, D_MODEL), lambda i, j: (i, 0)),
        out_shape=jax.ShapeDtypeStruct((M, D_MODEL), F32),
        scratch_shapes=[pltpu.VMEM((tm, D_MODEL), BF16), pltpu.VMEM((tm, D_MODEL), F32)],
        compiler_params=_cparams("parallel", "arbitrary"),
        name="ffn_dense",
    )(x, g.reshape(1, D_MODEL), wg, wu, wd)


def _router_kernel(x_ref, g_ref, wr_ref, gate_ref):
    x = x_ref[...]
    ms = jnp.mean(x * x, axis=-1, keepdims=True)
    xn = x * lax.rsqrt(ms + EPS) * g_ref[...]
    logits = jnp.dot(xn, wr_ref[...], preferred_element_type=F32, precision=lax.Precision.HIGHEST)
    lane = lax.broadcasted_iota(I32, logits.shape, 1)
    logits = jnp.where(lane < N_EXPERTS, logits, -jnp.inf)
    v1 = jnp.max(logits, axis=1, keepdims=True)
    i1 = jnp.min(jnp.where(logits == v1, lane, LANES), axis=1, keepdims=True)
    rest = jnp.where(lane == i1, -jnp.inf, logits)
    v2 = jnp.max(rest, axis=1, keepdims=True)
    i2 = jnp.min(jnp.where(rest == v2, lane, LANES), axis=1, keepdims=True)
    e2 = jnp.exp(v2 - v1)
    w1 = 1.0 / (1.0 + e2)
    w2 = e2 / (1.0 + e2)
    gate_ref[...] = jnp.where(lane == i1, w1, 0.0) + jnp.where(lane == i2, w2, 0.0)


def moe_router(x, g, wr):
    M = x.shape[0]
    tm = _pick(M, (512, 256, 128))
    wr_pad = jnp.pad(wr.astype(F32), ((0, 0), (0, LANES - N_EXPERTS)))
    return pl.pallas_call(
        _router_kernel,
        grid=(M // tm,),
        in_specs=[pl.BlockSpec((tm, D_MODEL), lambda i: (i, 0)),
                  pl.BlockSpec((1, D_MODEL), lambda i: (0, 0)),
                  pl.BlockSpec((D_MODEL, LANES), lambda i: (0, 0))],
        out_specs=pl.BlockSpec((tm, LANES), lambda i: (i, 0)),
        out_shape=jax.ShapeDtypeStruct((M, LANES), F32),
        compiler_params=_cparams("parallel"),
        name="moe_router",
    )(x, g.reshape(1, D_MODEL), wr_pad)


MOE_BLOCK_ROWS = (192, 256, 320, 384, 512)


def _moe_kernel(x_ref, g_ref, gate_ref, gatet_ref, wg_ref, wu_ref, wd_ref, o_ref,
                xn_ref, rkc_ref, rkr_ref, xe_ref, ye_ref, cnt_ref, *, tm, sizes):
    e = pl.program_id(1)
    j = pl.program_id(2)
    last_j = pl.num_programs(2) - 1

    def for_block_size(cnt, body):
        lo = 0
        for s in sizes:
            pl.when((cnt > lo) & (cnt <= s))(functools.partial(body, s))
            lo = s

    @pl.when((e == 0) & (j == 0))
    def _():
        x = x_ref[...]
        ms = jnp.mean(x * x, axis=-1, keepdims=True)
        xn_ref[...] = (x * lax.rsqrt(ms + EPS) * g_ref[...]).astype(BF16)
        o_ref[...] = x
        ti = lax.broadcasted_iota(I32, (tm, tm), 0)
        tj = lax.broadcasted_iota(I32, (tm, tm), 1)
        flags = jnp.where(gate_ref[...] > 0.0, 1.0, 0.0).astype(BF16)
        flags_t = jnp.where(gatet_ref[...] > 0.0, 1.0, 0.0).astype(BF16)
        rkc_ref[...] = _dot(jnp.where(ti > tj, 1.0, 0.0).astype(BF16), flags)
        rkr_ref[...] = _dot(flags_t, jnp.where(ti < tj, 1.0, 0.0).astype(BF16))

    @pl.when(j == 0)
    def _():
        rank_row = rkr_ref[pl.ds(e, 1), :]
        flag_row = gatet_ref[pl.ds(e, 1), :] > 0.0
        cnt = jnp.sum(jnp.where(flag_row, 1, 0))
        cnt_ref[0] = cnt

        def gather(s):
            slot = lax.broadcasted_iota(I32, (s, 1), 0).astype(F32)
            sel = jnp.where((rank_row == slot) & flag_row, 1.0, 0.0).astype(BF16)
            xe_ref[0:s, :] = _dot(sel, xn_ref[...]).astype(BF16)
            ye_ref[0:s, :] = jnp.zeros((s, D_MODEL), F32)

        for_block_size(cnt, gather)

    cnt = cnt_ref[0]

    def expert(s):
        xe = xe_ref[0:s, :]
        a = _silu(_dot(xe, wg_ref[0])) * _dot(xe, wu_ref[0])
        ye_ref[0:s, :] += _dot(a.astype(BF16), wd_ref[0])

    for_block_size(cnt, expert)

    @pl.when(j == last_j)
    def _():
        lane = lax.broadcasted_iota(I32, (tm, LANES), 1)
        gcol = jnp.sum(jnp.where(lane == e, gate_ref[...], 0.0), axis=1, keepdims=True)
        rcol = jnp.sum(jnp.where(lane == e, rkc_ref[...], 0.0), axis=1, keepdims=True)

        def scatter(s):
            slot = lax.broadcasted_iota(I32, (1, s), 1).astype(F32)
            sel_t = jnp.where((rcol == slot) & (gcol > 0.0), 1.0, 0.0).astype(BF16)
            ye = ye_ref[0:s, :]
            hi = ye.astype(BF16)
            lo = (ye - hi.astype(F32)).astype(BF16)
            o_ref[...] += gcol * (_dot(sel_t, hi) + _dot(sel_t, lo))

        for_block_size(cnt, scatter)


def moe_ffn(x, g, gate, wg, wu, wd):
    M = x.shape[0]
    tm = _pick(M, (1024, 512, 256, 128))
    tf = 896
    sizes = tuple(s for s in MOE_BLOCK_ROWS if s < tm) + (tm,)
    kern = functools.partial(_moe_kernel, tm=tm, sizes=sizes)
    return pl.pallas_call(
        kern,
        grid=(M // tm, N_EXPERTS, D_FF // tf),
        in_specs=[pl.BlockSpec((tm, D_MODEL), lambda i, e, j: (i, 0)),
                  pl.BlockSpec((1, D_MODEL), lambda i, e, j: (0, 0)),
                  pl.BlockSpec((tm, LANES), lambda i, e, j: (i, 0)),
                  pl.BlockSpec((LANES, tm), lambda i, e, j: (0, i)),
                  pl.BlockSpec((1, D_MODEL, tf), lambda i, e, j: (e, 0, j)),
                  pl.BlockSpec((1, D_MODEL, tf), lambda i, e, j: (e, 0, j)),
                  pl.BlockSpec((1, tf, D_MODEL), lambda i, e, j: (e, j, 0))],
        out_specs=pl.BlockSpec((tm, D_MODEL), lambda i, e, j: (i, 0)),
        out_shape=jax.ShapeDtypeStruct((M, D_MODEL), F32),
        scratch_shapes=[pltpu.VMEM((tm, D_MODEL), BF16),
                        pltpu.VMEM((tm, LANES), F32), pltpu.VMEM((LANES, tm), F32),
                        pltpu.VMEM((tm, D_MODEL), BF16), pltpu.VMEM((tm, D_MODEL), F32),
                        pltpu.SMEM((1,), I32)],
        compiler_params=pltpu.CompilerParams(
            dimension_semantics=("parallel", "arbitrary", "arbitrary"), vmem_limit_bytes=MOE_VMEM_LIMIT),
        name="moe_ffn",
    )(x, g.reshape(1, D_MODEL), gate, gate.T, wg, wu, wd)


def _rmsnorm_kernel(x_ref, g_ref, o_ref):
    x = x_ref[...]
    ms = jnp.mean(x * x, axis=-1, keepdims=True)
    o_ref[...] = x * lax.rsqrt(ms + EPS) * g_ref[...]


def rmsnorm(x, g):
    M = x.shape[0]
    tm = _pick(M, (1024, 512, 256, 128))
    return pl.pallas_call(
        _rmsnorm_kernel,
        grid=(M // tm,),
        in_specs=[pl.BlockSpec((tm, D_MODEL), lambda i: (i, 0)),
                  pl.BlockSpec((1, D_MODEL), lambda i: (0, 0))],
        out_specs=pl.BlockSpec((tm, D_MODEL), lambda i: (i, 0)),
        out_shape=jax.ShapeDtypeStruct((M, D_MODEL), F32),
        compiler_params=_cparams("parallel"),
        name="final_rmsnorm",
    )(x, g.reshape(1, D_MODEL))


def _split_w_in(w_in):
    offs = np.cumsum((0,) + IN_SPLITS)
    seg = {n: w_in[:, offs[i]:offs[i + 1]] for i, n in enumerate(
        ("z", "xbc", "dt", "q", "k", "v", "iq", "iw", "ik", "rq", "rk", "rv", "rg", "gates"))}
    cat = lambda *names: jnp.concatenate([seg[n] if isinstance(n, str) else n for n in names], axis=1)
    dt_wide = jnp.repeat(seg["dt"], SSM_HEAD_DIM, axis=1)
    small = cat("dt", "iw", jnp.zeros((D_MODEL, LANES - 2 * IDX_HEADS - IDX_DIM), w_in.dtype), "ik")
    q_scaled = seg["q"] * (ATT_HEAD_DIM ** -0.5)
    groups = dict(ssm=cat("z", "xbc", dt_wide), qiq=cat(q_scaled, "iq"), kvs=cat("k", "v", small),
                  ret=cat("rq", "rk", "rv", "rg"), gate=seg["gates"])
    return {n: w.astype(BF16) for n, w in groups.items()}


def _rope_tables(pos):
    half = RET_DK // 2
    inv = ROPE_BASE ** (-jnp.arange(half, dtype=F32) / half)
    ang = pos.astype(F32)[:, None] * inv[None, :]
    cos, sin = jnp.cos(ang), jnp.sin(ang)
    return jnp.concatenate([cos, cos], axis=1), jnp.concatenate([-sin, sin], axis=1)


def _pad_rows(a, L):
    return jnp.pad(a, ((0, 0), (0, L - a.shape[1]), (0, 0)))


def _mixers(h, pos0, conv0, ssm0, ret0, attend, lw):
    b, L, _ = h.shape
    hf = h.reshape(b * L, D_MODEL)
    proj = {n: rms_matmul(hf, lw["norm_g"], w).reshape(b, L, -1)
            for n, w in lw["w_in"].items() if n != "qiq"}
    qiq = rms_matmul_heads(hf, lw["norm_g"], lw["w_in"]["qiq"], ATT_HEAD_DIM)
    k = proj["kvs"][..., :KV_WIDTH]
    v = proj["kvs"][..., KV_WIDTH:2 * KV_WIDTH]
    small = proj["kvs"][..., 2 * KV_WIDTH:]
    ik = small[..., LANES - IDX_DIM:]

    Lp = -(-L // CHUNK) * CHUNK
    last_valid = L - (Lp - CHUNK)
    dtT = jnp.swapaxes(_pad_rows(small[..., :SSM_HEADS], Lp), 1, 2)
    s0 = jnp.transpose(ssm0, (0, 2, 1, 3)).reshape(b, SSM_STATE, SSM_INNER)
    ys, conv_new, s_new = ssd_branch(_pad_rows(proj["ssm"], Lp), dtT, conv0, s0, lw["conv_w"], lw["conv_b"],
                                     lw["dt_bias"], lw["a_log"], lw["d_skip"], lw["ssm_norm_g"], last_valid)
    ssm_new = jnp.transpose(s_new.reshape(b, SSM_STATE, SSM_HEADS, SSM_HEAD_DIM), (0, 2, 1, 3))

    cos2, sin2 = _rope_tables(pos0 + jnp.arange(Lp))
    yr, ret_new = ret_branch(_pad_rows(proj["ret"], Lp), cos2, sin2, ret0, last_valid)

    ya = attend(qiq, k, v, small, ik, b, L)

    ys = ys[:, :L].reshape(b * L, -1)
    yr = yr[:, :L].reshape(b * L, -1)
    out = merge_branches(ys, ya.reshape(b * L, -1), yr, proj["gate"].reshape(b * L, -1), hf,
                         lw["w_branch"], lw["w_out"])
    return out.reshape(b, L, D_MODEL), (k.reshape(b, L, ATT_KV_HEADS, ATT_HEAD_DIM),
                                        v.reshape(b, L, ATT_KV_HEADS, ATT_HEAD_DIM),
                                        ik, ssm_new, conv_new, ret_new)


def _attend_prompt(qiq, k, v, small, ik, b, L):
    k_hm = jnp.transpose(k.reshape(b, L, ATT_KV_HEADS, ATT_HEAD_DIM), (0, 2, 1, 3)).astype(BF16)
    v_t = jnp.transpose(v.reshape(b, L, ATT_KV_HEADS, ATT_HEAD_DIM), (0, 2, 3, 1)).astype(BF16)
    v_t = jnp.concatenate([v_t, jnp.ones((b, ATT_KV_HEADS, V_ROWS - ATT_HEAD_DIM, L), BF16)], axis=2)
    iw_t = jnp.swapaxes(small[..., IDX_HEADS:2 * IDX_HEADS], 1, 2)
    return dsa_prompt(qiq, iw_t, ik.astype(BF16), k_hm, v_t)


def _attend_sample(qiq, k, v, small, ik, b, L, *, ck_t, cv_t, cik_t, layer, page_table):
    rows = lambda a: jnp.transpose(a, (1, 0, 2)).reshape(b, L, ATT_HEADS * ATT_HEAD_DIM)
    iw = small[..., IDX_HEADS:2 * IDX_HEADS]
    return dsa_sample(rows(qiq[:ATT_HEADS]), k, v, rows(qiq[ATT_HEADS:]), iw, ik,
                      ck_t, cv_t, cik_t, layer, page_table)


def _channel_mixer(h, l, p):
    b, L, _ = h.shape
    hf = h.reshape(b * L, D_MODEL)
    j = l // 2
    if l % 2 == 0:
        out = ffn_dense(hf, p["norm_ffn_g"][l], p["w_ffn_gate"][j].astype(BF16),
                        p["w_ffn_up"][j].astype(BF16), p["w_ffn_down"][j].astype(BF16))
    else:
        gate = moe_router(hf, p["norm_ffn_g"][l], p["w_router"][j])
        out = moe_ffn(hf, p["norm_ffn_g"][l], gate, p["w_moe_gate"][j].astype(BF16),
                      p["w_moe_up"][j].astype(BF16), p["w_moe_down"][j].astype(BF16))
    return out.reshape(b, L, D_MODEL)


def kernel(x_prompt, x_sample, cache_k, cache_v, cache_idx_k, state_ssm, state_conv, state_ret,
           page_table, norm_mix_g, w_in, conv_w, conv_b, dt_bias, a_log, d_skip, ssm_norm_g,
           w_branch, w_out, norm_ffn_g, w_ffn_gate, w_ffn_up, w_ffn_down, w_router,
           w_moe_gate, w_moe_up, w_moe_down, final_norm_g):
    depth = w_in.shape[0]
    past = page_table.shape[1] * PAGE_SIZE
    bp = x_prompt.shape[0]
    p = dict(norm_ffn_g=norm_ffn_g, w_ffn_gate=w_ffn_gate, w_ffn_up=w_ffn_up, w_ffn_down=w_ffn_down,
             w_router=w_router, w_moe_gate=w_moe_gate, w_moe_up=w_moe_up, w_moe_down=w_moe_down)
    n_phys = cache_k.shape[1]
    ck_t = jnp.transpose(cache_k, (0, 1, 3, 4, 2)).reshape(depth, n_phys, KV_WIDTH, PAGE_SIZE)
    cv_t = jnp.transpose(cache_v, (0, 1, 3, 4, 2)).reshape(depth, n_phys, KV_WIDTH, PAGE_SIZE)
    cik_t = jnp.transpose(cache_idx_k, (0, 1, 3, 2))
    hp, hs = x_prompt, x_sample
    st_p, st_s = [], []
    for l in range(depth):
        lw = dict(norm_g=norm_mix_g[l], w_in=_split_w_in(w_in[l]), conv_w=conv_w[l], conv_b=conv_b[l],
                  dt_bias=dt_bias[l], a_log=a_log[l], d_skip=d_skip[l], ssm_norm_g=ssm_norm_g[l],
                  w_branch=w_branch[l].astype(BF16), w_out=w_out[l].astype(BF16))
        hp, sp = _mixers(hp, 0,
                         jnp.zeros((bp, CONV_WIDTH - 1, CONV_DIM), F32),
                         jnp.zeros((bp, SSM_HEADS, SSM_STATE, SSM_HEAD_DIM), F32),
                         jnp.zeros((bp, RET_HEADS, RET_DK, RET_DV), F32),
                         _attend_prompt, lw)
        attend_s = functools.partial(_attend_sample, ck_t=ck_t, cv_t=cv_t, cik_t=cik_t, layer=l,
                                     page_table=page_table)
        hs, ss = _mixers(hs, past, state_conv[l], state_ssm[l], state_ret[l], attend_s, lw)
        hp = _channel_mixer(hp, l, p)
        hs = _channel_mixer(hs, l, p)
        st_p.append(sp)
        st_s.append(ss)
    y_prompt = rmsnorm(hp.reshape(-1, D_MODEL), final_norm_g).reshape(hp.shape)
    y_sample = rmsnorm(hs.reshape(-1, D_MODEL), final_norm_g).reshape(hs.shape)
    stack = lambda sts, i: jnp.stack([s[i] for s in sts])
    return (y_prompt, y_sample,
            stack(st_p, 0), stack(st_p, 1), stack(st_p, 2), stack(st_p, 3), stack(st_p, 4), stack(st_p, 5),
            stack(st_s, 0), stack(st_s, 1), stack(st_s, 2), stack(st_s, 3), stack(st_s, 4), stack(st_s, 5))
```

```python
import functools
import math

import jax
import jax.numpy as jnp
import numpy as np
from jax import lax
from jax.experimental import pallas as pl
from jax.experimental.pallas import tpu as pltpu

F32 = jnp.float32
BF16 = jnp.bfloat16
I32 = jnp.int32

D_MODEL = 1024
PAGE_SIZE = 128
SSM_HEADS = 16
SSM_HEAD_DIM = 64
SSM_INNER = 1024
SSM_GROUPS = 2
SSM_STATE = 128
CONV_WIDTH = 4
CONV_DIM = 1536
ATT_HEADS = 16
ATT_KV_HEADS = 4
ATT_HEAD_DIM = 64
KV_WIDTH = 256
IDX_HEADS = 16
IDX_DIM = 64
TOPK_MAX = 256
RET_HEADS = 4
RET_DK = 128
RET_DV = 256
ROPE_BASE = 10000.0
N_BRANCH = 3
D_FF = 3584
N_EXPERTS = 8
EPS = 1e-6
IN_SPLITS = (1024, 1536, 16, 1024, 256, 256, 1024, 16, 64, 512, 512, 1024, 1024, 3072)

LANES = 128
SUBLANES = 8
VMEM_LIMIT = 48 * 1024 * 1024
MOE_VMEM_LIMIT = 56 * 1024 * 1024

CHUNK = 128
INT_MIN = -(2 ** 31)
KEY_POS_INF = 0x7F800000
KEY_NEG_INF = -2139095041
NEG_BIG = -1e30


def _cparams(*sem):
    return pltpu.CompilerParams(dimension_semantics=sem, vmem_limit_bytes=VMEM_LIMIT)


def _dot(a, b):
    return jnp.dot(a, b, preferred_element_type=F32)


def _dot_nt(a, b):
    return lax.dot_general(a, b, (((1,), (1,)), ((), ())), preferred_element_type=F32)


def _dot_tn(a, b):
    return lax.dot_general(a, b, (((0,), (0,)), ((), ())), preferred_element_type=F32)


def _split3(x):
    hi = x.astype(BF16)
    r = x - hi.astype(F32)
    mid = r.astype(BF16)
    lo = (r - mid.astype(F32)).astype(BF16)
    return hi, mid, lo


def _silu(x):
    return x * jax.nn.sigmoid(x)


def _softplus(x):
    return jnp.maximum(x, 0.0) + jnp.log1p(jnp.exp(-jnp.abs(x)))


def _float_key(x):
    x = jnp.where(x == 0.0, 0.0, x)
    b = lax.bitcast_convert_type(x, I32)
    return jnp.where(b >= 0, b, b ^ jnp.int32(0x7FFFFFFF))


def _rms_matmul_kernel(x_ref, g_ref, w_ref, o_ref):
    x = x_ref[...]
    ms = jnp.mean(x * x, axis=-1, keepdims=True)
    xn = (x * lax.rsqrt(ms + EPS) * g_ref[...]).astype(BF16)
    o_ref[...] = _dot(xn, w_ref[...]).astype(o_ref.dtype)


def _rms_matmul_heads_kernel(x_ref, g_ref, w_ref, o_ref, *, n_heads, head_dim):
    x = x_ref[...]
    ms = jnp.mean(x * x, axis=-1, keepdims=True)
    xn = (x * lax.rsqrt(ms + EPS) * g_ref[...]).astype(BF16)
    res = _dot(xn, w_ref[...])
    for h in range(n_heads):
        o_ref[h] = res[:, h * head_dim:(h + 1) * head_dim].astype(o_ref.dtype)


def _pick(n, prefs):
    for p in prefs:
        if n % p == 0:
            return p
    return n


def rms_matmul(x, g, w, out_dtype=F32):
    M, K = x.shape
    N = w.shape[1]
    tm = _pick(M, (512, 256, 128))
    return pl.pallas_call(
        _rms_matmul_kernel,
        grid=(M // tm,),
        in_specs=[pl.BlockSpec((tm, K), lambda i: (i, 0)),
                  pl.BlockSpec((1, K), lambda i: (0, 0)),
                  pl.BlockSpec((K, N), lambda i: (0, 0))],
        out_specs=pl.BlockSpec((tm, N), lambda i: (i, 0)),
        out_shape=jax.ShapeDtypeStruct((M, N), out_dtype),
        compiler_params=_cparams("parallel"),
        name="rms_matmul",
    )(x, g.reshape(1, K), w)


def rms_matmul_heads(x, g, w, head_dim):
    M, K = x.shape
    N = w.shape[1]
    n_heads = N // head_dim
    tm = _pick(M, (512, 256, 128))
    kern = functools.partial(_rms_matmul_heads_kernel, n_heads=n_heads, head_dim=head_dim)
    return pl.pallas_call(
        kern,
        grid=(M // tm,),
        in_specs=[pl.BlockSpec((tm, K), lambda i: (i, 0)),
                  pl.BlockSpec((1, K), lambda i: (0, 0)),
                  pl.BlockSpec((K, N), lambda i: (0, 0))],
        out_specs=pl.BlockSpec((n_heads, tm, head_dim), lambda i: (0, i, 0)),
        out_shape=jax.ShapeDtypeStruct((n_heads, M, head_dim), BF16),
        compiler_params=_cparams("parallel"),
        name="rms_matmul_heads",
    )(x, g.reshape(1, K), w)


def _ssd_kernel(zxd_ref, dtT_ref, conv0_ref, s0_ref, cw_ref, cb_ref, dtb_ref, alog_ref, dsk_ref,
                ng_ref, dtbT_ref, alogT_ref, y_ref, convn_ref, sn_ref, xpad_ref, st_ref,
                *, Q, last_valid, nc):
    c = pl.program_id(1)
    GW = SSM_INNER // SSM_GROUPS

    @pl.when(c == 0)
    def _():
        xpad_ref[0:8, :] = jnp.zeros((8, CONV_DIM), F32)
        xpad_ref[5:8, :] = conv0_ref[0]
        st_ref[...] = s0_ref[0]

    blk = zxd_ref[0]
    z = blk[:, :SSM_INNER]
    xbc = blk[:, SSM_INNER:SSM_INNER + CONV_DIM]
    dtr = blk[:, SSM_INNER + CONV_DIM:]
    xpad_ref[8:8 + Q, :] = xbc
    cw = cw_ref[...]
    conv = (xpad_ref[5:5 + Q, :] * cw[0:1] + xpad_ref[6:6 + Q, :] * cw[1:2]
            + xpad_ref[7:7 + Q, :] * cw[2:3] + xbc * cw[3:4]) + cb_ref[...]
    xc = _silu(conv)
    xs = xc[:, :SSM_INNER]
    Bm = xc[:, SSM_INNER:SSM_INNER + SSM_GROUPS * SSM_STATE]
    Cm = xc[:, SSM_INNER + SSM_GROUPS * SSM_STATE:]

    row = lax.broadcasted_iota(I32, (Q, 1), 0)
    colq = lax.broadcasted_iota(I32, (1, Q), 1)
    assert last_valid == Q or nc == 1
    lv = last_valid
    valid = row < lv
    tril = (lax.broadcasted_iota(I32, (Q, Q), 0) >= lax.broadcasted_iota(I32, (Q, Q), 1))
    tril_b = jnp.where(tril, 1.0, 0.0).astype(BF16)
    triu_b = jnp.where(lax.broadcasted_iota(I32, (Q, Q), 0) <= lax.broadcasted_iota(I32, (Q, Q), 1),
                       1.0, 0.0).astype(BF16)

    nega = -jnp.exp(alog_ref[...])
    dt = _softplus(dtr + dtb_ref[...])
    la = jnp.where(valid, dt * nega, 0.0)
    cum = sum(_dot(tril_b, p) for p in _split3(la))
    laT = jnp.where(colq < lv, _softplus(dtT_ref[0] + dtbT_ref[...]) * (-jnp.exp(alogT_ref[...])), 0.0)
    cumT = sum(_dot(p, triu_b) for p in _split3(laT))

    ecum = jnp.exp(cum)
    cl = cum[lv - 1:lv, :]
    xdt = xs * dt
    xtail = jnp.where(valid, xdt * jnp.exp(cl - cum), 0.0)
    cdecay = jnp.exp(cl)

    lane = lax.broadcasted_iota(I32, (1, LANES), 1)
    y_groups = []
    for g in range(SSM_GROUPS):
        l0 = g * GW
        Cg = Cm[:, g * SSM_STATE:(g + 1) * SSM_STATE].astype(BF16)
        Bg = Bm[:, g * SSM_STATE:(g + 1) * SSM_STATE].astype(BF16)
        G = _dot_nt(Cg, Bg)
        st_g = st_ref[:, l0:l0 + GW]
        inter = _dot(Cg, st_g.astype(BF16)) * ecum[:, l0:l0 + GW]
        local = _dot_tn(Bg, xtail[:, l0:l0 + GW].astype(BF16))
        st_ref[:, l0:l0 + GW] = st_g * cdecay[:, l0:l0 + GW] + local
        pairs = []
        for p in range(GW // LANES):
            xp = xdt[:, l0 + p * LANES:l0 + (p + 1) * LANES]
            acc = None
            for hh in range(2):
                h = (l0 + p * LANES) // SSM_HEAD_DIM + hh
                ccol = cum[:, h * SSM_HEAD_DIM:h * SSM_HEAD_DIM + 1]
                diff = ccol - cumT[h:h + 1, :]
                dm = jnp.exp(jnp.where(tril, diff, -jnp.inf))
                s = (G * dm).astype(BF16)
                half = (lane >= hh * SSM_HEAD_DIM) & (lane < (hh + 1) * SSM_HEAD_DIM)
                part = _dot(s, jnp.where(half, xp, 0.0).astype(BF16))
                acc = part if acc is None else acc + part
            pairs.append(acc)
        y_groups.append(jnp.concatenate(pairs, axis=1) + inter)
    y = jnp.concatenate(y_groups, axis=1)

    y = (y + dsk_ref[...] * xs) * _silu(z)
    outs = []
    for g in range(SSM_GROUPS):
        seg = y[:, g * GW:(g + 1) * GW]
        ms = jnp.mean(seg * seg, axis=-1, keepdims=True)
        outs.append(seg * lax.rsqrt(ms + EPS) * ng_ref[:, g * GW:(g + 1) * GW])
    y_ref[0] = jnp.concatenate(outs, axis=1)

    @pl.when(c == nc - 1)
    def _():
        convn_ref[0] = xpad_ref[5 + last_valid:8 + last_valid, :]
        sn_ref[0] = st_ref[...]

    xpad_ref[0:8, :] = xpad_ref[Q:Q + 8, :]


def ssd_branch(zxd, dtT, conv0, s0, conv_w, conv_b, dt_bias, a_log, d_skip, norm_g, last_valid):
    b, L, W = zxd.shape
    Q = CHUNK
    nc = L // Q
    rep = lambda v: jnp.repeat(v.astype(F32), SSM_HEAD_DIM).reshape(1, SSM_INNER)
    col = lambda v: v.astype(F32).reshape(SSM_HEADS, 1)
    full = lambda shape: pl.BlockSpec(shape, lambda i, c: (0,) * len(shape))
    kern = functools.partial(_ssd_kernel, Q=Q, last_valid=last_valid, nc=nc)
    return pl.pallas_call(
        kern,
        grid=(b, nc),
        in_specs=[pl.BlockSpec((1, Q, W), lambda i, c: (i, c, 0)),
                  pl.BlockSpec((1, SSM_HEADS, Q), lambda i, c: (i, 0, c)),
                  pl.BlockSpec((1, CONV_WIDTH - 1, CONV_DIM), lambda i, c: (i, 0, 0)),
                  pl.BlockSpec((1, SSM_STATE, SSM_INNER), lambda i, c: (i, 0, 0)),
                  full((CONV_WIDTH, CONV_DIM)), full((1, CONV_DIM)),
                  full((1, SSM_INNER)), full((1, SSM_INNER)), full((1, SSM_INNER)), full((1, SSM_INNER)),
                  full((SSM_HEADS, 1)), full((SSM_HEADS, 1))],
        out_specs=[pl.BlockSpec((1, Q, SSM_INNER), lambda i, c: (i, c, 0)),
                   pl.BlockSpec((1, CONV_WIDTH - 1, CONV_DIM), lambda i, c: (i, 0, 0)),
                   pl.BlockSpec((1, SSM_STATE, SSM_INNER), lambda i, c: (i, 0, 0))],
        out_shape=[jax.ShapeDtypeStruct((b, L, SSM_INNER), F32),
                   jax.ShapeDtypeStruct((b, CONV_WIDTH - 1, CONV_DIM), F32),
                   jax.ShapeDtypeStruct((b, SSM_STATE, SSM_INNER), F32)],
        scratch_shapes=[pltpu.VMEM((Q + 8, CONV_DIM), F32), pltpu.VMEM((SSM_STATE, SSM_INNER), F32)],
        compiler_params=_cparams("parallel", "arbitrary"),
        name="ssd_branch",
    )(zxd, dtT, conv0, s0, conv_w.astype(F32), conv_b.reshape(1, CONV_DIM).astype(F32),
      rep(dt_bias), rep(a_log), rep(d_skip), norm_g.reshape(1, SSM_INNER).astype(F32),
      col(dt_bias), col(a_log))


def _ret_kernel(x_ref, cos_ref, sin_ref, r0_ref, y_ref, rn_ref, st_ref, *, Q, last_valid, nc):
    c = pl.program_id(1)

    @pl.when(c == 0)
    def _():
        st_ref[...] = r0_ref[0]

    blk = x_ref[0]
    cos2 = cos_ref[...]
    sin2 = sin_ref[...]
    ri = lax.broadcasted_iota(I32, (Q, Q), 0)
    ci = lax.broadcasted_iota(I32, (Q, Q), 1)
    dij = (ri - ci).astype(F32)
    row = lax.broadcasted_iota(I32, (Q, 1), 0)
    assert last_valid == Q or nc == 1
    lv = last_valid
    rowf = row.astype(F32)
    lvf = float(lv)
    KO = RET_HEADS * RET_DK
    outs = []
    for h in range(RET_HEADS):
        lg = math.log1p(-2.0 ** (-5.0 - h))
        qh = blk[:, h * RET_DK:(h + 1) * RET_DK]
        kh = blk[:, KO + h * RET_DK:KO + (h + 1) * RET_DK]
        vh = blk[:, 2 * KO + h * RET_DV:2 * KO + (h + 1) * RET_DV].astype(BF16)
        gh = blk[:, 2 * KO + RET_HEADS * RET_DV + h * RET_DV:2 * KO + RET_HEADS * RET_DV + (h + 1) * RET_DV]
        qr = qh * cos2 + pltpu.roll(qh, RET_DK // 2, 1) * sin2
        kr = (kh * cos2 + pltpu.roll(kh, RET_DK // 2, 1) * sin2) * (RET_DK ** -0.5)
        qb = qr.astype(BF16)
        dm = jnp.exp(jnp.where(ri >= ci, dij * lg, -jnp.inf))
        s = (_dot_nt(qb, kr.astype(BF16)) * dm).astype(BF16)
        st_h = st_ref[h]
        y = _dot(s, vh) + jnp.exp((rowf + 1.0) * lg) * _dot(qb, st_h.astype(BF16))
        ktail = jnp.where(row < lv, kr * jnp.exp((lvf - 1.0 - rowf) * lg), 0.0)
        st_ref[h] = st_h * math.exp(lvf * lg) + _dot_tn(ktail.astype(BF16), vh)
        ms = jnp.mean(y * y, axis=-1, keepdims=True)
        outs.append(y * lax.rsqrt(ms + EPS) * _silu(gh))
    y_ref[0] = jnp.concatenate(outs, axis=1)

    @pl.when(c == nc - 1)
    def _():
        rn_ref[0] = st_ref[...]


def ret_branch(x, cos2, sin2, r0, last_valid):
    b, L, W = x.shape
    Q = CHUNK
    nc = L // Q
    kern = functools.partial(_ret_kernel, Q=Q, last_valid=last_valid, nc=nc)
    return pl.pallas_call(
        kern,
        grid=(b, nc),
        in_specs=[pl.BlockSpec((1, Q, W), lambda i, c: (i, c, 0)),
                  pl.BlockSpec((Q, RET_DK), lambda i, c: (c, 0)),
                  pl.BlockSpec((Q, RET_DK), lambda i, c: (c, 0)),
                  pl.BlockSpec((1, RET_HEADS, RET_DK, RET_DV), lambda i, c: (i, 0, 0, 0))],
        out_specs=[pl.BlockSpec((1, Q, RET_HEADS * RET_DV), lambda i, c: (i, c, 0)),
                   pl.BlockSpec((1, RET_HEADS, RET_DK, RET_DV), lambda i, c: (i, 0, 0, 0))],
        out_shape=[jax.ShapeDtypeStruct((b, L, RET_HEADS * RET_DV), F32),
                   jax.ShapeDtypeStruct((b, RET_HEADS, RET_DK, RET_DV), F32)],
        scratch_shapes=[pltpu.VMEM((RET_HEADS, RET_DK, RET_DV), F32)],
        compiler_params=_cparams("parallel", "arbitrary"),
        name="ret_branch",
    )(x, cos2, sin2, r0)


def _kth_largest_key(count_ge, k, shape):
    def body(t, prefix):
        bit = lax.shift_left(jnp.int32(1), jnp.int32(31) - t)
        cand = prefix | bit
        cnt = count_ge(cand ^ jnp.int32(INT_MIN))
        return jnp.where(cnt >= k, cand, prefix)

    prefix = lax.fori_loop(0, 32, body, jnp.zeros(shape, I32))
    return prefix ^ jnp.int32(INT_MIN)


def _dsa_prompt_kernel(q_ref, iq_ref, iwt_ref, ik_ref, k_ref, vt_ref, o_ref, keys_ref, jb_ref,
                       m_ref, acc_ref, *, tq, ck, cka, L, k_sel):
    i = pl.program_id(1)
    G = ATT_HEADS // ATT_KV_HEADS
    PART = 4 * SUBLANES
    nk = lax.div((i + 1) * tq + (ck - 1), ck)
    qpos = i * tq + lax.broadcasted_iota(I32, (1, tq), 1)
    rowk = lax.broadcasted_iota(I32, (ck, 1), 0)
    iw = iwt_ref[0] * (IDX_HEADS ** -0.5 * IDX_DIM ** -0.5)

    def score_body(kc, carry):
        off = pl.multiple_of(kc * ck, ck)
        ikc = ik_ref[0, pl.ds(off, ck), :]
        acc = jnp.zeros((ck, tq), F32)
        for h4 in range(0, IDX_HEADS, G):
            d = _dot_nt(ikc, iq_ref[h4:h4 + G].reshape(G * tq, IDX_DIM))
            for j in range(G):
                acc = acc + jnp.maximum(d[:, j * tq:(j + 1) * tq], 0.0) * iw[h4 + j:h4 + j + 1, :]
        key = jnp.where(off + rowk <= qpos, _float_key(acc), jnp.int32(INT_MIN))
        keys_ref[pl.ds(off, ck), :] = key
        return carry

    lax.fori_loop(0, nk, score_body, 0)

    def count(pred):
        def body(kc, part):
            off = pl.multiple_of(kc * ck, ck)
            hit = jnp.where(pred(keys_ref[pl.ds(off, ck), :], off + rowk), 1.0, 0.0)
            return part + jnp.sum(hit.reshape(ck // PART, PART, tq), axis=0)
        part = lax.fori_loop(0, nk, body, jnp.zeros((PART, tq), F32))
        return jnp.sum(part, axis=0, keepdims=True)

    kf = float(k_sel)
    thr = _kth_largest_key(lambda cand: count(lambda kk, pos: kk >= cand), kf, (1, tq))
    thr = jnp.maximum(thr, jnp.int32(KEY_NEG_INF + 1))
    n_ge = count(lambda kk, pos: kk >= thr)
    jb_ref[...] = jnp.full((1, tq), L, I32)

    @pl.when(jnp.max(n_ge) > kf)
    def _():
        need = kf - count(lambda kk, pos: kk > thr)
        nb = max(1, (L - 1).bit_length())

        def body(t, pfx):
            cand = pfx | lax.shift_left(jnp.int32(1), jnp.int32(nb - 1) - t)
            cnt = count(lambda kk, pos: (kk == thr) & (pos < cand))
            return jnp.where(cnt < need, cand, pfx)

        pfx = lax.fori_loop(0, nb, body, jnp.zeros((1, tq), I32))
        jb_ref[...] = jnp.where(n_ge > kf, pfx + 1, L)

    jb = jb_ref[...]

    m_ref[...] = jnp.full(m_ref.shape, NEG_BIG, F32)
    acc_ref[...] = jnp.zeros(acc_ref.shape, F32)
    rowa = lax.broadcasted_iota(I32, (cka, 1), 0)

    def att_body(kc, carry):
        off = pl.multiple_of(kc * cka, cka)
        kk = keys_ref[pl.ds(off, cka), :]
        sel = (kk >= thr) & (kk < jnp.int32(KEY_POS_INF)) & ((kk > thr) | (off + rowa < jb))
        bias = jnp.where(sel, 0.0, -jnp.inf)
        bias = jnp.concatenate([bias] * ATT_HEADS, axis=1)
        s = jnp.concatenate(
            [_dot_nt(k_ref[0, g, pl.ds(off, cka), :], q_ref[g * G:(g + 1) * G].reshape(G * tq, ATT_HEAD_DIM))
             for g in range(ATT_KV_HEADS)], axis=1) + bias
        m = m_ref[...]
        smax = jnp.max(jnp.max(s.reshape(cka // PART, PART, ATT_HEADS * tq), axis=0), axis=0, keepdims=True)
        m_new = jnp.maximum(m, smax)
        p = jnp.exp(s - m_new).astype(BF16)
        pv = jnp.concatenate(
            [_dot(vt_ref[0, g, :, pl.ds(off, cka)], p[:, g * G * tq:(g + 1) * G * tq])
             for g in range(ATT_KV_HEADS)], axis=1)
        acc_ref[...] = jnp.exp(m - m_new) * acc_ref[...] + pv
        m_ref[...] = m_new
        return carry

    lax.fori_loop(0, lax.div((i + 1) * tq + (cka - 1), cka), att_body, 0)
    acc = acc_ref[...]
    o = acc[:ATT_HEAD_DIM] / acc[ATT_HEAD_DIM:ATT_HEAD_DIM + 1]
    for hp in range(ATT_HEADS // 2):
        pair = jnp.concatenate([o[:, (2 * hp) * tq:(2 * hp + 1) * tq],
                                o[:, (2 * hp + 1) * tq:(2 * hp + 2) * tq]], axis=0)
        o_ref[0, :, hp * LANES:(hp + 1) * LANES] = pair.T


V_ROWS = ATT_HEAD_DIM + 16


def dsa_prompt(qiq, iw_t, ik, k_hm, v_t):
    b, _, L, _ = k_hm.shape
    tq = min(128, L)
    nq = L // tq
    ck = min(512, L)
    cka = ck
    gq = (ATT_HEADS // ATT_KV_HEADS) * tq
    k_sel = max(1, min(TOPK_MAX, L // 4))
    kern = functools.partial(_dsa_prompt_kernel, tq=tq, ck=ck, cka=cka, L=L, k_sel=k_sel)
    return pl.pallas_call(
        kern,
        grid=(b, L // tq),
        in_specs=[pl.BlockSpec((ATT_HEADS, tq, ATT_HEAD_DIM), lambda bi, i: (0, bi * nq + i, 0)),
                  pl.BlockSpec((IDX_HEADS, tq, IDX_DIM), lambda bi, i: (1, bi * nq + i, 0)),
                  pl.BlockSpec((1, IDX_HEADS, tq), lambda bi, i: (bi, 0, i)),
                  pl.BlockSpec((1, L, IDX_DIM), lambda bi, i: (bi, 0, 0)),
                  pl.BlockSpec((1, ATT_KV_HEADS, L, ATT_HEAD_DIM), lambda bi, i: (bi, 0, 0, 0)),
                  pl.BlockSpec((1, ATT_KV_HEADS, V_ROWS, L), lambda bi, i: (bi, 0, 0, 0))],
        out_specs=pl.BlockSpec((1, tq, ATT_HEADS * ATT_HEAD_DIM), lambda bi, i: (bi, i, 0)),
        out_shape=jax.ShapeDtypeStruct((b, L, ATT_HEADS * ATT_HEAD_DIM), F32),
        scratch_shapes=[pltpu.VMEM((L, tq), I32), pltpu.VMEM((1, tq), I32),
                        pltpu.VMEM((1, ATT_HEADS * tq), F32),
                        pltpu.VMEM((V_ROWS, ATT_HEADS * tq), F32)],
        compiler_params=_cparams("parallel", "arbitrary"),
        name="dsa_prompt",
    )(qiq, qiq, iw_t, ik, k_hm, v_t)


TQ8 = SUBLANES


def _sample_score_keys(d, iww):
    n = d.shape[1]
    r = jnp.maximum(d, 0.0) * (iww[:, 0:1] * (IDX_HEADS ** -0.5 * IDX_DIM ** -0.5))
    return _float_key(jnp.sum(r.reshape(TQ8, IDX_HEADS, n), axis=1))


def _dsa_sample_scores_kernel(pt_ref, *refs, n_pages):
    page_refs = refs[:n_pages]
    iq_ref, iww_ref, keys_ref = refs[n_pages:]
    ik_t = jnp.concatenate([r[0, 0] for r in page_refs], axis=1).astype(BF16)
    keys_ref[0] = _sample_score_keys(_dot(iq_ref[0], ik_t), iww_ref[0])


def _dsa_sample_thr_kernel(kp_ref, iq_ref, iww_ref, ikn_ref, thr_ref, jb_ref, kn_ref,
                           *, nb_seq, T, past, k_sel):
    R8 = nb_seq * TQ8
    kp = kp_ref[...].reshape(R8, past)
    rowi = lax.broadcasted_iota(I32, (TQ8, LANES), 0)
    col1 = lax.broadcasted_iota(I32, (TQ8, LANES), 1)
    kn = jnp.concatenate(
        [jnp.where((col1 <= rowi) & (col1 < T),
                   _sample_score_keys(_dot_nt(iq_ref[s], ikn_ref[s]), iww_ref[s]), jnp.int32(INT_MIN))
         for s in range(nb_seq)], axis=0)
    coln = lax.broadcasted_iota(I32, (R8, LANES), 1)
    colp = lax.broadcasted_iota(I32, (R8, past), 1)

    def count(pred):
        hit = jnp.where(pred(kp, colp), 1.0, 0.0)
        part = jnp.where(pred(kn, coln + past), 1.0, 0.0)
        for t in range(past // LANES):
            part = part + hit[:, t * LANES:(t + 1) * LANES]
        return jnp.sum(part, axis=1, keepdims=True)

    kf = float(k_sel)
    thr = _kth_largest_key(lambda cand: count(lambda kk, col: kk >= cand), kf, (R8, 1))
    thr = jnp.maximum(thr, jnp.int32(KEY_NEG_INF + 1))
    need = kf - count(lambda kk, col: kk > thr)
    nb = (past + LANES - 1).bit_length()

    def body(t, pfx):
        cand = pfx | lax.shift_left(jnp.int32(1), jnp.int32(nb - 1) - t)
        cnt = count(lambda kk, col: (kk == thr) & (col < cand))
        return jnp.where(cnt < need, cand, pfx)

    pfx = lax.fori_loop(0, nb, body, jnp.zeros((R8, 1), I32))
    thr_ref[...] = jnp.broadcast_to(thr, (R8, LANES)).reshape(nb_seq, TQ8, LANES)
    jb_ref[...] = jnp.broadcast_to(pfx + 1, (R8, LANES)).reshape(nb_seq, TQ8, LANES)
    kn_ref[...] = kn.reshape(nb_seq, TQ8, LANES)


def _dsa_sample_attn_kernel(pt_ref, *refs, n_pages, T, past):
    k_refs = refs[:n_pages]
    v_refs = refs[n_pages:2 * n_pages]
    (q_ref, kp_ref, thr_ref, jb_ref, kn_ref, knew_ref, vnew_ref,
     o_ref, m_ref, l_ref, acc_ref) = refs[2 * n_pages:]
    p = pl.program_id(1)
    R = T * ATT_HEADS

    @pl.when(p == 0)
    def _():
        m_ref[...] = jnp.full((R, 1), NEG_BIG, F32)
        l_ref[...] = jnp.zeros((R, 1), F32)
        acc_ref[...] = jnp.zeros((R, KV_WIDTH), F32)

    thr = thr_ref[0][0:T, 0:1]
    jb = jb_ref[0][0:T, 0:1]

    def step(kk, col0, s, pv):
        n = kk.shape[1]
        col = col0 + lax.broadcasted_iota(I32, (1, n), 1)
        sel = (kk >= thr) & (kk < jnp.int32(KEY_POS_INF)) & ((kk > thr) | (col < jb))
        bias = jnp.where(sel, 0.0, -jnp.inf)
        s = (s.reshape(T, ATT_HEADS, n) + bias[:, None, :]).reshape(R, n)
        m = m_ref[...]
        m_new = jnp.maximum(m, jnp.max(s, axis=1, keepdims=True))
        alpha = jnp.exp(m - m_new)
        pr = jnp.exp(s - m_new)
        l_ref[...] = alpha * l_ref[...] + jnp.sum(pr, axis=1, keepdims=True)
        acc_ref[...] = alpha * acc_ref[...] + pv(pr.astype(BF16))
        m_ref[...] = m_new

    k_t = jnp.concatenate([r[0, 0] for r in k_refs], axis=1).astype(BF16)
    v_t = jnp.concatenate([r[0, 0] for r in v_refs], axis=1).astype(BF16)
    step(kp_ref[0][0:T], p * (n_pages * PAGE_SIZE), _dot(q_ref[0], k_t), lambda pr: _dot_nt(pr, v_t))

    @pl.when(p == pl.num_programs(1) - 1)
    def _():
        step(kn_ref[0][0:T], past, _dot_nt(q_ref[0], knew_ref[0]), lambda pr: _dot(pr, vnew_ref[0]))
        o_ref[0] = acc_ref[...] / l_ref[...]


def _pages_per_step(npages, want):
    while npages % want:
        want //= 2
    return want


def dsa_sample(q, k, v, iq, iw, ik, ck_t, cv_t, cik_t, layer, page_table):
    b, T, _ = q.shape
    assert T <= TQ8
    npages = page_table.shape[1]
    past = npages * PAGE_SIZE
    k_sel = max(1, min(TOPK_MAX, (past + T) // 4))
    R = T * ATT_HEADS
    R8 = TQ8 * IDX_HEADS
    pad_to = lambda a, n: jnp.pad(a, ((0, 0), (0, n - a.shape[1]), (0, 0)))
    iq_rows = pad_to(iq.reshape(b, T * IDX_HEADS, IDX_DIM), R8).astype(BF16)
    iww = jnp.broadcast_to(pad_to(iw.reshape(b, T * IDX_HEADS, 1), R8), (b, R8, LANES)).astype(F32)
    ik_new, k_new, v_new = (pad_to(a, LANES).astype(BF16) for a in (ik, k, v))
    head_group = jnp.arange(ATT_HEADS) // (ATT_HEADS // ATT_KV_HEADS)
    onehot = (head_group[:, None] == jnp.arange(ATT_KV_HEADS)[None, :]).astype(F32)
    q_bd = (q.reshape(b, T, ATT_HEADS, 1, ATT_HEAD_DIM) * onehot[None, None, :, :, None])
    q_bd = q_bd.reshape(b, R, KV_WIDTH).astype(BF16)

    def page_spec(rows, per_step, j):
        return pl.BlockSpec((1, 1, rows, PAGE_SIZE),
                            lambda bi, p, pt: (layer, pt[bi, p * per_step + j], 0, 0))

    fixed = lambda shape: pl.BlockSpec((1,) + shape, lambda bi, p, pt: (bi, 0, 0))

    ps = _pages_per_step(npages, 64)
    keys_past = pl.pallas_call(
        functools.partial(_dsa_sample_scores_kernel, n_pages=ps),
        grid_spec=pltpu.PrefetchScalarGridSpec(
            num_scalar_prefetch=1, grid=(b, npages // ps),
            in_specs=[page_spec(IDX_DIM, ps, j) for j in range(ps)]
            + [fixed((R8, IDX_DIM)), fixed((R8, LANES))],
            out_specs=pl.BlockSpec((1, TQ8, ps * PAGE_SIZE), lambda bi, p, pt: (bi, 0, p))),
        out_shape=jax.ShapeDtypeStruct((b, TQ8, past), I32),
        compiler_params=_cparams("parallel", "arbitrary"),
        name="dsa_sample_scores",
    )(page_table, *([cik_t] * ps), iq_rows, iww)

    nb_seq = SUBLANES if b % SUBLANES == 0 else 1
    seq_spec = lambda r, w: pl.BlockSpec((nb_seq, r, w), lambda bi: (bi, 0, 0))
    thr, jb, keys_new = pl.pallas_call(
        functools.partial(_dsa_sample_thr_kernel, nb_seq=nb_seq, T=T, past=past, k_sel=k_sel),
        grid=(b // nb_seq,),
        in_specs=[seq_spec(TQ8, past), seq_spec(R8, IDX_DIM), seq_spec(R8, LANES),
                  seq_spec(LANES, IDX_DIM)],
        out_specs=[seq_spec(TQ8, LANES)] * 3,
        out_shape=[jax.ShapeDtypeStruct((b, TQ8, LANES), I32)] * 3,
        compiler_params=_cparams("parallel"),
        name="dsa_sample_threshold",
    )(keys_past, iq_rows, iww, ik_new)

    pa = _pages_per_step(npages, 32)
    o = pl.pallas_call(
        functools.partial(_dsa_sample_attn_kernel, n_pages=pa, T=T, past=past),
        grid_spec=pltpu.PrefetchScalarGridSpec(
            num_scalar_prefetch=1, grid=(b, npages // pa),
            in_specs=[page_spec(KV_WIDTH, pa, j) for j in range(pa)] * 2
            + [fixed((R, KV_WIDTH)),
               pl.BlockSpec((1, TQ8, pa * PAGE_SIZE), lambda bi, p, pt: (bi, 0, p)),
               fixed((TQ8, LANES)), fixed((TQ8, LANES)), fixed((TQ8, LANES)),
               fixed((LANES, KV_WIDTH)), fixed((LANES, KV_WIDTH))],
            out_specs=fixed((R, KV_WIDTH)),
            scratch_shapes=[pltpu.VMEM((R, 1), F32), pltpu.VMEM((R, 1), F32),
                            pltpu.VMEM((R, KV_WIDTH), F32)]),
        out_shape=jax.ShapeDtypeStruct((b, R, KV_WIDTH), F32),
        compiler_params=_cparams("parallel", "arbitrary"),
        name="dsa_sample_attention",
    )(page_table, *([ck_t] * pa), *([cv_t] * pa), q_bd, keys_past, thr, jb, keys_new, k_new, v_new)

    o = o.reshape(b, T, ATT_HEADS, ATT_KV_HEADS, ATT_HEAD_DIM)
    o = jnp.sum(o * onehot[None, None, :, :, None], axis=3)
    return o.reshape(b, T, ATT_HEADS * ATT_HEAD_DIM)


def _merge_kernel(ys_ref, ya_ref, yr_ref, g_ref, h_ref, wb_ref, wo_ref, o_ref):
    acc = None
    for n, y_ref in enumerate((ys_ref, ya_ref, yr_ref)):
        pr = _dot(y_ref[...].astype(BF16), wb_ref[n])
        t = jax.nn.sigmoid(g_ref[:, n * D_MODEL:(n + 1) * D_MODEL]) * pr
        acc = t if acc is None else acc + t
    o_ref[...] = h_ref[...] + _dot(acc.astype(BF16), wo_ref[...])


def merge_branches(ys, ya, yr, gates, h, wb, wo):
    M = h.shape[0]
    tm = _pick(M, (256, 128))
    rows = lambda w: pl.BlockSpec((tm, w), lambda i: (i, 0))
    return pl.pallas_call(
        _merge_kernel,
        grid=(M // tm,),
        in_specs=[rows(D_MODEL), rows(D_MODEL), rows(D_MODEL), rows(N_BRANCH * D_MODEL), rows(D_MODEL),
                  pl.BlockSpec((N_BRANCH, D_MODEL, D_MODEL), lambda i: (0, 0, 0)),
                  pl.BlockSpec((D_MODEL, D_MODEL), lambda i: (0, 0))],
        out_specs=rows(D_MODEL),
        out_shape=jax.ShapeDtypeStruct((M, D_MODEL), F32),
        compiler_params=_cparams("parallel"),
        name="merge_branches",
    )(ys, ya, yr, gates, h, wb, wo)


def _ffn_kernel(x_ref, g_ref, wg_ref, wu_ref, wd_ref, o_ref, xn_ref, acc_ref):
    j = pl.program_id(1)

    @pl.when(j == 0)
    def _():
        x = x_ref[...]
        ms = jnp.mean(x * x, axis=-1, keepdims=True)
        xn_ref[...] = (x * lax.rsqrt(ms + EPS) * g_ref[...]).astype(BF16)
        acc_ref[...] = jnp.zeros_like(acc_ref)

    xn = xn_ref[...]
    a = _silu(_dot(xn, wg_ref[...])) * _dot(xn, wu_ref[...])
    acc_ref[...] += _dot(a.astype(BF16), wd_ref[...])

    @pl.when(j == pl.num_programs(1) - 1)
    def _():
        o_ref[...] = x_ref[...] + acc_ref[...]


def ffn_dense(x, g, wg, wu, wd):
    M = x.shape[0]
    tm = _pick(M, (1024, 512, 256, 128))
    tf = 512
    return pl.pallas_call(
        _ffn_kernel,
        grid=(M // tm, D_FF // tf),
        in_specs=[pl.BlockSpec((tm, D_MODEL), lambda i, j: (i, 0)),
                  pl.BlockSpec((1, D_MODEL), lambda i, j: (0, 0)),
                  pl.BlockSpec((D_MODEL, tf), lambda i, j: (0, j)),
                  pl.BlockSpec((D_MODEL, tf), lambda i, j: (0, j)),
                  pl.BlockSpec((tf, D_MODEL), lambda i, j: (j, 0))],
        out_specs=pl.BlockSpec((tm, D_MODEL), lambda i, j: (i, 0)),
        out_shape=jax.ShapeDtypeStruct((M, D_MODEL), F32),
        scratch_shapes=[pltpu.VMEM((tm, D_MODEL), BF16), pltpu.VMEM((tm, D_MODEL), F32)],
        compiler_params=_cparams("parallel", "arbitrary"),
        name="ffn_dense",
    )(x, g.reshape(1, D_MODEL), wg, wu, wd)


def _router_kernel(x_ref, g_ref, wr_ref, gate_ref):
    x = x_ref[...]
    ms = jnp.mean(x * x, axis=-1, keepdims=True)
    xn = x * lax.rsqrt(ms + EPS) * g_ref[...]
    logits = jnp.dot(xn, wr_ref[...], preferred_element_type=F32, precision=lax.Precision.HIGHEST)
    lane = lax.broadcasted_iota(I32, logits.shape, 1)
    logits = jnp.where(lane < N_EXPERTS, logits, -jnp.inf)
    v1 = jnp.max(logits, axis=1, keepdims=True)
    i1 = jnp.min(jnp.where(logits == v1, lane, LANES), axis=1, keepdims=True)
    rest = jnp.where(lane == i1, -jnp.inf, logits)
    v2 = jnp.max(rest, axis=1, keepdims=True)
    i2 = jnp.min(jnp.where(rest == v2, lane, LANES), axis=1, keepdims=True)
    e2 = jnp.exp(v2 - v1)
    w1 = 1.0 / (1.0 + e2)
    w2 = e2 / (1.0 + e2)
    gate_ref[...] = jnp.where(lane == i1, w1, 0.0) + jnp.where(lane == i2, w2, 0.0)


def moe_router(x, g, wr):
    M = x.shape[0]
    tm = _pick(M, (512, 256, 128))
    wr_pad = jnp.pad(wr.astype(F32), ((0, 0), (0, LANES - N_EXPERTS)))
    return pl.pallas_call(
        _router_kernel,
        grid=(M // tm,),
        in_specs=[pl.BlockSpec((tm, D_MODEL), lambda i: (i, 0)),
                  pl.BlockSpec((1, D_MODEL), lambda i: (0, 0)),
                  pl.BlockSpec((D_MODEL, LANES), lambda i: (0, 0))],
        out_specs=pl.BlockSpec((tm, LANES), lambda i: (i, 0)),
        out_shape=jax.ShapeDtypeStruct((M, LANES), F32),
        compiler_params=_cparams("parallel"),
        name="moe_router",
    )(x, g.reshape(1, D_MODEL), wr_pad)


MOE_BLOCK_ROWS = (192, 256, 320, 384, 512)


def _moe_kernel(x_ref, g_ref, gate_ref, gatet_ref, wg_ref, wu_ref, wd_ref, o_ref,
                xn_ref, rkc_ref, rkr_ref, xe_ref, ye_ref, cnt_ref, *, tm, sizes):
    e = pl.program_id(1)
    j = pl.program_id(2)
    last_j = pl.num_programs(2) - 1

    def for_block_size(cnt, body):
        lo = 0
        for s in sizes:
            pl.when((cnt > lo) & (cnt <= s))(functools.partial(body, s))
            lo = s

    @pl.when((e == 0) & (j == 0))
    def _():
        x = x_ref[...]
        ms = jnp.mean(x * x, axis=-1, keepdims=True)
        xn_ref[...] = (x * lax.rsqrt(ms + EPS) * g_ref[...]).astype(BF16)
        o_ref[...] = x
        ti = lax.broadcasted_iota(I32, (tm, tm), 0)
        tj = lax.broadcasted_iota(I32, (tm, tm), 1)
        flags = jnp.where(gate_ref[...] > 0.0, 1.0, 0.0).astype(BF16)
        flags_t = jnp.where(gatet_ref[...] > 0.0, 1.0, 0.0).astype(BF16)
        rkc_ref[...] = _dot(jnp.where(ti > tj, 1.0, 0.0).astype(BF16), flags)
        rkr_ref[...] = _dot(flags_t, jnp.where(ti < tj, 1.0, 0.0).astype(BF16))

    @pl.when(j == 0)
    def _():
        rank_row = rkr_ref[pl.ds(e, 1), :]
        flag_row = gatet_ref[pl.ds(e, 1), :] > 0.0
        cnt = jnp.sum(jnp.where(flag_row, 1, 0))
        cnt_ref[0] = cnt

        def gather(s):
            slot = lax.broadcasted_iota(I32, (s, 1), 0).astype(F32)
            sel = jnp.where((rank_row == slot) & flag_row, 1.0, 0.0).astype(BF16)
            xe_ref[0:s, :] = _dot(sel, xn_ref[...]).astype(BF16)
            ye_ref[0:s, :] = jnp.zeros((s, D_MODEL), F32)

        for_block_size(cnt, gather)

    cnt = cnt_ref[0]

    def expert(s):
        xe = xe_ref[0:s, :]
        a = _silu(_dot(xe, wg_ref[0])) * _dot(xe, wu_ref[0])
        ye_ref[0:s, :] += _dot(a.astype(BF16), wd_ref[0])

    for_block_size(cnt, expert)

    @pl.when(j == last_j)
    def _():
        lane = lax.broadcasted_iota(I32, (tm, LANES), 1)
        gcol = jnp.sum(jnp.where(lane == e, gate_ref[...], 0.0), axis=1, keepdims=True)
        rcol = jnp.sum(jnp.where(lane == e, rkc_ref[...], 0.0), axis=1, keepdims=True)

        def scatter(s):
            slot = lax.broadcasted_iota(I32, (1, s), 1).astype(F32)
            sel_t = jnp.where((rcol == slot) & (gcol > 0.0), 1.0, 0.0).astype(BF16)
            ye = ye_ref[0:s, :]
            hi = ye.astype(BF16)
            lo = (ye - hi.astype(F32)).astype(BF16)
            o_ref[...] += gcol * (_dot(sel_t, hi) + _dot(sel_t, lo))

        for_block_size(cnt, scatter)


def moe_ffn(x, g, gate, wg, wu, wd):
    M = x.shape[0]
    tm = _pick(M, (1024, 512, 256, 128))
    tf = 896
    sizes = tuple(s for s in MOE_BLOCK_ROWS if s < tm) + (tm,)
    kern = functools.partial(_moe_kernel, tm=tm, sizes=sizes)
    return pl.pallas_call(
        kern,
        grid=(M // tm, N_EXPERTS, D_FF // tf),
        in_specs=[pl.BlockSpec((tm, D_MODEL), lambda i, e, j: (i, 0)),
                  pl.BlockSpec((1, D_MODEL), lambda i, e, j: (0, 0)),
                  pl.BlockSpec((tm, LANES), lambda i, e, j: (i, 0)),
                  pl.BlockSpec((LANES, tm), lambda i, e, j: (0, i)),
                  pl.BlockSpec((1, D_MODEL, tf), lambda i, e, j: (e, 0, j)),
                  pl.BlockSpec((1, D_MODEL, tf), lambda i, e, j: (e, 0, j)),
                  pl.BlockSpec((1, tf, D_MODEL), lambda i, e, j: (e, j, 0))],
        out_specs=pl.BlockSpec((tm, D_MODEL), lambda i, e, j: (i, 0)),
        out_shape=jax.ShapeDtypeStruct((M, D_MODEL), F32),
        scratch_shapes=[pltpu.VMEM((tm, D_MODEL), BF16),
                        pltpu.VMEM((tm, LANES), F32), pltpu.VMEM((LANES, tm), F32),
                        pltpu.VMEM((tm, D_MODEL), BF16), pltpu.VMEM((tm, D_MODEL), F32),
                        pltpu.SMEM((1,), I32)],
        compiler_params=pltpu.CompilerParams(
            dimension_semantics=("parallel", "arbitrary", "arbitrary"), vmem_limit_bytes=MOE_VMEM_LIMIT),
        name="moe_ffn",
    )(x, g.reshape(1, D_MODEL), gate, gate.T, wg, wu, wd)


def _rmsnorm_kernel(x_ref, g_ref, o_ref):
    x = x_ref[...]
    ms = jnp.mean(x * x, axis=-1, keepdims=True)
    o_ref[...] = x * lax.rsqrt(ms + EPS) * g_ref[...]


def rmsnorm(x, g):
    M = x.shape[0]
    tm = _pick(M, (1024, 512, 256, 128))
    return pl.pallas_call(
        _rmsnorm_kernel,
        grid=(M // tm,),
        in_specs=[pl.BlockSpec((tm, D_MODEL), lambda i: (i, 0)),
                  pl.BlockSpec((1, D_MODEL), lambda i: (0, 0))],
        out_specs=pl.BlockSpec((tm, D_MODEL), lambda i: (i, 0)),
        out_shape=jax.ShapeDtypeStruct((M, D_MODEL), F32),
        compiler_params=_cparams("parallel"),
        name="final_rmsnorm",
    )(x, g.reshape(1, D_MODEL))


def _split_w_in(w_in):
    offs = np.cumsum((0,) + IN_SPLITS)
    seg = {n: w_in[:, offs[i]:offs[i + 1]] for i, n in enumerate(
        ("z", "xbc", "dt", "q", "k", "v", "iq", "iw", "ik", "rq", "rk", "rv", "rg", "gates"))}
    cat = lambda *names: jnp.concatenate([seg[n] if isinstance(n, str) else n for n in names], axis=1)
    dt_wide = jnp.repeat(seg["dt"], SSM_HEAD_DIM, axis=1)
    small = cat("dt", "iw", jnp.zeros((D_MODEL, LANES - 2 * IDX_HEADS - IDX_DIM), w_in.dtype), "ik")
    q_scaled = seg["q"] * (ATT_HEAD_DIM ** -0.5)
    groups = dict(ssm=cat("z", "xbc", dt_wide), qiq=cat(q_scaled, "iq"), kvs=cat("k", "v", small),
                  ret=cat("rq", "rk", "rv", "rg"), gate=seg["gates"])
    return {n: w.astype(BF16) for n, w in groups.items()}


def _rope_tables(pos):
    half = RET_DK // 2
    inv = ROPE_BASE ** (-jnp.arange(half, dtype=F32) / half)
    ang = pos.astype(F32)[:, None] * inv[None, :]
    cos, sin = jnp.cos(ang), jnp.sin(ang)
    return jnp.concatenate([cos, cos], axis=1), jnp.concatenate([-sin, sin], axis=1)


def _pad_rows(a, L):
    return jnp.pad(a, ((0, 0), (0, L - a.shape[1]), (0, 0)))


def _mixers(h, pos0, conv0, ssm0, ret0, attend, lw):
    b, L, _ = h.shape
    hf = h.reshape(b * L, D_MODEL)
    proj = {n: rms_matmul(hf, lw["norm_g"], w).reshape(b, L, -1)
            for n, w in lw["w_in"].items() if n != "qiq"}
    qiq = rms_matmul_heads(hf, lw["norm_g"], lw["w_in"]["qiq"], ATT_HEAD_DIM)
    k = proj["kvs"][..., :KV_WIDTH]
    v = proj["kvs"][..., KV_WIDTH:2 * KV_WIDTH]
    small = proj["kvs"][..., 2 * KV_WIDTH:]
    ik = small[..., LANES - IDX_DIM:]

    Lp = -(-L // CHUNK) * CHUNK
    last_valid = L - (Lp - CHUNK)
    dtT = jnp.swapaxes(_pad_rows(small[..., :SSM_HEADS], Lp), 1, 2)
    s0 = jnp.transpose(ssm0, (0, 2, 1, 3)).reshape(b, SSM_STATE, SSM_INNER)
    ys, conv_new, s_new = ssd_branch(_pad_rows(proj["ssm"], Lp), dtT, conv0, s0, lw["conv_w"], lw["conv_b"],
                                     lw["dt_bias"], lw["a_log"], lw["d_skip"], lw["ssm_norm_g"], last_valid)
    ssm_new = jnp.transpose(s_new.reshape(b, SSM_STATE, SSM_HEADS, SSM_HEAD_DIM), (0, 2, 1, 3))

    cos2, sin2 = _rope_tables(pos0 + jnp.arange(Lp))
    yr, ret_new = ret_branch(_pad_rows(proj["ret"], Lp), cos2, sin2, ret0, last_valid)

    ya = attend(qiq, k, v, small, ik, b, L)

    ys = ys[:, :L].reshape(b * L, -1)
    yr = yr[:, :L].reshape(b * L, -1)
    out = merge_branches(ys, ya.reshape(b * L, -1), yr, proj["gate"].reshape(b * L, -1), hf,
                         lw["w_branch"], lw["w_out"])
    return out.reshape(b, L, D_MODEL), (k.reshape(b, L, ATT_KV_HEADS, ATT_HEAD_DIM),
                                        v.reshape(b, L, ATT_KV_HEADS, ATT_HEAD_DIM),
                                        ik, ssm_new, conv_new, ret_new)


def _attend_prompt(qiq, k, v, small, ik, b, L):
    k_hm = jnp.transpose(k.reshape(b, L, ATT_KV_HEADS, ATT_HEAD_DIM), (0, 2, 1, 3)).astype(BF16)
    v_t = jnp.transpose(v.reshape(b, L, ATT_KV_HEADS, ATT_HEAD_DIM), (0, 2, 3, 1)).astype(BF16)
    v_t = jnp.concatenate([v_t, jnp.ones((b, ATT_KV_HEADS, V_ROWS - ATT_HEAD_DIM, L), BF16)], axis=2)
    iw_t = jnp.swapaxes(small[..., IDX_HEADS:2 * IDX_HEADS], 1, 2)
    return dsa_prompt(qiq, iw_t, ik.astype(BF16), k_hm, v_t)


def _attend_sample(qiq, k, v, small, ik, b, L, *, ck_t, cv_t, cik_t, layer, page_table):
    rows = lambda a: jnp.transpose(a, (1, 0, 2)).reshape(b, L, ATT_HEADS * ATT_HEAD_DIM)
    iw = small[..., IDX_HEADS:2 * IDX_HEADS]
    return dsa_sample(rows(qiq[:ATT_HEADS]), k, v, rows(qiq[ATT_HEADS:]), iw, ik,
                      ck_t, cv_t, cik_t, layer, page_table)


def _channel_mixer(h, l, p):
    b, L, _ = h.shape
    hf = h.reshape(b * L, D_MODEL)
    j = l // 2
    if l % 2 == 0:
        out = ffn_dense(hf, p["norm_ffn_g"][l], p["w_ffn_gate"][j].astype(BF16),
                        p["w_ffn_up"][j].astype(BF16), p["w_ffn_down"][j].astype(BF16))
    else:
        gate = moe_router(hf, p["norm_ffn_g"][l], p["w_router"][j])
        out = moe_ffn(hf, p["norm_ffn_g"][l], gate, p["w_moe_gate"][j].astype(BF16),
                      p["w_moe_up"][j].astype(BF16), p["w_moe_down"][j].astype(BF16))
    return out.reshape(b, L, D_MODEL)


def kernel(x_prompt, x_sample, cache_k, cache_v, cache_idx_k, state_ssm, state_conv, state_ret,
           page_table, norm_mix_g, w_in, conv_w, conv_b, dt_bias, a_log, d_skip, ssm_norm_g,
           w_branch, w_out, norm_ffn_g, w_ffn_gate, w_ffn_up, w_ffn_down, w_router,
           w_moe_gate, w_moe_up, w_moe_down, final_norm_g):
    depth = w_in.shape[0]
    past = page_table.shape[1] * PAGE_SIZE
    bp = x_prompt.shape[0]
    p = dict(norm_ffn_g=norm_ffn_g, w_ffn_gate=w_ffn_gate, w_ffn_up=w_ffn_up, w_ffn_down=w_ffn_down,
             w_router=w_router, w_moe_gate=w_moe_gate, w_moe_up=w_moe_up, w_moe_down=w_moe_down)
    n_phys = cache_k.shape[1]
    ck_t = jnp.transpose(cache_k, (0, 1, 3, 4, 2)).reshape(depth, n_phys, KV_WIDTH, PAGE_SIZE)
    cv_t = jnp.transpose(cache_v, (0, 1, 3, 4, 2)).reshape(depth, n_phys, KV_WIDTH, PAGE_SIZE)
    cik_t = jnp.transpose(cache_idx_k, (0, 1, 3, 2))
    hp, hs = x_prompt, x_sample
    st_p, st_s = [], []
    for l in range(depth):
        lw = dict(norm_g=norm_mix_g[l], w_in=_split_w_in(w_in[l]), conv_w=conv_w[l], conv_b=conv_b[l],
                  dt_bias=dt_bias[l], a_log=a_log[l], d_skip=d_skip[l], ssm_norm_g=ssm_norm_g[l],
                  w_branch=w_branch[l].astype(BF16), w_out=w_out[l].astype(BF16))
        hp, sp = _mixers(hp, 0,
                         jnp.zeros((bp, CONV_WIDTH - 1, CONV_DIM), F32),
                         jnp.zeros((bp, SSM_HEADS, SSM_STATE, SSM_HEAD_DIM), F32),
                         jnp.zeros((bp, RET_HEADS, RET_DK, RET_DV), F32),
                         _attend_prompt, lw)
        attend_s = functools.partial(_attend_sample, ck_t=ck_t, cv_t=cv_t, cik_t=cik_t, layer=l,
                                     page_table=page_table)
        hs, ss = _mixers(hs, past, state_conv[l], state_ssm[l], state_ret[l], attend_s, lw)
        hp = _channel_mixer(hp, l, p)
        hs = _channel_mixer(hs, l, p)
        st_p.append(sp)
        st_s.append(ss)
    y_prompt = rmsnorm(hp.reshape(-1, D_MODEL), final_norm_g).reshape(hp.shape)
    y_sample = rmsnorm(hs.reshape(-1, D_MODEL), final_norm_g).reshape(hs.shape)
    stack = lambda sts, i: jnp.stack([s[i] for s in sts])
    return (y_prompt, y_sample,
            stack(st_p, 0), stack(st_p, 1), stack(st_p, 2), stack(st_p, 3), stack(st_p, 4), stack(st_p, 5),
            stack(st_s, 0), stack(st_s, 1), stack(st_s, 2), stack(st_s, 3), stack(st_s, 4), stack(st_s, 5))
```

```python
import functools
import math

import jax
import jax.numpy as jnp
import numpy as np
from jax import lax
from jax.experimental import pallas as pl
from jax.experimental.pallas import tpu as pltpu

F32 = jnp.float32
BF16 = jnp.bfloat16
I32 = jnp.int32

D_MODEL = 1024
PAGE_SIZE = 128
SSM_HEADS = 16
SSM_HEAD_DIM = 64
SSM_INNER = 1024
SSM_GROUPS = 2
SSM_STATE = 128
CONV_WIDTH = 4
CONV_DIM = 1536
ATT_HEADS = 16
ATT_KV_HEADS = 4
ATT_HEAD_DIM = 64
KV_WIDTH = 256
IDX_HEADS = 16
IDX_DIM = 64
TOPK_MAX = 256
RET_HEADS = 4
RET_DK = 128
RET_DV = 256
ROPE_BASE = 10000.0
N_BRANCH = 3
D_FF = 3584
N_EXPERTS = 8
EPS = 1e-6
IN_SPLITS = (1024, 1536, 16, 1024, 256, 256, 1024, 16, 64, 512, 512, 1024, 1024, 3072)

LANES = 128
SUBLANES = 8
VMEM_LIMIT = 48 * 1024 * 1024
MOE_VMEM_LIMIT = 56 * 1024 * 1024

CHUNK = 128
INT_MIN = -(2 ** 31)
KEY_POS_INF = 0x7F800000
KEY_NEG_INF = -2139095041
NEG_BIG = -1e30


def _cparams(*sem):
    return pltpu.CompilerParams(dimension_semantics=sem, vmem_limit_bytes=VMEM_LIMIT)


def _dot(a, b):
    return jnp.dot(a, b, preferred_element_type=F32)


def _dot_nt(a, b):
    return lax.dot_general(a, b, (((1,), (1,)), ((), ())), preferred_element_type=F32)


def _dot_tn(a, b):
    return lax.dot_general(a, b, (((0,), (0,)), ((), ())), preferred_element_type=F32)


def _split3(x):
    hi = x.astype(BF16)
    r = x - hi.astype(F32)
    mid = r.astype(BF16)
    lo = (r - mid.astype(F32)).astype(BF16)
    return hi, mid, lo


def _silu(x):
    return x * jax.nn.sigmoid(x)


def _softplus(x):
    return jnp.maximum(x, 0.0) + jnp.log1p(jnp.exp(-jnp.abs(x)))


def _float_key(x):
    x = jnp.where(x == 0.0, 0.0, x)
    b = lax.bitcast_convert_type(x, I32)
    return jnp.where(b >= 0, b, b ^ jnp.int32(0x7FFFFFFF))


def _rms_matmul_kernel(x_ref, g_ref, w_ref, o_ref):
    x = x_ref[...]
    ms = jnp.mean(x * x, axis=-1, keepdims=True)
    xn = (x * lax.rsqrt(ms + EPS) * g_ref[...]).astype(BF16)
    o_ref[...] = _dot(xn, w_ref[...]).astype(o_ref.dtype)


def _rms_matmul_heads_kernel(x_ref, g_ref, w_ref, o_ref, *, n_heads, head_dim):
    x = x_ref[...]
    ms = jnp.mean(x * x, axis=-1, keepdims=True)
    xn = (x * lax.rsqrt(ms + EPS) * g_ref[...]).astype(BF16)
    res = _dot(xn, w_ref[...])
    for h in range(n_heads):
        o_ref[h] = res[:, h * head_dim:(h + 1) * head_dim].astype(o_ref.dtype)


def _pick(n, prefs):
    for p in prefs:
        if n % p == 0:
            return p
    return n


def rms_matmul(x, g, w, out_dtype=F32):
    M, K = x.shape
    N = w.shape[1]
    tm = _pick(M, (512, 256, 128))
    return pl.pallas_call(
        _rms_matmul_kernel,
        grid=(M // tm,),
        in_specs=[pl.BlockSpec((tm, K), lambda i: (i, 0)),
                  pl.BlockSpec((1, K), lambda i: (0, 0)),
                  pl.BlockSpec((K, N), lambda i: (0, 0))],
        out_specs=pl.BlockSpec((tm, N), lambda i: (i, 0)),
        out_shape=jax.ShapeDtypeStruct((M, N), out_dtype),
        compiler_params=_cparams("parallel"),
        name="rms_matmul",
    )(x, g.reshape(1, K), w)


def rms_matmul_heads(x, g, w, head_dim):
    M, K = x.shape
    N = w.shape[1]
    n_heads = N // head_dim
    tm = _pick(M, (512, 256, 128))
    kern = functools.partial(_rms_matmul_heads_kernel, n_heads=n_heads, head_dim=head_dim)
    return pl.pallas_call(
        kern,
        grid=(M // tm,),
        in_specs=[pl.BlockSpec((tm, K), lambda i: (i, 0)),
                  pl.BlockSpec((1, K), lambda i: (0, 0)),
                  pl.BlockSpec((K, N), lambda i: (0, 0))],
        out_specs=pl.BlockSpec((n_heads, tm, head_dim), lambda i: (0, i, 0)),
        out_shape=jax.ShapeDtypeStruct((n_heads, M, head_dim), BF16),
        compiler_params=_cparams("parallel"),
        name="rms_matmul_heads",
    )(x, g.reshape(1, K), w)


def _ssd_kernel(zxd_ref, dtT_ref, conv0_ref, s0_ref, cw_ref, cb_ref, dtb_ref, alog_ref, dsk_ref,
                ng_ref, dtbT_ref, alogT_ref, y_ref, convn_ref, sn_ref, xpad_ref, st_ref,
                *, Q, last_valid, nc):
    c = pl.program_id(1)
    GW = SSM_INNER // SSM_GROUPS

    @pl.when(c == 0)
    def _():
        xpad_ref[0:8, :] = jnp.zeros((8, CONV_DIM), F32)
        xpad_ref[5:8, :] = conv0_ref[0]
        st_ref[...] = s0_ref[0]

    blk = zxd_ref[0]
    z = blk[:, :SSM_INNER]
    xbc = blk[:, SSM_INNER:SSM_INNER + CONV_DIM]
    dtr = blk[:, SSM_INNER + CONV_DIM:]
    xpad_ref[8:8 + Q, :] = xbc
    cw = cw_ref[...]
    conv = (xpad_ref[5:5 + Q, :] * cw[0:1] + xpad_ref[6:6 + Q, :] * cw[1:2]
            + xpad_ref[7:7 + Q, :] * cw[2:3] + xbc * cw[3:4]) + cb_ref[...]
    xc = _silu(conv)
    xs = xc[:, :SSM_INNER]
    Bm = xc[:, SSM_INNER:SSM_INNER + SSM_GROUPS * SSM_STATE]
    Cm = xc[:, SSM_INNER + SSM_GROUPS * SSM_STATE:]

    row = lax.broadcasted_iota(I32, (Q, 1), 0)
    colq = lax.broadcasted_iota(I32, (1, Q), 1)
    assert last_valid == Q or nc == 1
    lv = last_valid
    valid = row < lv
    tril = (lax.broadcasted_iota(I32, (Q, Q), 0) >= lax.broadcasted_iota(I32, (Q, Q), 1))
    tril_b = jnp.where(tril, 1.0, 0.0).astype(BF16)
    triu_b = jnp.where(lax.broadcasted_iota(I32, (Q, Q), 0) <= lax.broadcasted_iota(I32, (Q, Q), 1),
                       1.0, 0.0).astype(BF16)

    nega = -jnp.exp(alog_ref[...])
    dt = _softplus(dtr + dtb_ref[...])
    la = jnp.where(valid, dt * nega, 0.0)
    cum = sum(_dot(tril_b, p) for p in _split3(la))
    laT = jnp.where(colq < lv, _softplus(dtT_ref[0] + dtbT_ref[...]) * (-jnp.exp(alogT_ref[...])), 0.0)
    cumT = sum(_dot(p, triu_b) for p in _split3(laT))

    ecum = jnp.exp(cum)
    cl = cum[lv - 1:lv, :]
    xdt = xs * dt
    xtail = jnp.where(valid, xdt * jnp.exp(cl - cum), 0.0)
    cdecay = jnp.exp(cl)

    lane = lax.broadcasted_iota(I32, (1, LANES), 1)
    y_groups = []
    for g in range(SSM_GROUPS):
        l0 = g * GW
        Cg = Cm[:, g * SSM_STATE:(g + 1) * SSM_STATE].astype(BF16)
        Bg = Bm[:, g * SSM_STATE:(g + 1) * SSM_STATE].astype(BF16)
        G = _dot_nt(Cg, Bg)
        st_g = st_ref[:, l0:l0 + GW]
        inter = _dot(Cg, st_g.astype(BF16)) * ecum[:, l0:l0 + GW]
        local = _dot_tn(Bg, xtail[:, l0:l0 + GW].astype(BF16))
        st_ref[:, l0:l0 + GW] = st_g * cdecay[:, l0:l0 + GW] + local
        pairs = []
        for p in range(GW // LANES):
            xp = xdt[:, l0 + p * LANES:l0 + (p + 1) * LANES]
            acc = None
            for hh in range(2):
                h = (l0 + p * LANES) // SSM_HEAD_DIM + hh
                ccol = cum[:, h * SSM_HEAD_DIM:h * SSM_HEAD_DIM + 1]
                diff = ccol - cumT[h:h + 1, :]
                dm = jnp.exp(jnp.where(tril, diff, -jnp.inf))
                s = (G * dm).astype(BF16)
                half = (lane >= hh * SSM_HEAD_DIM) & (lane < (hh + 1) * SSM_HEAD_DIM)
                part = _dot(s, jnp.where(half, xp, 0.0).astype(BF16))
                acc = part if acc is None else acc + part
            pairs.append(acc)
        y_groups.append(jnp.concatenate(pairs, axis=1) + inter)
    y = jnp.concatenate(y_groups, axis=1)

    y = (y + dsk_ref[...] * xs) * _silu(z)
    outs = []
    for g in range(SSM_GROUPS):
        seg = y[:, g * GW:(g + 1) * GW]
        ms = jnp.mean(seg * seg, axis=-1, keepdims=True)
        outs.append(seg * lax.rsqrt(ms + EPS) * ng_ref[:, g * GW:(g + 1) * GW])
    y_ref[0] = jnp.concatenate(outs, axis=1)

    @pl.when(c == nc - 1)
    def _():
        convn_ref[0] = xpad_ref[5 + last_valid:8 + last_valid, :]
        sn_ref[0] = st_ref[...]

    xpad_ref[0:8, :] = xpad_ref[Q:Q + 8, :]


def ssd_branch(zxd, dtT, conv0, s0, conv_w, conv_b, dt_bias, a_log, d_skip, norm_g, last_valid):
    b, L, W = zxd.shape
    Q = CHUNK
    nc = L // Q
    rep = lambda v: jnp.repeat(v.astype(F32), SSM_HEAD_DIM).reshape(1, SSM_INNER)
    col = lambda v: v.astype(F32).reshape(SSM_HEADS, 1)
    full = lambda shape: pl.BlockSpec(shape, lambda i, c: (0,) * len(shape))
    kern = functools.partial(_ssd_kernel, Q=Q, last_valid=last_valid, nc=nc)
    return pl.pallas_call(
        kern,
        grid=(b, nc),
        in_specs=[pl.BlockSpec((1, Q, W), lambda i, c: (i, c, 0)),
                  pl.BlockSpec((1, SSM_HEADS, Q), lambda i, c: (i, 0, c)),
                  pl.BlockSpec((1, CONV_WIDTH - 1, CONV_DIM), lambda i, c: (i, 0, 0)),
                  pl.BlockSpec((1, SSM_STATE, SSM_INNER), lambda i, c: (i, 0, 0)),
                  full((CONV_WIDTH, CONV_DIM)), full((1, CONV_DIM)),
                  full((1, SSM_INNER)), full((1, SSM_INNER)), full((1, SSM_INNER)), full((1, SSM_INNER)),
                  full((SSM_HEADS, 1)), full((SSM_HEADS, 1))],
        out_specs=[pl.BlockSpec((1, Q, SSM_INNER), lambda i, c: (i, c, 0)),
                   pl.BlockSpec((1, CONV_WIDTH - 1, CONV_DIM), lambda i, c: (i, 0, 0)),
                   pl.BlockSpec((1, SSM_STATE, SSM_INNER), lambda i, c: (i, 0, 0))],
        out_shape=[jax.ShapeDtypeStruct((b, L, SSM_INNER), F32),
                   jax.ShapeDtypeStruct((b, CONV_WIDTH - 1, CONV_DIM), F32),
                   jax.ShapeDtypeStruct((b, SSM_STATE, SSM_INNER), F32)],
        scratch_shapes=[pltpu.VMEM((Q + 8, CONV_DIM), F32), pltpu.VMEM((SSM_STATE, SSM_INNER), F32)],
        compiler_params=_cparams("parallel", "arbitrary"),
        name="ssd_branch",
    )(zxd, dtT, conv0, s0, conv_w.astype(F32), conv_b.reshape(1, CONV_DIM).astype(F32),
      rep(dt_bias), rep(a_log), rep(d_skip), norm_g.reshape(1, SSM_INNER).astype(F32),
      col(dt_bias), col(a_log))


def _ret_kernel(x_ref, cos_ref, sin_ref, r0_ref, y_ref, rn_ref, st_ref, *, Q, last_valid, nc):
    c = pl.program_id(1)

    @pl.when(c == 0)
    def _():
        st_ref[...] = r0_ref[0]

    blk = x_ref[0]
    cos2 = cos_ref[...]
    sin2 = sin_ref[...]
    ri = lax.broadcasted_iota(I32, (Q, Q), 0)
    ci = lax.broadcasted_iota(I32, (Q, Q), 1)
    dij = (ri - ci).astype(F32)
    row = lax.broadcasted_iota(I32, (Q, 1), 0)
    assert last_valid == Q or nc == 1
    lv = last_valid
    rowf = row.astype(F32)
    lvf = float(lv)
    KO = RET_HEADS * RET_DK
    outs = []
    for h in range(RET_HEADS):
        lg = math.log1p(-2.0 ** (-5.0 - h))
        qh = blk[:, h * RET_DK:(h + 1) * RET_DK]
        kh = blk[:, KO + h * RET_DK:KO + (h + 1) * RET_DK]
        vh = blk[:, 2 * KO + h * RET_DV:2 * KO + (h + 1) * RET_DV].astype(BF16)
        gh = blk[:, 2 * KO + RET_HEADS * RET_DV + h * RET_DV:2 * KO + RET_HEADS * RET_DV + (h + 1) * RET_DV]
        qr = qh * cos2 + pltpu.roll(qh, RET_DK // 2, 1) * sin2
        kr = (kh * cos2 + pltpu.roll(kh, RET_DK // 2, 1) * sin2) * (RET_DK ** -0.5)
        qb = qr.astype(BF16)
        dm = jnp.exp(jnp.where(ri >= ci, dij * lg, -jnp.inf))
        s = (_dot_nt(qb, kr.astype(BF16)) * dm).astype(BF16)
        st_h = st_ref[h]
        y = _dot(s, vh) + jnp.exp((rowf + 1.0) * lg) * _dot(qb, st_h.astype(BF16))
        ktail = jnp.where(row < lv, kr * jnp.exp((lvf - 1.0 - rowf) * lg), 0.0)
        st_ref[h] = st_h * math.exp(lvf * lg) + _dot_tn(ktail.astype(BF16), vh)
        ms = jnp.mean(y * y, axis=-1, keepdims=True)
        outs.append(y * lax.rsqrt(ms + EPS) * _silu(gh))
    y_ref[0] = jnp.concatenate(outs, axis=1)

    @pl.when(c == nc - 1)
    def _():
        rn_ref[0] = st_ref[...]


def ret_branch(x, cos2, sin2, r0, last_valid):
    b, L, W = x.shape
    Q = CHUNK
    nc = L // Q
    kern = functools.partial(_ret_kernel, Q=Q, last_valid=last_valid, nc=nc)
    return pl.pallas_call(
        kern,
        grid=(b, nc),
        in_specs=[pl.BlockSpec((1, Q, W), lambda i, c: (i, c, 0)),
                  pl.BlockSpec((Q, RET_DK), lambda i, c: (c, 0)),
                  pl.BlockSpec((Q, RET_DK), lambda i, c: (c, 0)),
                  pl.BlockSpec((1, RET_HEADS, RET_DK, RET_DV), lambda i, c: (i, 0, 0, 0))],
        out_specs=[pl.BlockSpec((1, Q, RET_HEADS * RET_DV), lambda i, c: (i, c, 0)),
                   pl.BlockSpec((1, RET_HEADS, RET_DK, RET_DV), lambda i, c: (i, 0, 0, 0))],
        out_shape=[jax.ShapeDtypeStruct((b, L, RET_HEADS * RET_DV), F32),
                   jax.ShapeDtypeStruct((b, RET_HEADS, RET_DK, RET_DV), F32)],
        scratch_shapes=[pltpu.VMEM((RET_HEADS, RET_DK, RET_DV), F32)],
        compiler_params=_cparams("parallel", "arbitrary"),
        name="ret_branch",
    )(x, cos2, sin2, r0)


def _kth_largest_key(count_ge, k, shape):
    def body(t, prefix):
        bit = lax.shift_left(jnp.int32(1), jnp.int32(31) - t)
        cand = prefix | bit
        cnt = count_ge(cand ^ jnp.int32(INT_MIN))
        return jnp.where(cnt >= k, cand, prefix)

    prefix = lax.fori_loop(0, 32, body, jnp.zeros(shape, I32))
    return prefix ^ jnp.int32(INT_MIN)


def _dsa_prompt_kernel(q_ref, iq_ref, iwt_ref, ik_ref, k_ref, vt_ref, o_ref, keys_ref, jb_ref,
                       m_ref, acc_ref, *, tq, ck, cka, L, k_sel):
    i = pl.program_id(1)
    G = ATT_HEADS // ATT_KV_HEADS
    PART = 4 * SUBLANES
    nk = lax.div((i + 1) * tq + (ck - 1), ck)
    qpos = i * tq + lax.broadcasted_iota(I32, (1, tq), 1)
    rowk = lax.broadcasted_iota(I32, (ck, 1), 0)
    iw = iwt_ref[0] * (IDX_HEADS ** -0.5 * IDX_DIM ** -0.5)

    def score_body(kc, carry):
        off = pl.multiple_of(kc * ck, ck)
        ikc = ik_ref[0, pl.ds(off, ck), :]
        acc = jnp.zeros((ck, tq), F32)
        for h4 in range(0, IDX_HEADS, G):
            d = _dot_nt(ikc, iq_ref[h4:h4 + G].reshape(G * tq, IDX_DIM))
            for j in range(G):
                acc = acc + jnp.maximum(d[:, j * tq:(j + 1) * tq], 0.0) * iw[h4 + j:h4 + j + 1, :]
        key = jnp.where(off + rowk <= qpos, _float_key(acc), jnp.int32(INT_MIN))
        keys_ref[pl.ds(off, ck), :] = key
        return carry

    lax.fori_loop(0, nk, score_body, 0)

    def count(pred):
        def body(kc, part):
            off = pl.multiple_of(kc * ck, ck)
            hit = jnp.where(pred(keys_ref[pl.ds(off, ck), :], off + rowk), 1.0, 0.0)
            return part + jnp.sum(hit.reshape(ck // PART, PART, tq), axis=0)
        part = lax.fori_loop(0, nk, body, jnp.zeros((PART, tq), F32))
        return jnp.sum(part, axis=0, keepdims=True)

    kf = float(k_sel)
    thr = _kth_largest_key(lambda cand: count(lambda kk, pos: kk >= cand), kf, (1, tq))
    thr = jnp.maximum(thr, jnp.int32(KEY_NEG_INF + 1))
    n_ge = count(lambda kk, pos: kk >= thr)
    jb_ref[...] = jnp.full((1, tq), L, I32)

    @pl.when(jnp.max(n_ge) > kf)
    def _():
        need = kf - count(lambda kk, pos: kk > thr)
        nb = max(1, (L - 1).bit_length())

        def body(t, pfx):
            cand = pfx | lax.shift_left(jnp.int32(1), jnp.int32(nb - 1) - t)
            cnt = count(lambda kk, pos: (kk == thr) & (pos < cand))
            return jnp.where(cnt < need, cand, pfx)

        pfx = lax.fori_loop(0, nb, body, jnp.zeros((1, tq), I32))
        jb_ref[...] = jnp.where(n_ge > kf, pfx + 1, L)

    jb = jb_ref[...]

    m_ref[...] = jnp.full(m_ref.shape, NEG_BIG, F32)
    acc_ref[...] = jnp.zeros(acc_ref.shape, F32)
    rowa = lax.broadcasted_iota(I32, (cka, 1), 0)

    def att_body(kc, carry):
        off = pl.multiple_of(kc * cka, cka)
        kk = keys_ref[pl.ds(off, cka), :]
        sel = (kk >= thr) & (kk < jnp.int32(KEY_POS_INF)) & ((kk > thr) | (off + rowa < jb))
        bias = jnp.where(sel, 0.0, -jnp.inf)
        bias = jnp.concatenate([bias] * ATT_HEADS, axis=1)
        s = jnp.concatenate(
            [_dot_nt(k_ref[0, g, pl.ds(off, cka), :], q_ref[g * G:(g + 1) * G].reshape(G * tq, ATT_HEAD_DIM))
             for g in range(ATT_KV_HEADS)], axis=1) + bias
        m = m_ref[...]
        smax = jnp.max(jnp.max(s.reshape(cka // PART, PART, ATT_HEADS * tq), axis=0), axis=0, keepdims=True)
        m_new = jnp.maximum(m, smax)
        p = jnp.exp((s - m_new).astype(BF16))
        pv = jnp.concatenate(
            [_dot(vt_ref[0, g, :, pl.ds(off, cka)], p[:, g * G * tq:(g + 1) * G * tq])
             for g in range(ATT_KV_HEADS)], axis=1)
        acc_ref[...] = jnp.exp(m - m_new) * acc_ref[...] + pv
        m_ref[...] = m_new
        return carry

    lax.fori_loop(0, lax.div((i + 1) * tq + (cka - 1), cka), att_body, 0)
    acc = acc_ref[...]
    o = acc[:ATT_HEAD_DIM] / acc[ATT_HEAD_DIM:ATT_HEAD_DIM + 1]
    for hp in range(ATT_HEADS // 2):
        pair = jnp.concatenate([o[:, (2 * hp) * tq:(2 * hp + 1) * tq],
                                o[:, (2 * hp + 1) * tq:(2 * hp + 2) * tq]], axis=0)
        o_ref[0, :, hp * LANES:(hp + 1) * LANES] = pair.T


V_ROWS = ATT_HEAD_DIM + 16


def dsa_prompt(qiq, iw_t, ik, k_hm, v_t):
    b, _, L, _ = k_hm.shape
    tq = min(128, L)
    nq = L // tq
    ck = min(512, L)
    cka = ck
    gq = (ATT_HEADS // ATT_KV_HEADS) * tq
    k_sel = max(1, min(TOPK_MAX, L // 4))
    kern = functools.partial(_dsa_prompt_kernel, tq=tq, ck=ck, cka=cka, L=L, k_sel=k_sel)
    return pl.pallas_call(
        kern,
        grid=(b, L // tq),
        in_specs=[pl.BlockSpec((ATT_HEADS, tq, ATT_HEAD_DIM), lambda bi, i: (0, bi * nq + i, 0)),
                  pl.BlockSpec((IDX_HEADS, tq, IDX_DIM), lambda bi, i: (1, bi * nq + i, 0)),
                  pl.BlockSpec((1, IDX_HEADS, tq), lambda bi, i: (bi, 0, i)),
                  pl.BlockSpec((1, L, IDX_DIM), lambda bi, i: (bi, 0, 0)),
                  pl.BlockSpec((1, ATT_KV_HEADS, L, ATT_HEAD_DIM), lambda bi, i: (bi, 0, 0, 0)),
                  pl.BlockSpec((1, ATT_KV_HEADS, V_ROWS, L), lambda bi, i: (bi, 0, 0, 0))],
        out_specs=pl.BlockSpec((1, tq, ATT_HEADS * ATT_HEAD_DIM), lambda bi, i: (bi, i, 0)),
        out_shape=jax.ShapeDtypeStruct((b, L, ATT_HEADS * ATT_HEAD_DIM), F32),
        scratch_shapes=[pltpu.VMEM((L, tq), I32), pltpu.VMEM((1, tq), I32),
                        pltpu.VMEM((1, ATT_HEADS * tq), F32),
                        pltpu.VMEM((V_ROWS, ATT_HEADS * tq), F32)],
        compiler_params=_cparams("parallel", "arbitrary"),
        name="dsa_prompt",
    )(qiq, qiq, iw_t, ik, k_hm, v_t)


TQ8 = SUBLANES


def _sample_score_keys(d, iww):
    n = d.shape[1]
    r = jnp.maximum(d, 0.0) * (iww[:, 0:1] * (IDX_HEADS ** -0.5 * IDX_DIM ** -0.5))
    return _float_key(jnp.sum(r.reshape(TQ8, IDX_HEADS, n), axis=1))


def _dsa_sample_scores_kernel(pt_ref, *refs, n_pages):
    page_refs = refs[:n_pages]
    iq_ref, iww_ref, keys_ref = refs[n_pages:]
    ik_t = jnp.concatenate([r[0, 0] for r in page_refs], axis=1).astype(BF16)
    keys_ref[0] = _sample_score_keys(_dot(iq_ref[0], ik_t), iww_ref[0])


def _dsa_sample_thr_kernel(kp_ref, iq_ref, iww_ref, ikn_ref, thr_ref, jb_ref, kn_ref,
                           *, nb_seq, T, past, k_sel):
    R8 = nb_seq * TQ8
    kp = kp_ref[...].reshape(R8, past)
    rowi = lax.broadcasted_iota(I32, (TQ8, LANES), 0)
    col1 = lax.broadcasted_iota(I32, (TQ8, LANES), 1)
    kn = jnp.concatenate(
        [jnp.where((col1 <= rowi) & (col1 < T),
                   _sample_score_keys(_dot_nt(iq_ref[s], ikn_ref[s]), iww_ref[s]), jnp.int32(INT_MIN))
         for s in range(nb_seq)], axis=0)
    coln = lax.broadcasted_iota(I32, (R8, LANES), 1)
    colp = lax.broadcasted_iota(I32, (R8, past), 1)

    def count(pred):
        hit = jnp.where(pred(kp, colp), 1.0, 0.0)
        part = jnp.where(pred(kn, coln + past), 1.0, 0.0)
        for t in range(past // LANES):
            part = part + hit[:, t * LANES:(t + 1) * LANES]
        return jnp.sum(part, axis=1, keepdims=True)

    kf = float(k_sel)
    thr = _kth_largest_key(lambda cand: count(lambda kk, col: kk >= cand), kf, (R8, 1))
    thr = jnp.maximum(thr, jnp.int32(KEY_NEG_INF + 1))
    need = kf - count(lambda kk, col: kk > thr)
    nb = (past + LANES - 1).bit_length()

    def body(t, pfx):
        cand = pfx | lax.shift_left(jnp.int32(1), jnp.int32(nb - 1) - t)
        cnt = count(lambda kk, col: (kk == thr) & (col < cand))
        return jnp.where(cnt < need, cand, pfx)

    pfx = lax.fori_loop(0, nb, body, jnp.zeros((R8, 1), I32))
    thr_ref[...] = jnp.broadcast_to(thr, (R8, LANES)).reshape(nb_seq, TQ8, LANES)
    jb_ref[...] = jnp.broadcast_to(pfx + 1, (R8, LANES)).reshape(nb_seq, TQ8, LANES)
    kn_ref[...] = kn.reshape(nb_seq, TQ8, LANES)


def _dsa_sample_attn_kernel(pt_ref, *refs, n_pages, T, past):
    k_refs = refs[:n_pages]
    v_refs = refs[n_pages:2 * n_pages]
    (q_ref, kp_ref, thr_ref, jb_ref, kn_ref, knew_ref, vnew_ref,
     o_ref, m_ref, l_ref, acc_ref) = refs[2 * n_pages:]
    p = pl.program_id(1)
    R = T * ATT_HEADS

    @pl.when(p == 0)
    def _():
        m_ref[...] = jnp.full((R, 1), NEG_BIG, F32)
        l_ref[...] = jnp.zeros((R, 1), F32)
        acc_ref[...] = jnp.zeros((R, KV_WIDTH), F32)

    thr = thr_ref[0][0:T, 0:1]
    jb = jb_ref[0][0:T, 0:1]

    def step(kk, col0, s, pv):
        n = kk.shape[1]
        col = col0 + lax.broadcasted_iota(I32, (1, n), 1)
        sel = (kk >= thr) & (kk < jnp.int32(KEY_POS_INF)) & ((kk > thr) | (col < jb))
        bias = jnp.where(sel, 0.0, -jnp.inf)
        s = (s.reshape(T, ATT_HEADS, n) + bias[:, None, :]).reshape(R, n)
        m = m_ref[...]
        m_new = jnp.maximum(m, jnp.max(s, axis=1, keepdims=True))
        alpha = jnp.exp(m - m_new)
        pr = jnp.exp(s - m_new)
        l_ref[...] = alpha * l_ref[...] + jnp.sum(pr, axis=1, keepdims=True)
        acc_ref[...] = alpha * acc_ref[...] + pv(pr.astype(BF16))
        m_ref[...] = m_new

    k_t = jnp.concatenate([r[0, 0] for r in k_refs], axis=1).astype(BF16)
    v_t = jnp.concatenate([r[0, 0] for r in v_refs], axis=1).astype(BF16)
    step(kp_ref[0][0:T], p * (n_pages * PAGE_SIZE), _dot(q_ref[0], k_t), lambda pr: _dot_nt(pr, v_t))

    @pl.when(p == pl.num_programs(1) - 1)
    def _():
        step(kn_ref[0][0:T], past, _dot_nt(q_ref[0], knew_ref[0]), lambda pr: _dot(pr, vnew_ref[0]))
        o_ref[0] = acc_ref[...] / l_ref[...]


def _pages_per_step(npages, want):
    while npages % want:
        want //= 2
    return want


def dsa_sample(q, k, v, iq, iw, ik, ck_t, cv_t, cik_t, layer, page_table):
    b, T, _ = q.shape
    assert T <= TQ8
    npages = page_table.shape[1]
    past = npages * PAGE_SIZE
    k_sel = max(1, min(TOPK_MAX, (past + T) // 4))
    R = T * ATT_HEADS
    R8 = TQ8 * IDX_HEADS
    pad_to = lambda a, n: jnp.pad(a, ((0, 0), (0, n - a.shape[1]), (0, 0)))
    iq_rows = pad_to(iq.reshape(b, T * IDX_HEADS, IDX_DIM), R8).astype(BF16)
    iww = jnp.broadcast_to(pad_to(iw.reshape(b, T * IDX_HEADS, 1), R8), (b, R8, LANES)).astype(F32)
    ik_new, k_new, v_new = (pad_to(a, LANES).astype(BF16) for a in (ik, k, v))
    head_group = jnp.arange(ATT_HEADS) // (ATT_HEADS // ATT_KV_HEADS)
    onehot = (head_group[:, None] == jnp.arange(ATT_KV_HEADS)[None, :]).astype(F32)
    q_bd = (q.reshape(b, T, ATT_HEADS, 1, ATT_HEAD_DIM) * onehot[None, None, :, :, None])
    q_bd = q_bd.reshape(b, R, KV_WIDTH).astype(BF16)

    def page_spec(rows, per_step, j):
        return pl.BlockSpec((1, 1, rows, PAGE_SIZE),
                            lambda bi, p, pt: (layer, pt[bi, p * per_step + j], 0, 0))

    fixed = lambda shape: pl.BlockSpec((1,) + shape, lambda bi, p, pt: (bi, 0, 0))

    ps = _pages_per_step(npages, 64)
    keys_past = pl.pallas_call(
        functools.partial(_dsa_sample_scores_kernel, n_pages=ps),
        grid_spec=pltpu.PrefetchScalarGridSpec(
            num_scalar_prefetch=1, grid=(b, npages // ps),
            in_specs=[page_spec(IDX_DIM, ps, j) for j in range(ps)]
            + [fixed((R8, IDX_DIM)), fixed((R8, LANES))],
            out_specs=pl.BlockSpec((1, TQ8, ps * PAGE_SIZE), lambda bi, p, pt: (bi, 0, p))),
        out_shape=jax.ShapeDtypeStruct((b, TQ8, past), I32),
        compiler_params=_cparams("parallel", "arbitrary"),
        name="dsa_sample_scores",
    )(page_table, *([cik_t] * ps), iq_rows, iww)

    nb_seq = SUBLANES if b % SUBLANES == 0 else 1
    seq_spec = lambda r, w: pl.BlockSpec((nb_seq, r, w), lambda bi: (bi, 0, 0))
    thr, jb, keys_new = pl.pallas_call(
        functools.partial(_dsa_sample_thr_kernel, nb_seq=nb_seq, T=T, past=past, k_sel=k_sel),
        grid=(b // nb_seq,),
        in_specs=[seq_spec(TQ8, past), seq_spec(R8, IDX_DIM), seq_spec(R8, LANES),
                  seq_spec(LANES, IDX_DIM)],
        out_specs=[seq_spec(TQ8, LANES)] * 3,
        out_shape=[jax.ShapeDtypeStruct((b, TQ8, LANES), I32)] * 3,
        compiler_params=_cparams("parallel"),
        name="dsa_sample_threshold",
    )(keys_past, iq_rows, iww, ik_new)

    pa = _pages_per_step(npages, 32)
    o = pl.pallas_call(
        functools.partial(_dsa_sample_attn_kernel, n_pages=pa, T=T, past=past),
        grid_spec=pltpu.PrefetchScalarGridSpec(
            num_scalar_prefetch=1, grid=(b, npages // pa),
            in_specs=[page_spec(KV_WIDTH, pa, j) for j in range(pa)] * 2
            + [fixed((R, KV_WIDTH)),
               pl.BlockSpec((1, TQ8, pa * PAGE_SIZE), lambda bi, p, pt: (bi, 0, p)),
               fixed((TQ8, LANES)), fixed((TQ8, LANES)), fixed((TQ8, LANES)),
               fixed((LANES, KV_WIDTH)), fixed((LANES, KV_WIDTH))],
            out_specs=fixed((R, KV_WIDTH)),
            scratch_shapes=[pltpu.VMEM((R, 1), F32), pltpu.VMEM((R, 1), F32),
                            pltpu.VMEM((R, KV_WIDTH), F32)]),
        out_shape=jax.ShapeDtypeStruct((b, R, KV_WIDTH), F32),
        compiler_params=_cparams("parallel", "arbitrary"),
        name="dsa_sample_attention",
    )(page_table, *([ck_t] * pa), *([cv_t] * pa), q_bd, keys_past, thr, jb, keys_new, k_new, v_new)

    o = o.reshape(b, T, ATT_HEADS, ATT_KV_HEADS, ATT_HEAD_DIM)
    o = jnp.sum(o * onehot[None, None, :, :, None], axis=3)
    return o.reshape(b, T, ATT_HEADS * ATT_HEAD_DIM)


def _merge_kernel(ys_ref, ya_ref, yr_ref, g_ref, h_ref, wb_ref, wo_ref, o_ref):
    acc = None
    for n, y_ref in enumerate((ys_ref, ya_ref, yr_ref)):
        pr = _dot(y_ref[...].astype(BF16), wb_ref[n])
        t = jax.nn.sigmoid(g_ref[:, n * D_MODEL:(n + 1) * D_MODEL]) * pr
        acc = t if acc is None else acc + t
    o_ref[...] = h_ref[...] + _dot(acc.astype(BF16), wo_ref[...])


def merge_branches(ys, ya, yr, gates, h, wb, wo):
    M = h.shape[0]
    tm = _pick(M, (256, 128))
    rows = lambda w: pl.BlockSpec((tm, w), lambda i: (i, 0))
    return pl.pallas_call(
        _merge_kernel,
        grid=(M // tm,),
        in_specs=[rows(D_MODEL), rows(D_MODEL), rows(D_MODEL), rows(N_BRANCH * D_MODEL), rows(D_MODEL),
                  pl.BlockSpec((N_BRANCH, D_MODEL, D_MODEL), lambda i: (0, 0, 0)),
                  pl.BlockSpec((D_MODEL, D_MODEL), lambda i: (0, 0))],
        out_specs=rows(D_MODEL),
        out_shape=jax.ShapeDtypeStruct((M, D_MODEL), F32),
        compiler_params=_cparams("parallel"),
        name="merge_branches",
    )(ys, ya, yr, gates, h, wb, wo)


def _ffn_kernel(x_ref, g_ref, wg_ref, wu_ref, wd_ref, o_ref, xn_ref, acc_ref):
    j = pl.program_id(1)

    @pl.when(j == 0)
    def _():
        x = x_ref[...]
        ms = jnp.mean(x * x, axis=-1, keepdims=True)
        xn_ref[...] = (x * lax.rsqrt(ms + EPS) * g_ref[...]).astype(BF16)
        acc_ref[...] = jnp.zeros_like(acc_ref)

    xn = xn_ref[...]
    a = _silu(_dot(xn, wg_ref[...])) * _dot(xn, wu_ref[...])
    acc_ref[...] += _dot(a.astype(BF16), wd_ref[...])

    @pl.when(j == pl.num_programs(1) - 1)
    def _():
        o_ref[...] = x_ref[...] + acc_ref[...]


def ffn_dense(x, g, wg, wu, wd):
    M = x.shape[0]
    tm = _pick(M, (1024, 512, 256, 128))
    tf = 512
    return pl.pallas_call(
        _ffn_kernel,
        grid=(M // tm, D_FF // tf),
        in_specs=[pl.BlockSpec((tm, D_MODEL), lambda i, j: (i, 0)),
                  pl.BlockSpec((1, D_MODEL), lambda i, j: (0, 0)),
                  pl.BlockSpec((D_MODEL, tf), lambda i, j: (0, j)),
                  pl.BlockSpec((D_MODEL, tf), lambda i, j: (0, j)),
                  pl.BlockSpec((tf, D_MODEL), lambda i, j: (j, 0))],
        out_specs=pl.BlockSpec((tm, D_MODEL), lambda i, j: (i, 0)),
        out_shape=jax.ShapeDtypeStruct((M, D_MODEL), F32),
        scratch_shapes=[pltpu.VMEM((tm, D_MODEL), BF16), pltpu.VMEM((tm, D_MODEL), F32)],
        compiler_params=_cparams("parallel", "arbitrary"),
        name="ffn_dense",
    )(x, g.reshape(1, D_MODEL), wg, wu, wd)


def _router_kernel(x_ref, g_ref, wr_ref, gate_ref):
    x = x_ref[...]
    ms = jnp.mean(x * x, axis=-1, keepdims=True)
    xn = x * lax.rsqrt(ms + EPS) * g_ref[...]
    logits = jnp.dot(xn, wr_ref[...], preferred_element_type=F32, precision=lax.Precision.HIGHEST)
    lane = lax.broadcasted_iota(I32, logits.shape, 1)
    logits = jnp.where(lane < N_EXPERTS, logits, -jnp.inf)
    v1 = jnp.max(logits, axis=1, keepdims=True)
    i1 = jnp.min(jnp.where(logits == v1, lane, LANES), axis=1, keepdims=True)
    rest = jnp.where(lane == i1, -jnp.inf, logits)
    v2 = jnp.max(rest, axis=1, keepdims=True)
    i2 = jnp.min(jnp.where(rest == v2, lane, LANES), axis=1, keepdims=True)
    e2 = jnp.exp(v2 - v1)
    w1 = 1.0 / (1.0 + e2)
    w2 = e2 / (1.0 + e2)
    gate_ref[...] = jnp.where(lane == i1, w1, 0.0) + jnp.where(lane == i2, w2, 0.0)


def moe_router(x, g, wr):
    M = x.shape[0]
    tm = _pick(M, (512, 256, 128))
    wr_pad = jnp.pad(wr.astype(F32), ((0, 0), (0, LANES - N_EXPERTS)))
    return pl.pallas_call(
        _router_kernel,
        grid=(M // tm,),
        in_specs=[pl.BlockSpec((tm, D_MODEL), lambda i: (i, 0)),
                  pl.BlockSpec((1, D_MODEL), lambda i: (0, 0)),
                  pl.BlockSpec((D_MODEL, LANES), lambda i: (0, 0))],
        out_specs=pl.BlockSpec((tm, LANES), lambda i: (i, 0)),
        out_shape=jax.ShapeDtypeStruct((M, LANES), F32),
        compiler_params=_cparams("parallel"),
        name="moe_router",
    )(x, g.reshape(1, D_MODEL), wr_pad)


MOE_BLOCK_ROWS = (192, 256, 320, 384, 512)


def _moe_kernel(x_ref, g_ref, gate_ref, gatet_ref, wg_ref, wu_ref, wd_ref, o_ref,
                xn_ref, rkc_ref, rkr_ref, xe_ref, ye_ref, cnt_ref, *, tm, sizes):
    e = pl.program_id(1)
    j = pl.program_id(2)
    last_j = pl.num_programs(2) - 1

    def for_block_size(cnt, body):
        lo = 0
        for s in sizes:
            pl.when((cnt > lo) & (cnt <= s))(functools.partial(body, s))
            lo = s

    @pl.when((e == 0) & (j == 0))
    def _():
        x = x_ref[...]
        ms = jnp.mean(x * x, axis=-1, keepdims=True)
        xn_ref[...] = (x * lax.rsqrt(ms + EPS) * g_ref[...]).astype(BF16)
        o_ref[...] = x
        ti = lax.broadcasted_iota(I32, (tm, tm), 0)
        tj = lax.broadcasted_iota(I32, (tm, tm), 1)
        flags = jnp.where(gate_ref[...] > 0.0, 1.0, 0.0).astype(BF16)
        flags_t = jnp.where(gatet_ref[...] > 0.0, 1.0, 0.0).astype(BF16)
        rkc_ref[...] = _dot(jnp.where(ti > tj, 1.0, 0.0).astype(BF16), flags)
        rkr_ref[...] = _dot(flags_t, jnp.where(ti < tj, 1.0, 0.0).astype(BF16))

    @pl.when(j == 0)
    def _():
        rank_row = rkr_ref[pl.ds(e, 1), :]
        flag_row = gatet_ref[pl.ds(e, 1), :] > 0.0
        cnt = jnp.sum(jnp.where(flag_row, 1, 0))
        cnt_ref[0] = cnt

        def gather(s):
            slot = lax.broadcasted_iota(I32, (s, 1), 0).astype(F32)
            sel = jnp.where((rank_row == slot) & flag_row, 1.0, 0.0).astype(BF16)
            xe_ref[0:s, :] = _dot(sel, xn_ref[...]).astype(BF16)
            ye_ref[0:s, :] = jnp.zeros((s, D_MODEL), F32)

        for_block_size(cnt, gather)

    cnt = cnt_ref[0]

    def expert(s):
        xe = xe_ref[0:s, :]
        a = _silu(_dot(xe, wg_ref[0])) * _dot(xe, wu_ref[0])
        ye_ref[0:s, :] += _dot(a.astype(BF16), wd_ref[0])

    for_block_size(cnt, expert)

    @pl.when(j == last_j)
    def _():
        lane = lax.broadcasted_iota(I32, (tm, LANES), 1)
        gcol = jnp.sum(jnp.where(lane == e, gate_ref[...], 0.0), axis=1, keepdims=True)
        rcol = jnp.sum(jnp.where(lane == e, rkc_ref[...], 0.0), axis=1, keepdims=True)

        def scatter(s):
            slot = lax.broadcasted_iota(I32, (1, s), 1).astype(F32)
            sel_t = jnp.where((rcol == slot) & (gcol > 0.0), 1.0, 0.0).astype(BF16)
            ye = ye_ref[0:s, :]
            hi = ye.astype(BF16)
            lo = (ye - hi.astype(F32)).astype(BF16)
            o_ref[...] += gcol * (_dot(sel_t, hi) + _dot(sel_t, lo))

        for_block_size(cnt, scatter)


def moe_ffn(x, g, gate, wg, wu, wd):
    M = x.shape[0]
    tm = _pick(M, (1024, 512, 256, 128))
    tf = 896
    sizes = tuple(s for s in MOE_BLOCK_ROWS if s < tm) + (tm,)
    kern = functools.partial(_moe_kernel, tm=tm, sizes=sizes)
    return pl.pallas_call(
        kern,
        grid=(M // tm, N_EXPERTS, D_FF // tf),
        in_specs=[pl.BlockSpec((tm, D_MODEL), lambda i, e, j: (i, 0)),
                  pl.BlockSpec((1, D_MODEL), lambda i, e, j: (0, 0)),
                  pl.BlockSpec((tm, LANES), lambda i, e, j: (i, 0)),
                  pl.BlockSpec((LANES, tm), lambda i, e, j: (0, i)),
                  pl.BlockSpec((1, D_MODEL, tf), lambda i, e, j: (e, 0, j)),
                  pl.BlockSpec((1, D_MODEL, tf), lambda i, e, j: (e, 0, j)),
                  pl.BlockSpec((1, tf, D_MODEL), lambda i, e, j: (e, j, 0))],
        out_specs=pl.BlockSpec((tm, D_MODEL), lambda i, e, j: (i, 0)),
        out_shape=jax.ShapeDtypeStruct((M, D_MODEL), F32),
        scratch_shapes=[pltpu.VMEM((tm, D_MODEL), BF16),
                        pltpu.VMEM((tm, LANES), F32), pltpu.VMEM((LANES, tm), F32),
                        pltpu.VMEM((tm, D_MODEL), BF16), pltpu.VMEM((tm, D_MODEL), F32),
                        pltpu.SMEM((1,), I32)],
        compiler_params=pltpu.CompilerParams(
            dimension_semantics=("parallel", "arbitrary", "arbitrary"), vmem_limit_bytes=MOE_VMEM_LIMIT),
        name="moe_ffn",
    )(x, g.reshape(1, D_MODEL), gate, gate.T, wg, wu, wd)


def _rmsnorm_kernel(x_ref, g_ref, o_ref):
    x = x_ref[...]
    ms = jnp.mean(x * x, axis=-1, keepdims=True)
    o_ref[...] = x * lax.rsqrt(ms + EPS) * g_ref[...]


def rmsnorm(x, g):
    M = x.shape[0]
    tm = _pick(M, (1024, 512, 256, 128))
    return pl.pallas_call(
        _rmsnorm_kernel,
        grid=(M // tm,),
        in_specs=[pl.BlockSpec((tm, D_MODEL), lambda i: (i, 0)),
                  pl.BlockSpec((1, D_MODEL), lambda i: (0, 0))],
        out_specs=pl.BlockSpec((tm, D_MODEL), lambda i: (i, 0)),
        out_shape=jax.ShapeDtypeStruct((M, D_MODEL), F32),
        compiler_params=_cparams("parallel"),
        name="final_rmsnorm",
    )(x, g.reshape(1, D_MODEL))


def _split_w_in(w_in):
    offs = np.cumsum((0,) + IN_SPLITS)
    seg = {n: w_in[:, offs[i]:offs[i + 1]] for i, n in enumerate(
        ("z", "xbc", "dt", "q", "k", "v", "iq", "iw", "ik", "rq", "rk", "rv", "rg", "gates"))}
    cat = lambda *names: jnp.concatenate([seg[n] if isinstance(n, str) else n for n in names], axis=1)
    dt_wide = jnp.repeat(seg["dt"], SSM_HEAD_DIM, axis=1)
    small = cat("dt", "iw", jnp.zeros((D_MODEL, LANES - 2 * IDX_HEADS - IDX_DIM), w_in.dtype), "ik")
    q_scaled = seg["q"] * (ATT_HEAD_DIM ** -0.5)
    groups = dict(ssm=cat("z", "xbc", dt_wide), qiq=cat(q_scaled, "iq"), kvs=cat("k", "v", small),
                  ret=cat("rq", "rk", "rv", "rg"), gate=seg["gates"])
    return {n: w.astype(BF16) for n, w in groups.items()}


def _rope_tables(pos):
    half = RET_DK // 2
    inv = ROPE_BASE ** (-jnp.arange(half, dtype=F32) / half)
    ang = pos.astype(F32)[:, None] * inv[None, :]
    cos, sin = jnp.cos(ang), jnp.sin(ang)
    return jnp.concatenate([cos, cos], axis=1), jnp.concatenate([-sin, sin], axis=1)


def _pad_rows(a, L):
    return jnp.pad(a, ((0, 0), (0, L - a.shape[1]), (0, 0)))


def _mixers(h, pos0, conv0, ssm0, ret0, attend, lw):
    b, L, _ = h.shape
    hf = h.reshape(b * L, D_MODEL)
    proj = {n: rms_matmul(hf, lw["norm_g"], w).reshape(b, L, -1)
            for n, w in lw["w_in"].items() if n != "qiq"}
    qiq = rms_matmul_heads(hf, lw["norm_g"], lw["w_in"]["qiq"], ATT_HEAD_DIM)
    k = proj["kvs"][..., :KV_WIDTH]
    v = proj["kvs"][..., KV_WIDTH:2 * KV_WIDTH]
    small = proj["kvs"][..., 2 * KV_WIDTH:]
    ik = small[..., LANES - IDX_DIM:]

    Lp = -(-L // CHUNK) * CHUNK
    last_valid = L - (Lp - CHUNK)
    dtT = jnp.swapaxes(_pad_rows(small[..., :SSM_HEADS], Lp), 1, 2)
    s0 = jnp.transpose(ssm0, (0, 2, 1, 3)).reshape(b, SSM_STATE, SSM_INNER)
    ys, conv_new, s_new = ssd_branch(_pad_rows(proj["ssm"], Lp), dtT, conv0, s0, lw["conv_w"], lw["conv_b"],
                                     lw["dt_bias"], lw["a_log"], lw["d_skip"], lw["ssm_norm_g"], last_valid)
    ssm_new = jnp.transpose(s_new.reshape(b, SSM_STATE, SSM_HEADS, SSM_HEAD_DIM), (0, 2, 1, 3))

    cos2, sin2 = _rope_tables(pos0 + jnp.arange(Lp))
    yr, ret_new = ret_branch(_pad_rows(proj["ret"], Lp), cos2, sin2, ret0, last_valid)

    ya = attend(qiq, k, v, small, ik, b, L)

    ys = ys[:, :L].reshape(b * L, -1)
    yr = yr[:, :L].reshape(b * L, -1)
    out = merge_branches(ys, ya.reshape(b * L, -1), yr, proj["gate"].reshape(b * L, -1), hf,
                         lw["w_branch"], lw["w_out"])
    return out.reshape(b, L, D_MODEL), (k.reshape(b, L, ATT_KV_HEADS, ATT_HEAD_DIM),
                                        v.reshape(b, L, ATT_KV_HEADS, ATT_HEAD_DIM),
                                        ik, ssm_new, conv_new, ret_new)


def _attend_prompt(qiq, k, v, small, ik, b, L):
    k_hm = jnp.transpose(k.reshape(b, L, ATT_KV_HEADS, ATT_HEAD_DIM), (0, 2, 1, 3)).astype(BF16)
    v_t = jnp.transpose(v.reshape(b, L, ATT_KV_HEADS, ATT_HEAD_DIM), (0, 2, 3, 1)).astype(BF16)
    v_t = jnp.concatenate([v_t, jnp.ones((b, ATT_KV_HEADS, V_ROWS - ATT_HEAD_DIM, L), BF16)], axis=2)
    iw_t = jnp.swapaxes(small[..., IDX_HEADS:2 * IDX_HEADS], 1, 2)
    return dsa_prompt(qiq, iw_t, ik.astype(BF16), k_hm, v_t)


def _attend_sample(qiq, k, v, small, ik, b, L, *, ck_t, cv_t, cik_t, layer, page_table):
    rows = lambda a: jnp.transpose(a, (1, 0, 2)).reshape(b, L, ATT_HEADS * ATT_HEAD_DIM)
    iw = small[..., IDX_HEADS:2 * IDX_HEADS]
    return dsa_sample(rows(qiq[:ATT_HEADS]), k, v, rows(qiq[ATT_HEADS:]), iw, ik,
                      ck_t, cv_t, cik_t, layer, page_table)


def _channel_mixer(h, l, p):
    b, L, _ = h.shape
    hf = h.reshape(b * L, D_MODEL)
    j = l // 2
    if l % 2 == 0:
        out = ffn_dense(hf, p["norm_ffn_g"][l], p["w_ffn_gate"][j].astype(BF16),
                        p["w_ffn_up"][j].astype(BF16), p["w_ffn_down"][j].astype(BF16))
    else:
        gate = moe_router(hf, p["norm_ffn_g"][l], p["w_router"][j])
        out = moe_ffn(hf, p["norm_ffn_g"][l], gate, p["w_moe_gate"][j].astype(BF16),
                      p["w_moe_up"][j].astype(BF16), p["w_moe_down"][j].astype(BF16))
    return out.reshape(b, L, D_MODEL)


def kernel(x_prompt, x_sample, cache_k, cache_v, cache_idx_k, state_ssm, state_conv, state_ret,
           page_table, norm_mix_g, w_in, conv_w, conv_b, dt_bias, a_log, d_skip, ssm_norm_g,
           w_branch, w_out, norm_ffn_g, w_ffn_gate, w_ffn_up, w_ffn_down, w_router,
           w_moe_gate, w_moe_up, w_moe_down, final_norm_g):
    depth = w_in.shape[0]
    past = page_table.shape[1] * PAGE_SIZE
    bp = x_prompt.shape[0]
    p = dict(norm_ffn_g=norm_ffn_g, w_ffn_gate=w_ffn_gate, w_ffn_up=w_ffn_up, w_ffn_down=w_ffn_down,
             w_router=w_router, w_moe_gate=w_moe_gate, w_moe_up=w_moe_up, w_moe_down=w_moe_down)
    n_phys = cache_k.shape[1]
    ck_t = jnp.transpose(cache_k, (0, 1, 3, 4, 2)).reshape(depth, n_phys, KV_WIDTH, PAGE_SIZE)
    cv_t = jnp.transpose(cache_v, (0, 1, 3, 4, 2)).reshape(depth, n_phys, KV_WIDTH, PAGE_SIZE)
    cik_t = jnp.transpose(cache_idx_k, (0, 1, 3, 2))
    hp, hs = x_prompt, x_sample
    st_p, st_s = [], []
    for l in range(depth):
        lw = dict(norm_g=norm_mix_g[l], w_in=_split_w_in(w_in[l]), conv_w=conv_w[l], conv_b=conv_b[l],
                  dt_bias=dt_bias[l], a_log=a_log[l], d_skip=d_skip[l], ssm_norm_g=ssm_norm_g[l],
                  w_branch=w_branch[l].astype(BF16), w_out=w_out[l].astype(BF16))
        hp, sp = _mixers(hp, 0,
                         jnp.zeros((bp, CONV_WIDTH - 1, CONV_DIM), F32),
                         jnp.zeros((bp, SSM_HEADS, SSM_STATE, SSM_HEAD_DIM), F32),
                         jnp.zeros((bp, RET_HEADS, RET_DK, RET_DV), F32),
                         _attend_prompt, lw)
        attend_s = functools.partial(_attend_sample, ck_t=ck_t, cv_t=cv_t, cik_t=cik_t, layer=l,
                                     page_table=page_table)
        hs, ss = _mixers(hs, past, state_conv[l], state_ssm[l], state_ret[l], attend_s, lw)
        hp = _channel_mixer(hp, l, p)
        hs = _channel_mixer(hs, l, p)
        st_p.append(sp)
        st_s.append(ss)
    y_prompt = rmsnorm(hp.reshape(-1, D_MODEL), final_norm_g).reshape(hp.shape)
    y_sample = rmsnorm(hs.reshape(-1, D_MODEL), final_norm_g).reshape(hs.shape)
    stack = lambda sts, i: jnp.stack([s[i] for s in sts])
    return (y_prompt, y_sample,
            stack(st_p, 0), stack(st_p, 1), stack(st_p, 2), stack(st_p, 3), stack(st_p, 4), stack(st_p, 5),
            stack(st_s, 0), stack(st_s, 1), stack(st_s, 2), stack(st_s, 3), stack(st_s, 4), stack(st_s, 5))
```
